```python
import jax
import jax.numpy as jnp
from jax import lax
import numpy as np

D_MODEL = 2048
BATCH = 4
SEQ = 2048
DEPTH = 2

CHUNK = 64
EPS = 1e-6
MIX_WIDTH = D_MODEL
N_MIXERS = 4
GROUP_WIDTH = MIX_WIDTH // N_MIXERS

CONV_A_WIDTH = 3
POOL_WINDOWS = (2, 4, 8, 16)
POOL_GROUP = GROUP_WIDTH // len(POOL_WINDOWS)
GDN_HEADS = 4
GDN_HEAD_DIM = GROUP_WIDTH // GDN_HEADS
GDN_CONV = 4
SSM_HEAD_DIM = 64
SSM_HEADS = GROUP_WIDTH // SSM_HEAD_DIM
SSM_GROUPS = 2
SSM_STATE = 128
SSM_CONV = 4
SSM_XBC = GROUP_WIDTH + 2 * SSM_GROUPS * SSM_STATE

A_COLS = 3 * GROUP_WIDTH
B_COLS = GROUP_WIDTH
C_COLS = 4 * GROUP_WIDTH + 2 * GDN_HEADS
D_COLS = GROUP_WIDTH + SSM_XBC + SSM_HEADS
IN_COLS = A_COLS + B_COLS + C_COLS + D_COLS
IN_SPLITS = (A_COLS, A_COLS + B_COLS, A_COLS + B_COLS + C_COLS)

MEM_LEN = 256
XA_HEADS = 4
XA_HEAD_DIM = D_MODEL // XA_HEADS
D_FF = -(-8 * D_MODEL // (3 * 256)) * 256

MIX_PRE = 0
MIX_POST = 1
XA_PRE = 2
XA_POST = 3
MEM_NORM = 4
FFN_PRE = 5
FFN_POST = 6
N_NORMS = 7

kernel_name = 'hybrid_parallel_mixer_stream_encoder'


def rmsnorm(x, g):
    xf = x.astype(jnp.float32)
    y = xf * lax.rsqrt(jnp.mean(xf * xf, axis=-1, keepdims=True) + EPS)
    return (y * g.astype(jnp.float32)).astype(x.dtype)


def l2norm(x):
    return x * lax.rsqrt(jnp.sum(x * x, axis=-1, keepdims=True) + EPS)


def causal_dwconv(x, w):
    width, ch = w.shape
    return lax.conv_general_dilated(
        x, w[:, None, :], window_strides=(1,), padding=[(width - 1, 0)],
        dimension_numbers=('NWC', 'WIO', 'NWC'), feature_group_count=ch)


def short_conv_mixer(u, conv_w):
    b_gate, c_gate, h = jnp.split(u, 3, axis=-1)
    return b_gate * causal_dwconv(c_gate * h, conv_w)


def multiscale_pool_mixer(u, pool_w, pool_scale):
    bsz, T, _ = u.shape
    uf = u.astype(jnp.float32).reshape(bsz, T, len(POOL_WINDOWS), POOL_GROUP)
    cs = jnp.pad(jnp.cumsum(uf, axis=1), ((0, 0), (1, 0), (0, 0), (0, 0)))
    pos = jnp.arange(1, T + 1, dtype=jnp.float32)
    pooled = []
    for gi, w in enumerate(POOL_WINDOWS):
        cg = cs[:, :, gi]
        lo = jnp.pad(cg, ((0, 0), (w - 1, 0), (0, 0)))[:, :T]
        cnt = jnp.minimum(pos, float(w))
        pooled.append((cg[:, 1:] - lo) / cnt[None, :, None])
    pooled = jnp.stack(pooled, axis=2) - uf
    y = jnp.einsum('btgc,gcd->btgd', pooled, pool_w.astype(jnp.float32))
    y = y.reshape(bsz, T, GROUP_WIDTH) * pool_scale.astype(jnp.float32)
    return y.astype(u.dtype)


def gated_delta_rule_chunked(q, k, v, g, beta):
    bsz, T, H, dk = q.shape
    dv = v.shape[-1]
    n = T // CHUNK

    def to_blocks(t):
        t = t.astype(jnp.float32).reshape((bsz, n, CHUNK, H) + t.shape[3:])
        return jnp.moveaxis(t, 3, 1)

    q, k, v, g, beta = (to_blocks(t) for t in (q, k, v, g, beta))
    G = jnp.cumsum(g, axis=-1)
    incl = jnp.tril(jnp.ones((CHUNK, CHUNK), dtype=bool))
    strict = jnp.tril(jnp.ones((CHUNK, CHUNK), dtype=bool), k=-1)
    diff = G[..., :, None] - G[..., None, :]
    decay = jnp.where(incl, jnp.exp(jnp.where(incl, diff, 0.0)), 0.0)
    kb = k * beta[..., None]
    m_low = jnp.where(strict, jnp.einsum('bhnid,bhnjd->bhnij', kb, k) * decay, 0.0)
    eye = jnp.eye(CHUNK, dtype=jnp.float32)
    rhs = jnp.concatenate([v * beta[..., None], kb * jnp.exp(G)[..., None]], axis=-1)
    sol = lax.linalg.triangular_solve(eye + m_low, rhs, left_side=True, lower=True,
                                      unit_diagonal=True)
    u_blk, w_blk = sol[..., :dv], sol[..., dv:]
    qk = jnp.einsum('bhnid,bhnjd->bhnij', q, k) * decay
    q_dec = q * jnp.exp(G)[..., None]
    k_end = k * jnp.exp(G[..., -1:] - G)[..., None]
    blk_dec = jnp.exp(G[..., -1])
    xs = tuple(jnp.moveaxis(t, 2, 0) for t in (q_dec, qk, u_blk, w_blk, k_end, blk_dec))

    def step(S, inp):
        q_c, qk_c, u_c, w_c, k_c, d_c = inp
        v_new = u_c - jnp.einsum('bhlk,bhkv->bhlv', w_c, S)
        o_c = jnp.einsum('bhlk,bhkv->bhlv', q_c, S) + jnp.einsum('bhlm,bhmv->bhlv', qk_c, v_new)
        S = S * d_c[..., None, None] + jnp.einsum('bhlk,bhlv->bhkv', k_c, v_new)
        return S, o_c

    S0 = jnp.zeros((bsz, H, dk, dv), jnp.float32)
    _, o = lax.scan(step, S0, xs)
    return jnp.transpose(o, (1, 0, 3, 2, 4)).reshape(bsz, T, H, dv)


def gated_deltanet_mixer(u, conv_w, A_log, dt_bias, norm_g):
    bsz, T, _ = u.shape
    qkv, z, a, b = jnp.split(u, (3 * GROUP_WIDTH, 4 * GROUP_WIDTH, 4 * GROUP_WIDTH + GDN_HEADS), axis=-1)
    qkv = jax.nn.silu(causal_dwconv(qkv, conv_w))
    q, k, v = (t.astype(jnp.float32).reshape(bsz, T, GDN_HEADS, GDN_HEAD_DIM)
               for t in jnp.split(qkv, 3, axis=-1))
    q = l2norm(q) * GDN_HEAD_DIM ** -0.5
    k = l2norm(k)
    g = -jnp.exp(A_log.astype(jnp.float32)) * jax.nn.softplus(a.astype(jnp.float32) + dt_bias.astype(jnp.float32))
    beta = jax.nn.sigmoid(b.astype(jnp.float32))
    o = gated_delta_rule_chunked(q, k, v, g, beta)
    gate = jax.nn.silu(z.astype(jnp.float32).reshape(bsz, T, GDN_HEADS, GDN_HEAD_DIM))
    o = rmsnorm(o, norm_g) * gate
    return o.reshape(bsz, T, GROUP_WIDTH).astype(u.dtype)


def ssd_chunked(x, dt, A, Bm, Cm):
    bsz, T, H, P = x.shape
    n = T // CHUNK
    hg = H // SSM_GROUPS
    X = (x.astype(jnp.float32) * dt[..., None]).reshape(bsz, n, CHUNK, SSM_GROUPS, hg, P)
    a = jnp.transpose((dt * A).reshape(bsz, n, CHUNK, SSM_GROUPS, hg), (0, 3, 4, 1, 2))
    Bm = Bm.astype(jnp.float32).reshape(bsz, n, CHUNK, SSM_GROUPS, SSM_STATE)
    Cm = Cm.astype(jnp.float32).reshape(bsz, n, CHUNK, SSM_GROUPS, SSM_STATE)
    Acs = jnp.cumsum(a, axis=-1)
    incl = jnp.tril(jnp.ones((CHUNK, CHUNK), dtype=bool))
    seg = Acs[..., :, None] - Acs[..., None, :]
    Lm = jnp.where(incl, jnp.exp(jnp.where(incl, seg, 0.0)), 0.0)
    CB = jnp.einsum('bnlgs,bnmgs->bgnlm', Cm, Bm)
    y_diag = jnp.einsum('bghnlm,bnmghp->bnlghp', CB[:, :, None] * Lm, X)
    to_end = jnp.exp(Acs[..., -1:] - Acs)
    states = jnp.einsum('bnlgs,bghnl,bnlghp->nbghps', Bm, to_end, X)
    blk_dec = jnp.moveaxis(jnp.exp(Acs[..., -1]), 3, 0)

    def step(h, inp):
        s_c, d_c = inp
        return h * d_c[..., None, None] + s_c, h

    h0 = jnp.zeros((bsz, SSM_GROUPS, hg, P, SSM_STATE), jnp.float32)
    _, h_prev = lax.scan(step, h0, (states, blk_dec))
    y_off = jnp.einsum('bnlgs,nbghps,bghnl->bnlghp', Cm, h_prev, jnp.exp(Acs))
    return (y_diag + y_off).reshape(bsz, T, H, P)


def mamba2_mixer(u, conv_w, conv_b, A_log, dt_bias, D_skip, norm_g):
    bsz, T, _ = u.shape
    z, xBC, dt = jnp.split(u, (GROUP_WIDTH, GROUP_WIDTH + SSM_XBC), axis=-1)
    xBC = jax.nn.silu(causal_dwconv(xBC, conv_w) + conv_b)
    x, Bm, Cm = jnp.split(xBC, (GROUP_WIDTH, GROUP_WIDTH + SSM_GROUPS * SSM_STATE), axis=-1)
    x = x.reshape(bsz, T, SSM_HEADS, SSM_HEAD_DIM)
    Bm = Bm.reshape(bsz, T, SSM_GROUPS, SSM_STATE)
    Cm = Cm.reshape(bsz, T, SSM_GROUPS, SSM_STATE)
    dt = jax.nn.softplus(dt.astype(jnp.float32) + dt_bias.astype(jnp.float32))
    A = -jnp.exp(A_log.astype(jnp.float32))
    y = ssd_chunked(x, dt, A, Bm, Cm) + D_skip.astype(jnp.float32)[:, None] * x.astype(jnp.float32)
    y = y.reshape(bsz, T, GROUP_WIDTH) * jax.nn.silu(z.astype(jnp.float32))
    y = rmsnorm(y.reshape(bsz, T, SSM_GROUPS, -1), norm_g.reshape(SSM_GROUPS, -1))
    return y.reshape(bsz, T, GROUP_WIDTH).astype(u.dtype)


def memory_cross_attention(h, m, wq, wkv, wo):
    bsz, T, _ = h.shape
    q = (h @ wq).reshape(bsz, T, XA_HEADS, XA_HEAD_DIM)
    k, v = jnp.split(m @ wkv, 2, axis=-1)
    k = k.reshape(bsz, m.shape[1], XA_HEADS, XA_HEAD_DIM)
    v = v.reshape(bsz, m.shape[1], XA_HEADS, XA_HEAD_DIM)
    s = jnp.einsum('bthd,bmhd->bhtm', q, k).astype(jnp.float32) * XA_HEAD_DIM ** -0.5
    p = jax.nn.softmax(s, axis=-1).astype(v.dtype)
    o = jnp.einsum('bhtm,bmhd->bthd', p, v).reshape(bsz, T, D_MODEL)
    return o @ wo


def swiglu_ffn(h, w_gu, w_down):
    gate, up = jnp.split(h @ w_gu, 2, axis=-1)
    return (jax.nn.silu(gate) * up) @ w_down


def setup_inputs(seed: int = 0) -> dict:
    key = jax.random.key(seed)
    ks = jax.random.split(key, 24)
    f32 = jnp.float32

    def dense(k, shape, fan_in):
        return jax.random.normal(k, shape, f32) * fan_in ** -0.5

    def gains(k, shape, noise):
        return 1.0 + noise * jax.random.normal(k, shape, f32)

    def log_a(k, n_heads):
        return jnp.log(jax.random.uniform(k, (DEPTH, n_heads), f32, 1.0, 16.0))

    def dt_bias(k, n_heads):
        dt = jnp.exp(jax.random.uniform(k, (DEPTH, n_heads), f32, np.log(1e-3), np.log(1e-1)))
        return dt + jnp.log(-jnp.expm1(-dt))

    return {
        'x': jax.random.normal(ks[0], (BATCH, SEQ, D_MODEL), f32),
        'mem': jax.random.normal(ks[1], (BATCH, MEM_LEN, D_MODEL), f32),
        'norm_g': gains(ks[2], (DEPTH, N_NORMS, D_MODEL), 0.05),
        'w_in': dense(ks[3], (DEPTH, D_MODEL, IN_COLS), D_MODEL),
        'conv_a_w': dense(ks[4], (DEPTH, CONV_A_WIDTH, GROUP_WIDTH), CONV_A_WIDTH),
        'pool_w': dense(ks[5], (DEPTH, len(POOL_WINDOWS), POOL_GROUP, POOL_GROUP), POOL_GROUP),
        'pool_scale': gains(ks[6], (DEPTH, GROUP_WIDTH), 0.1),
        'gdn_conv_w': dense(ks[7], (DEPTH, GDN_CONV, 3 * GROUP_WIDTH), GDN_CONV),
        'gdn_A_log': log_a(ks[8], GDN_HEADS),
        'gdn_dt_bias': dt_bias(ks[9], GDN_HEADS),
        'gdn_norm_g': gains(ks[10], (DEPTH, GDN_HEAD_DIM), 0.05),
        'ssm_conv_w': dense(ks[11], (DEPTH, SSM_CONV, SSM_XBC), SSM_CONV),
        'ssm_conv_b': 0.02 * jax.random.normal(ks[12], (DEPTH, SSM_XBC), f32),
        'ssm_A_log': log_a(ks[13], SSM_HEADS),
        'ssm_dt_bias': dt_bias(ks[14], SSM_HEADS),
        'ssm_D': gains(ks[15], (DEPTH, SSM_HEADS), 0.1),
        'ssm_norm_g': gains(ks[16], (DEPTH, GROUP_WIDTH), 0.05),
        'w_out': dense(ks[17], (DEPTH, MIX_WIDTH, D_MODEL), MIX_WIDTH),
        'xa_wq': dense(ks[18], (DEPTH, D_MODEL, D_MODEL), D_MODEL),
        'xa_wkv': dense(ks[19], (DEPTH, D_MODEL, 2 * D_MODEL), D_MODEL),
        'xa_wo': dense(ks[20], (DEPTH, D_MODEL, D_MODEL), D_MODEL),
        'ffn_w_gu': dense(ks[21], (DEPTH, D_MODEL, 2 * D_FF), D_MODEL),
        'ffn_w_down': dense(ks[22], (DEPTH, D_FF, D_MODEL), D_FF),
    }


def reference(x, mem, norm_g, w_in, conv_a_w, pool_w, pool_scale, gdn_conv_w, gdn_A_log,
              gdn_dt_bias, gdn_norm_g, ssm_conv_w, ssm_conv_b, ssm_A_log, ssm_dt_bias, ssm_D,
              ssm_norm_g, w_out, xa_wq, xa_wkv, xa_wo, ffn_w_gu, ffn_w_down):
    for l in range(DEPTH):
        g = norm_g[l]
        u = rmsnorm(x, g[MIX_PRE]) @ w_in[l]
        u_a, u_b, u_c, u_d = jnp.split(u, IN_SPLITS, axis=-1)
        mixed = jnp.concatenate([
            short_conv_mixer(u_a, conv_a_w[l]),
            multiscale_pool_mixer(u_b, pool_w[l], pool_scale[l]),
            gated_deltanet_mixer(u_c, gdn_conv_w[l], gdn_A_log[l], gdn_dt_bias[l], gdn_norm_g[l]),
            mamba2_mixer(u_d, ssm_conv_w[l], ssm_conv_b[l], ssm_A_log[l], ssm_dt_bias[l],
                         ssm_D[l], ssm_norm_g[l]),
        ], axis=-1).astype(x.dtype)
        x = x + rmsnorm(mixed @ w_out[l], g[MIX_POST])
        xa = memory_cross_attention(rmsnorm(x, g[XA_PRE]), rmsnorm(mem, g[MEM_NORM]),
                                    xa_wq[l], xa_wkv[l], xa_wo[l])
        x = x + rmsnorm(xa, g[XA_POST])
        x = x + rmsnorm(swiglu_ffn(rmsnorm(x, g[FFN_PRE]), ffn_w_gu[l], ffn_w_down[l]), g[FFN_POST])
    return x
```

```python
import functools

import jax
import jax.numpy as jnp
from jax import lax
from jax.experimental import pallas as pl
from jax.experimental.pallas import tpu as pltpu

F32 = jnp.float32
BF16 = jnp.bfloat16
EPS = 1e-6

D_MODEL = 2048
GROUP_WIDTH = 512
CHUNK = 64
POOL_WINDOWS = (2, 4, 8, 16)
POOL_GROUP = 128
GDN_HEADS = 4
GDN_HEAD_DIM = 128
SSM_HEADS = 8
SSM_HEAD_DIM = 64
SSM_GROUPS = 2
SSM_STATE = 128
SSM_XBC = 1024
MEM_LEN = 256
XA_HEADS = 4
XA_HEAD_DIM = 512
D_FF = 5632

OFF_A = 0
OFF_B = 1536
OFF_QKV = 2048
OFF_CZ = 3584
OFF_DZ = 4096
OFF_XBC = 4608
OFF_SMALL = 5632
IN_COLS_PADDED = 5760
SMALL_A = 0
SMALL_B = 4
SMALL_DT = 8
HALO = 16

VMEM_LIMIT = 48 * 1024 * 1024


def _rmsnorm(x, g):
    return x * lax.rsqrt(jnp.mean(x * x, axis=-1, keepdims=True) + EPS) * g


def _silu(x):
    return x * jax.nn.sigmoid(x)


def _softplus(x):
    return jnp.maximum(x, 0.0) + jnp.log1p(jnp.exp(-jnp.abs(x)))


def _dot(a, b):
    return jnp.dot(a.astype(BF16), b.astype(BF16), preferred_element_type=F32)


def _dot_nt(a, b):
    return lax.dot_general(a.astype(BF16), b.astype(BF16), (((1,), (1,)), ((), ())),
                           preferred_element_type=F32)


def _dot_tn(a, b):
    return lax.dot_general(a.astype(BF16), b.astype(BF16), (((0,), (0,)), ((), ())),
                           preferred_element_type=F32)


def _hdot(a, b):
    return jnp.dot(a, b, preferred_element_type=F32, precision=lax.Precision.HIGHEST)


def _norm_matmul_kernel(x_ref, g_ref, w_ref, o_ref, xn_ref):
    @pl.when(pl.program_id(1) == 0)
    def _():
        xn_ref[...] = _rmsnorm(x_ref[...], g_ref[...]).astype(BF16)

    o_ref[...] = jnp.dot(xn_ref[...], w_ref[...], preferred_element_type=F32).astype(o_ref.dtype)


def _norm_matmul(x, g, w, out_dtype, tm, tn):
    n, k = x.shape
    m = w.shape[1]
    return pl.pallas_call(
        _norm_matmul_kernel,
        grid=(n // tm, m // tn),
        in_specs=[
            pl.BlockSpec((tm, k), lambda i, j: (i, 0)),
            pl.BlockSpec((1, k), lambda i, j: (0, 0)),
            pl.BlockSpec((k, tn), lambda i, j: (0, j)),
        ],
        out_specs=pl.BlockSpec((tm, tn), lambda i, j: (i, j)),
        out_shape=jax.ShapeDtypeStruct((n, m), out_dtype),
        scratch_shapes=[pltpu.VMEM((tm, k), BF16)],
        compiler_params=pltpu.CompilerParams(
            dimension_semantics=("arbitrary", "arbitrary"), vmem_limit_bytes=VMEM_LIMIT),
        name="norm_matmul",
    )(x, g.reshape(1, k), w)


def _proj_post_kernel(a_ref, w_ref, g_ref, x_ref, o_ref):
    y = jnp.dot(a_ref[...], w_ref[...], preferred_element_type=F32)
    o_ref[...] = x_ref[...] + _rmsnorm(y, g_ref[...])


def _proj_post(a, w, g, x, tm):
    n, k = a.shape
    d = w.shape[1]
    return pl.pallas_call(
        _proj_post_kernel,
        grid=(n // tm,),
        in_specs=[
            pl.BlockSpec((tm, k), lambda i: (i, 0)),
            pl.BlockSpec((k, d), lambda i: (0, 0)),
            pl.BlockSpec((1, d), lambda i: (0, 0)),
            pl.BlockSpec((tm, d), lambda i: (i, 0)),
        ],
        out_specs=pl.BlockSpec((tm, d), lambda i: (i, 0)),
        out_shape=jax.ShapeDtypeStruct((n, d), F32),
        compiler_params=pltpu.CompilerParams(
            dimension_semantics=("arbitrary",), vmem_limit_bytes=VMEM_LIMIT),
        name="proj_post",
    )(a, w, g.reshape(1, d), x)


def _shift_rows(x, k):
    return pltpu.roll(x, k, axis=0) if k else x


def _causal_conv(x_ext, w_ref):
    width = w_ref.shape[0]
    acc = x_ext * w_ref[width - 1:width, :]
    for k in range(width - 1):
        acc = acc + _shift_rows(x_ext, width - 1 - k) * w_ref[k:k + 1, :]
    return acc


def _expand_heads(cols, first, n_heads, width):
    rows = cols.shape[0]
    per_vreg = 128 // width
    lane = lax.broadcasted_iota(jnp.int32, (rows, 128), 1)
    pieces = []
    for m in range(n_heads // per_vreg):
        piece = jnp.broadcast_to(cols[:, first + m * per_vreg:first + m * per_vreg + 1], (rows, 128))
        for j in range(1, per_vreg):
            c = first + m * per_vreg + j
            piece = jnp.where(lane >= j * width, jnp.broadcast_to(cols[:, c:c + 1], (rows, 128)), piece)
        pieces.append(piece)
    return jnp.concatenate(pieces, axis=1)


def _unit_lower_inverse(m, row, col):
    same16 = (row >> 4) == (col >> 4)
    same32 = (row >> 5) == (col >> 5)
    eye = (row == col).astype(F32)
    d = jnp.where(same16, m, 0.0)
    d2 = _hdot(d, d)
    d4 = _hdot(d2, d2)
    d8 = _hdot(d4, d4)
    p = eye - d
    p = p + _hdot(p, d2)
    p = p + _hdot(p, d4)
    p = p + _hdot(p, d8)
    l32 = jnp.where(same32 & jnp.logical_not(same16), m, 0.0)
    q = p - _hdot(_hdot(p, l32), p)
    l64 = jnp.where(same32, 0.0, m)
    return q - _hdot(_hdot(q, l64), q)


def _mixer_kernel(ucur_ref, uprev_ref, conv_a_ref, pool_w_ref, pool_scale_ref, gdn_conv_ref,
                  gdn_norm_ref, ssm_conv_ref, ssm_bias_ref, ssm_d_ref, ssm_norm_ref,
                  alog_ref, dtbias_ref, out_ref, s_ref, h_ref, *, tile):
    i = pl.program_id(1)

    @pl.when(i == 0)
    def _():
        s_ref[...] = jnp.zeros_like(s_ref)
        h_ref[...] = jnp.zeros_like(h_ref)

    keep_prev = (i > 0).astype(F32)

    def ext(off, width):
        prev = uprev_ref[:, off:off + width] * keep_prev
        return jnp.concatenate([prev, ucur_ref[:, off:off + width]], axis=0)

    p = ext(OFF_A + GROUP_WIDTH, GROUP_WIDTH) * ext(OFF_A + 2 * GROUP_WIDTH, GROUP_WIDTH)
    ya = ucur_ref[:, OFF_A:OFF_A + GROUP_WIDTH] * _causal_conv(p, conv_a_ref)[HALO:]
    out_ref[:, 0:GROUP_WIDTH] = ya.astype(out_ref.dtype)

    pos = (i * tile + 1 + lax.broadcasted_iota(jnp.int32, (tile, 1), 0)).astype(F32)
    for gi, window in enumerate(POOL_WINDOWS):
        off = OFF_B + gi * POOL_GROUP
        s = ext(off, POOL_GROUP)
        k = 1
        while k < window:
            s = s + _shift_rows(s, k)
            k *= 2
        pooled = s[HALO:] / jnp.minimum(pos, float(window)) - ucur_ref[:, off:off + POOL_GROUP]
        yb = _dot(pooled, pool_w_ref[gi]) * pool_scale_ref[:, gi * POOL_GROUP:(gi + 1) * POOL_GROUP]
        out_ref[:, GROUP_WIDTH + gi * POOL_GROUP:GROUP_WIDTH + (gi + 1) * POOL_GROUP] = yb.astype(out_ref.dtype)

    small = ucur_ref[:, OFF_SMALL:OFF_SMALL + 128]
    sp = _softplus(small + dtbias_ref[...])
    rate = -jnp.exp(alog_ref[...]) * sp
    beta_all = jax.nn.sigmoid(small)

    qkv = _silu(_causal_conv(ext(OFF_QKV, 3 * GROUP_WIDTH), gdn_conv_ref)[HALO:])
    xbc = _silu(_causal_conv(ext(OFF_XBC, SSM_XBC), ssm_conv_ref)[HALO:] + ssm_bias_ref[...])

    row = lax.broadcasted_iota(jnp.int32, (CHUNK, CHUNK), 0)
    col = lax.broadcasted_iota(jnp.int32, (CHUNK, CHUNK), 1)
    incl = row >= col
    strict = row > col
    ltri = incl.astype(F32)
    lane256 = lax.broadcasted_iota(jnp.int32, (CHUNK, 4 * SSM_HEAD_DIM), 1)

    for c in range(tile // CHUNK):
        r0, r1 = c * CHUNK, (c + 1) * CHUNK
        acs = _hdot(ltri, rate[r0:r1])
        acs_t = acs.T
        acs_last = acs[CHUNK - 1:CHUNK, :]
        e_acs = jnp.exp(acs)
        e_end = jnp.exp(acs_last - acs)
        e_blk = jnp.exp(acs_last)

        for h in range(GDN_HEADS):
            lo = h * GDN_HEAD_DIM
            qh = qkv[r0:r1, lo:lo + GDN_HEAD_DIM]
            kh = qkv[r0:r1, GROUP_WIDTH + lo:GROUP_WIDTH + lo + GDN_HEAD_DIM]
            vh = qkv[r0:r1, 2 * GROUP_WIDTH + lo:2 * GROUP_WIDTH + lo + GDN_HEAD_DIM]
            qh = qh * lax.rsqrt(jnp.sum(qh * qh, axis=-1, keepdims=True) + EPS) * GDN_HEAD_DIM ** -0.5
            kh = kh * lax.rsqrt(jnp.sum(kh * kh, axis=-1, keepdims=True) + EPS)
            cg = SMALL_A + h
            beta = beta_all[r0:r1, SMALL_B + h:SMALL_B + h + 1]
            diff = acs[:, cg:cg + 1] - acs_t[cg:cg + 1, :]
            decay = jnp.where(incl, jnp.exp(jnp.where(incl, diff, 0.0)), 0.0)
            e_g = e_acs[:, cg:cg + 1]
            kb = kh * beta
            m_low = jnp.where(strict, _dot_nt(kb, kh) * decay, 0.0)
            t_inv = _unit_lower_inverse(m_low, row, col)
            sol = _hdot(t_inv, jnp.concatenate([vh * beta, kb * e_g], axis=1))
            u_blk = sol[:, :GDN_HEAD_DIM]
            w_blk = sol[:, GDN_HEAD_DIM:]
            qk = _dot_nt(qh, kh) * decay
            state = s_ref[h]
            v_new = u_blk - _dot(w_blk, state)
            o = _dot(qh * e_g, state) + _dot(qk, v_new)
            s_ref[h] = state * e_blk[:, cg:cg + 1] + _dot_tn(kh * e_end[:, cg:cg + 1], v_new)
            o = _rmsnorm(o, gdn_norm_ref[...])
            o = o * _silu(ucur_ref[r0:r1, OFF_CZ + lo:OFF_CZ + lo + GDN_HEAD_DIM])
            out_ref[r0:r1, 2 * GROUP_WIDTH + lo:2 * GROUP_WIDTH + lo + GDN_HEAD_DIM] = o.astype(out_ref.dtype)

        dt_x = _expand_heads(sp[r0:r1], SMALL_DT, SSM_HEADS, SSM_HEAD_DIM)
        e_acs_x = _expand_heads(e_acs, SMALL_DT, SSM_HEADS, SSM_HEAD_DIM)
        e_end_x = _expand_heads(e_end, SMALL_DT, SSM_HEADS, SSM_HEAD_DIM)
        e_blk_x = _expand_heads(e_blk, SMALL_DT, SSM_HEADS, SSM_HEAD_DIM)
        x_in = xbc[r0:r1, :GROUP_WIDTH]
        x_dt = x_in * dt_x
        x_end = x_dt * e_end_x
        hg = SSM_HEADS // SSM_GROUPS
        gw = hg * SSM_HEAD_DIM
        ys = []
        for g in range(SSM_GROUPS):
            b_g = xbc[r0:r1, GROUP_WIDTH + g * SSM_STATE:GROUP_WIDTH + (g + 1) * SSM_STATE]
            c_g = xbc[r0:r1, GROUP_WIDTH + (SSM_GROUPS + g) * SSM_STATE:
                      GROUP_WIDTH + (SSM_GROUPS + g + 1) * SSM_STATE]
            cb = _dot_nt(c_g, b_g)
            x_g = x_dt[:, g * gw:(g + 1) * gw]
            state = h_ref[g]
            y_g = _dot(c_g, state) * e_acs_x[:, g * gw:(g + 1) * gw]
            for j in range(hg):
                ch = SMALL_DT + g * hg + j
                seg = acs[:, ch:ch + 1] - acs_t[ch:ch + 1, :]
                l_m = jnp.where(incl, jnp.exp(jnp.where(incl, seg, 0.0)), 0.0)
                x_h = jnp.where((lane256 >= j * SSM_HEAD_DIM) & (lane256 < (j + 1) * SSM_HEAD_DIM), x_g, 0.0)
                y_g = y_g + _dot(cb * l_m, x_h)
            h_ref[g] = state * e_blk_x[:, g * gw:(g + 1) * gw] + _dot_tn(b_g, x_end[:, g * gw:(g + 1) * gw])
            ys.append(y_g)
        y = jnp.concatenate(ys, axis=1) + ssm_d_ref[...] * x_in
        y = y * _silu(ucur_ref[r0:r1, OFF_DZ:OFF_DZ + GROUP_WIDTH])
        for g in range(SSM_GROUPS):
            yn = _rmsnorm(y[:, g * gw:(g + 1) * gw], ssm_norm_ref[:, g * gw:(g + 1) * gw])
            out_ref[r0:r1, 3 * GROUP_WIDTH + g * gw:3 * GROUP_WIDTH + (g + 1) * gw] = yn.astype(out_ref.dtype)


def _mixers(u, batch, seq, conv_a_w, pool_w, pool_scale, gdn_conv_w, gdn_norm_g, ssm_conv_w, ssm_conv_b,
            ssm_d_x, ssm_norm_g, alog_row, dtbias_row, tile):
    n_tiles = seq // tile

    def const(shape):
        return pl.BlockSpec(shape, lambda b, i: (0,) * len(shape))

    return pl.pallas_call(
        functools.partial(_mixer_kernel, tile=tile),
        grid=(batch, n_tiles),
        in_specs=[
            pl.BlockSpec((tile, IN_COLS_PADDED), lambda b, i: (b * n_tiles + i, 0)),
            pl.BlockSpec((HALO, IN_COLS_PADDED),
                         lambda b, i: (jnp.maximum((b * n_tiles + i) * (tile // HALO) - 1, 0), 0)),
            const(conv_a_w.shape), const(pool_w.shape), const(pool_scale.shape), const(gdn_conv_w.shape),
            const(gdn_norm_g.shape), const(ssm_conv_w.shape), const(ssm_conv_b.shape), const(ssm_d_x.shape),
            const(ssm_norm_g.shape), const(alog_row.shape), const(dtbias_row.shape),
        ],
        out_specs=pl.BlockSpec((tile, D_MODEL), lambda b, i: (b * n_tiles + i, 0)),
        out_shape=jax.ShapeDtypeStruct((batch * seq, D_MODEL), BF16),
        scratch_shapes=[
            pltpu.VMEM((GDN_HEADS, GDN_HEAD_DIM, GDN_HEAD_DIM), F32),
            pltpu.VMEM((SSM_GROUPS, SSM_STATE, (SSM_HEADS // SSM_GROUPS) * SSM_HEAD_DIM), F32),
        ],
        compiler_params=pltpu.CompilerParams(
            dimension_semantics=("arbitrary", "arbitrary"), vmem_limit_bytes=VMEM_LIMIT),
        name="mixers",
    )(u, u, conv_a_w, pool_w, pool_scale, gdn_conv_w, gdn_norm_g, ssm_conv_w, ssm_conv_b, ssm_d_x,
      ssm_norm_g, alog_row, dtbias_row)


def _xattn_kernel(x_ref, g_ref, wq_ref, kv_ref, o_ref):
    xn = _rmsnorm(x_ref[...], g_ref[...]).astype(BF16)
    q = jnp.dot(xn, wq_ref[...], preferred_element_type=F32)
    for h in range(XA_HEADS):
        lo = h * XA_HEAD_DIM
        k_h = kv_ref[:, lo:lo + XA_HEAD_DIM]
        v_h = kv_ref[:, D_MODEL + lo:D_MODEL + lo + XA_HEAD_DIM]
        s = _dot_nt(q[:, lo:lo + XA_HEAD_DIM], k_h) * XA_HEAD_DIM ** -0.5
        e = jnp.exp(s - jnp.max(s, axis=-1, keepdims=True))
        p = e / jnp.sum(e, axis=-1, keepdims=True)
        o_ref[:, lo:lo + XA_HEAD_DIM] = _dot(p, v_h).astype(o_ref.dtype)


def _xattn(x, g, wq, kv, batch, seq, tq):
    n, d = x.shape
    n_tiles = seq // tq
    return pl.pallas_call(
        _xattn_kernel,
        grid=(batch, n_tiles),
        in_specs=[
            pl.BlockSpec((tq, d), lambda b, i: (b * n_tiles + i, 0)),
            pl.BlockSpec((1, d), lambda b, i: (0, 0)),
            pl.BlockSpec((d, d), lambda b, i: (0, 0)),
            pl.BlockSpec((MEM_LEN, 2 * d), lambda b, i: (b, 0)),
        ],
        out_specs=pl.BlockSpec((tq, d), lambda b, i: (b * n_tiles + i, 0)),
        out_shape=jax.ShapeDtypeStruct((n, d), BF16),
        compiler_params=pltpu.CompilerParams(
            dimension_semantics=("arbitrary", "arbitrary"), vmem_limit_bytes=VMEM_LIMIT),
        name="xattn",
    )(x, g.reshape(1, d), wq, kv)


def _ffn_kernel(x_ref, gpre_ref, wg_ref, wu_ref, wd_ref, gpost_ref, o_ref, xn_ref, acc_ref):
    j = pl.program_id(1)

    @pl.when(j == 0)
    def _():
        xn_ref[...] = _rmsnorm(x_ref[...], gpre_ref[...]).astype(BF16)
        acc_ref[...] = jnp.zeros_like(acc_ref)

    xn = xn_ref[...]
    gate = jnp.dot(xn, wg_ref[...], preferred_element_type=F32)
    up = jnp.dot(xn, wu_ref[...], preferred_element_type=F32)
    acc_ref[...] += jnp.dot((_silu(gate) * up).astype(BF16), wd_ref[...], preferred_element_type=F32)

    @pl.when(j == pl.num_programs(1) - 1)
    def _():
        o_ref[...] = x_ref[...] + _rmsnorm(acc_ref[...], gpost_ref[...])


def _ffn(x, gpre, w_gu, w_down, gpost, tm, tf):
    n, d = x.shape
    n_ff = D_FF // tf
    return pl.pallas_call(
        _ffn_kernel,
        grid=(n // tm, n_ff),
        in_specs=[
            pl.BlockSpec((tm, d), lambda i, j: (i, 0)),
            pl.BlockSpec((1, d), lambda i, j: (0, 0)),
            pl.BlockSpec((d, tf), lambda i, j: (0, j)),
            pl.BlockSpec((d, tf), lambda i, j: (0, j + n_ff)),
            pl.BlockSpec((tf, d), lambda i, j: (j, 0)),
            pl.BlockSpec((1, d), lambda i, j: (0, 0)),
        ],
        out_specs=pl.BlockSpec((tm, d), lambda i, j: (i, 0)),
        out_shape=jax.ShapeDtypeStruct((n, d), F32),
        scratch_shapes=[pltpu.VMEM((tm, d), BF16), pltpu.VMEM((tm, d), F32)],
        compiler_params=pltpu.CompilerParams(
            dimension_semantics=("arbitrary", "arbitrary"), vmem_limit_bytes=VMEM_LIMIT),
        name="ffn",
    )(x, gpre.reshape(1, d), w_gu, w_gu, w_down, gpost.reshape(1, d))


def _rearrange_w_in(w):
    c_small = 4096
    d_start = 4104
    dt_start = 5640
    pad = IN_COLS_PADDED - OFF_SMALL - 16
    return jnp.concatenate([
        w[:, :c_small], w[:, d_start:dt_start], w[:, c_small:d_start], w[:, dt_start:],
        jnp.zeros((w.shape[0], pad), w.dtype)], axis=1).astype(BF16)


def _small_row(gdn_vals, ssm_vals):
    row = jnp.zeros((1, 128), F32)
    row = row.at[0, SMALL_A:SMALL_A + GDN_HEADS].set(gdn_vals.astype(F32))
    return row.at[0, SMALL_DT:SMALL_DT + SSM_HEADS].set(ssm_vals.astype(F32))


def kernel(x, mem, norm_g, w_in, conv_a_w, pool_w, pool_scale, gdn_conv_w, gdn_A_log, gdn_dt_bias, gdn_norm_g,
           ssm_conv_w, ssm_conv_b, ssm_A_log, ssm_dt_bias, ssm_D, ssm_norm_g, w_out, xa_wq, xa_wkv, xa_wo,
           ffn_w_gu, ffn_w_down):
    batch, seq, d = x.shape
    depth = w_in.shape[0]
    xf = x.reshape(batch * seq, d)
    memf = mem.reshape(batch * MEM_LEN, d)
    for l in range(depth):
        g = norm_g[l]
        u = _norm_matmul(xf, g[0], _rearrange_w_in(w_in[l]), F32, tm=512, tn=1152)
        mixed = _mixers(
            u, batch, seq, conv_a_w[l], pool_w[l].astype(BF16), pool_scale[l].reshape(1, -1), gdn_conv_w[l],
            gdn_norm_g[l].reshape(1, -1), ssm_conv_w[l], ssm_conv_b[l].reshape(1, -1),
            jnp.repeat(ssm_D[l], SSM_HEAD_DIM).reshape(1, -1), ssm_norm_g[l].reshape(1, -1),
            _small_row(gdn_A_log[l], ssm_A_log[l]), _small_row(gdn_dt_bias[l], ssm_dt_bias[l]), tile=CHUNK)
        xf = _proj_post(mixed, w_out[l].astype(BF16), g[1], xf, tm=512)
        kv = _norm_matmul(memf, g[4], xa_wkv[l].astype(BF16), BF16, tm=512, tn=1024)
        att = _xattn(xf, g[2], xa_wq[l].astype(BF16), kv, batch, seq, tq=512)
        xf = _proj_post(att, xa_wo[l].astype(BF16), g[3], xf, tm=512)
        xf = _ffn(xf, g[5], ffn_w_gu[l].astype(BF16), ffn_w_down[l].astype(BF16), g[6], tm=512, tf=512)
    return xf.reshape(batch, seq, d)
```

```python
import functools

import jax
import jax.numpy as jnp
from jax import lax
from jax.experimental import pallas as pl
from jax.experimental.pallas import tpu as pltpu

F32 = jnp.float32
BF16 = jnp.bfloat16
EPS = 1e-6

D_MODEL = 2048
GROUP_WIDTH = 512
CHUNK = 64
POOL_WINDOWS = (2, 4, 8, 16)
POOL_GROUP = 128
GDN_HEADS = 4
GDN_HEAD_DIM = 128
SSM_HEADS = 8
SSM_HEAD_DIM = 64
SSM_GROUPS = 2
SSM_STATE = 128
SSM_XBC = 1024
MEM_LEN = 256
XA_HEADS = 4
XA_HEAD_DIM = 512
D_FF = 5632

OFF_A = 0
OFF_B = 1536
OFF_QKV = 2048
OFF_CZ = 3584
OFF_DZ = 4096
OFF_XBC = 4608
OFF_SMALL = 5632
IN_COLS_PADDED = 5760
SMALL_A = 0
SMALL_B = 4
SMALL_DT = 8
HALO = 16

VMEM_LIMIT = 48 * 1024 * 1024


def _rmsnorm(x, g):
    return x * lax.rsqrt(jnp.mean(x * x, axis=-1, keepdims=True) + EPS) * g


def _silu(x):
    return x * jax.nn.sigmoid(x)


def _softplus(x):
    return jnp.maximum(x, 0.0) + jnp.log1p(jnp.exp(-jnp.abs(x)))


def _dot(a, b):
    return jnp.dot(a.astype(BF16), b.astype(BF16), preferred_element_type=F32)


def _dot_nt(a, b):
    return lax.dot_general(a.astype(BF16), b.astype(BF16), (((1,), (1,)), ((), ())),
                           preferred_element_type=F32)


def _dot_tn(a, b):
    return lax.dot_general(a.astype(BF16), b.astype(BF16), (((0,), (0,)), ((), ())),
                           preferred_element_type=F32)


def _hdot(a, b):
    return jnp.dot(a, b, preferred_element_type=F32, precision=lax.Precision.HIGHEST)


def _split_bf16(a):
    hi = a.astype(BF16)
    return hi, (a - hi.astype(F32)).astype(BF16)


def _dot3(a, b):
    return (jnp.dot(a[0], b[0], preferred_element_type=F32)
            + jnp.dot(a[0], b[1], preferred_element_type=F32)
            + jnp.dot(a[1], b[0], preferred_element_type=F32))


def _norm_matmul_kernel(x_ref, g_ref, w_ref, o_ref, xn_ref):
    @pl.when(pl.program_id(1) == 0)
    def _():
        xn_ref[...] = _rmsnorm(x_ref[...], g_ref[...]).astype(BF16)

    o_ref[...] = jnp.dot(xn_ref[...], w_ref[...], preferred_element_type=F32).astype(o_ref.dtype)


def _norm_matmul(x, g, w, out_dtype, tm, tn):
    n, k = x.shape
    m = w.shape[1]
    return pl.pallas_call(
        _norm_matmul_kernel,
        grid=(n // tm, m // tn),
        in_specs=[
            pl.BlockSpec((tm, k), lambda i, j: (i, 0)),
            pl.BlockSpec((1, k), lambda i, j: (0, 0)),
            pl.BlockSpec((k, tn), lambda i, j: (0, j)),
        ],
        out_specs=pl.BlockSpec((tm, tn), lambda i, j: (i, j)),
        out_shape=jax.ShapeDtypeStruct((n, m), out_dtype),
        scratch_shapes=[pltpu.VMEM((tm, k), BF16)],
        compiler_params=pltpu.CompilerParams(
            dimension_semantics=("arbitrary", "arbitrary"), vmem_limit_bytes=VMEM_LIMIT),
        name="norm_matmul",
    )(x, g.reshape(1, k), w)


def _proj_post_kernel(a_ref, w_ref, g_ref, x_ref, o_ref):
    y = jnp.dot(a_ref[...], w_ref[...], preferred_element_type=F32)
    o_ref[...] = x_ref[...] + _rmsnorm(y, g_ref[...])


def _proj_post(a, w, g, x, tm):
    n, k = a.shape
    d = w.shape[1]
    return pl.pallas_call(
        _proj_post_kernel,
        grid=(n // tm,),
        in_specs=[
            pl.BlockSpec((tm, k), lambda i: (i, 0)),
            pl.BlockSpec((k, d), lambda i: (0, 0)),
            pl.BlockSpec((1, d), lambda i: (0, 0)),
            pl.BlockSpec((tm, d), lambda i: (i, 0)),
        ],
        out_specs=pl.BlockSpec((tm, d), lambda i: (i, 0)),
        out_shape=jax.ShapeDtypeStruct((n, d), F32),
        compiler_params=pltpu.CompilerParams(
            dimension_semantics=("arbitrary",), vmem_limit_bytes=VMEM_LIMIT),
        name="proj_post",
    )(a, w, g.reshape(1, d), x)


def _shift_rows(x, k):
    return pltpu.roll(x, k, axis=0) if k else x


def _causal_conv(x_ext, w_ref):
    width = w_ref.shape[0]
    acc = x_ext * w_ref[width - 1:width, :]
    for k in range(width - 1):
        acc = acc + _shift_rows(x_ext, width - 1 - k) * w_ref[k:k + 1, :]
    return acc


def _expand_heads(cols, first, n_heads, width):
    rows = cols.shape[0]
    per_vreg = 128 // width
    lane = lax.broadcasted_iota(jnp.int32, (rows, 128), 1)
    pieces = []
    for m in range(n_heads // per_vreg):
        piece = jnp.broadcast_to(cols[:, first + m * per_vreg:first + m * per_vreg + 1], (rows, 128))
        for j in range(1, per_vreg):
            c = first + m * per_vreg + j
            piece = jnp.where(lane >= j * width, jnp.broadcast_to(cols[:, c:c + 1], (rows, 128)), piece)
        pieces.append(piece)
    return jnp.concatenate(pieces, axis=1)


def _unit_lower_inverse(ms, row, col):
    same16 = (row >> 4) == (col >> 4)
    same32 = (row >> 5) == (col >> 5)
    off16 = same32 & jnp.logical_not(same16)
    eye = (row == col).astype(F32)
    split = lambda xs: [_split_bf16(x) for x in xs]
    d = [jnp.where(same16, m, 0.0) for m in ms]
    ds = split(d)
    d2 = split([_dot3(x, x) for x in ds])
    p = [eye - x for x in d]
    p = [x + _dot3(xs, y) for x, xs, y in zip(p, split(p), d2)]
    d4 = split([_dot3(x, x) for x in d2])
    p = [x + _dot3(xs, y) for x, xs, y in zip(p, split(p), d4)]
    d8 = split([_dot3(x, x) for x in d4])
    p = [x + _dot3(xs, y) for x, xs, y in zip(p, split(p), d8)]
    ps = split(p)
    t = split([_dot3(xs, _split_bf16(jnp.where(off16, m, 0.0))) for xs, m in zip(ps, ms)])
    q = [x - _dot3(y, xs) for x, xs, y in zip(p, ps, t)]
    qs = split(q)
    t = split([_dot3(xs, _split_bf16(jnp.where(same32, 0.0, m))) for xs, m in zip(qs, ms)])
    return [x - _dot3(y, xs) for x, xs, y in zip(q, qs, t)]


def _mixer_kernel(ucur_ref, uprev_ref, conv_a_ref, pool_w_ref, pool_scale_ref, gdn_conv_ref,
                  gdn_norm_ref, ssm_conv_ref, ssm_bias_ref, ssm_d_ref, ssm_norm_ref,
                  alog_ref, dtbias_ref, out_ref, s_ref, h_ref, *, tile):
    i = pl.program_id(1)
    n_chunks = tile // CHUNK
    hg = SSM_HEADS // SSM_GROUPS
    gw = hg * SSM_HEAD_DIM

    @pl.when(i == 0)
    def _():
        s_ref[...] = jnp.zeros_like(s_ref)
        h_ref[...] = jnp.zeros_like(h_ref)

    keep_prev = (i > 0).astype(F32)

    def ext(off, width):
        prev = uprev_ref[:, off:off + width] * keep_prev
        return jnp.concatenate([prev, ucur_ref[:, off:off + width]], axis=0)

    p = ext(OFF_A + GROUP_WIDTH, GROUP_WIDTH) * ext(OFF_A + 2 * GROUP_WIDTH, GROUP_WIDTH)
    ya = ucur_ref[:, OFF_A:OFF_A + GROUP_WIDTH] * _causal_conv(p, conv_a_ref)[HALO:]
    out_ref[:, 0:GROUP_WIDTH] = ya.astype(out_ref.dtype)

    pos = (i * tile + 1 + lax.broadcasted_iota(jnp.int32, (tile, 1), 0)).astype(F32)
    for gi, window in enumerate(POOL_WINDOWS):
        off = OFF_B + gi * POOL_GROUP
        s = ext(off, POOL_GROUP)
        k = 1
        while k < window:
            s = s + _shift_rows(s, k)
            k *= 2
        pooled = s[HALO:] / jnp.minimum(pos, float(window)) - ucur_ref[:, off:off + POOL_GROUP]
        yb = _dot(pooled, pool_w_ref[gi]) * pool_scale_ref[:, gi * POOL_GROUP:(gi + 1) * POOL_GROUP]
        out_ref[:, GROUP_WIDTH + gi * POOL_GROUP:GROUP_WIDTH + (gi + 1) * POOL_GROUP] = yb.astype(out_ref.dtype)

    small = ucur_ref[:, OFF_SMALL:OFF_SMALL + 128]
    sp = _softplus(small + dtbias_ref[...])
    rate = -jnp.exp(alog_ref[...]) * sp
    beta_all = jax.nn.sigmoid(small)

    qkv = _silu(_causal_conv(ext(OFF_QKV, 3 * GROUP_WIDTH), gdn_conv_ref)[HALO:])
    xbc = _silu(_causal_conv(ext(OFF_XBC, SSM_XBC), ssm_conv_ref)[HALO:] + ssm_bias_ref[...])

    row = lax.broadcasted_iota(jnp.int32, (CHUNK, CHUNK), 0)
    col = lax.broadcasted_iota(jnp.int32, (CHUNK, CHUNK), 1)
    incl = row >= col
    strict = row > col
    ltri = incl.astype(F32)
    lane_head = lax.broadcasted_iota(jnp.int32, (CHUNK, gw), 1) // SSM_HEAD_DIM

    def masked_decay(cum, cum_t, ch):
        diff = cum[:, ch:ch + 1] - cum_t[ch:ch + 1, :]
        return jnp.where(incl, jnp.exp(jnp.where(incl, diff, 0.0)), 0.0)

    chunks = range(n_chunks)
    rows = [slice(c * CHUNK, (c + 1) * CHUNK) for c in chunks]
    acs = [_hdot(ltri, rate[r]) for r in rows]
    acs_t = [a.T for a in acs]
    e_acs = [jnp.exp(a) for a in acs]
    e_end = [jnp.exp(a[CHUNK - 1:CHUNK, :] - a) for a in acs]
    e_blk = [jnp.exp(a[CHUNK - 1:CHUNK, :]) for a in acs]

    items = [(c, h) for c in chunks for h in range(GDN_HEADS)]

    def head_cols(base, h):
        return slice(base + h * GDN_HEAD_DIM, base + (h + 1) * GDN_HEAD_DIM)

    q_all = [qkv[rows[c], head_cols(0, h)] for c, h in items]
    k_all = [qkv[rows[c], head_cols(GROUP_WIDTH, h)] for c, h in items]
    v_all = [qkv[rows[c], head_cols(2 * GROUP_WIDTH, h)] for c, h in items]
    q_all = [x * lax.rsqrt(jnp.sum(x * x, axis=-1, keepdims=True) + EPS) * GDN_HEAD_DIM ** -0.5 for x in q_all]
    k_all = [x * lax.rsqrt(jnp.sum(x * x, axis=-1, keepdims=True) + EPS) for x in k_all]
    beta = [beta_all[rows[c], SMALL_B + h:SMALL_B + h + 1] for c, h in items]
    decay = [masked_decay(acs[c], acs_t[c], SMALL_A + h) for c, h in items]
    e_g = [e_acs[c][:, SMALL_A + h:SMALL_A + h + 1] for c, h in items]
    kb = [x * b for x, b in zip(k_all, beta)]
    m_low = [jnp.where(strict, _dot_nt(x, y) * dc, 0.0) for x, y, dc in zip(kb, k_all, decay)]
    qk = [_dot_nt(x, y) * dc for x, y, dc in zip(q_all, k_all, decay)]
    t_inv = _unit_lower_inverse(m_low, row, col)
    sol = [_dot3(_split_bf16(t), _split_bf16(jnp.concatenate([x * b, y * e], axis=1)))
           for t, x, b, y, e in zip(t_inv, v_all, beta, kb, e_g)]
    q_dec = [x * e for x, e in zip(q_all, e_g)]
    k_end = [x * e_end[c][:, SMALL_A + h:SMALL_A + h + 1] for x, (c, h) in zip(k_all, items)]

    x_in, y_diag, s_inc, c_mat, e_acs_x, e_blk_x = [], [], [], [], [], []
    for c in chunks:
        r = rows[c]
        x_in.append(xbc[r, :GROUP_WIDTH])
        x_dt = x_in[c] * _expand_heads(sp[r], SMALL_DT, SSM_HEADS, SSM_HEAD_DIM)
        x_end = x_dt * _expand_heads(e_end[c], SMALL_DT, SSM_HEADS, SSM_HEAD_DIM)
        e_acs_x.append(_expand_heads(e_acs[c], SMALL_DT, SSM_HEADS, SSM_HEAD_DIM))
        e_blk_x.append(_expand_heads(e_blk[c], SMALL_DT, SSM_HEADS, SSM_HEAD_DIM))
        for g in range(SSM_GROUPS):
            b_g = xbc[r, GROUP_WIDTH + g * SSM_STATE:GROUP_WIDTH + (g + 1) * SSM_STATE]
            c_g = xbc[r, GROUP_WIDTH + (SSM_GROUPS + g) * SSM_STATE:GROUP_WIDTH + (SSM_GROUPS + g + 1) * SSM_STATE]
            cb = _dot_nt(c_g, b_g)
            x_g = x_dt[:, g * gw:(g + 1) * gw]
            y_g = None
            for j in range(hg):
                l_m = masked_decay(acs[c], acs_t[c], SMALL_DT + g * hg + j)
                part = _dot(cb * l_m, jnp.where(lane_head == j, x_g, 0.0))
                y_g = part if y_g is None else y_g + part
            y_diag.append(y_g)
            s_inc.append(_dot_tn(b_g, x_end[:, g * gw:(g + 1) * gw]))
            c_mat.append(c_g)

    s_state = [s_ref[h] for h in range(GDN_HEADS)]
    h_state = [h_ref[g] for g in range(SSM_GROUPS)]
    for c in chunks:
        r = rows[c]
        idx = [c * GDN_HEADS + h for h in range(GDN_HEADS)]
        v_new = [sol[n][:, :GDN_HEAD_DIM] - _dot(sol[n][:, GDN_HEAD_DIM:], s_state[h])
                 for h, n in enumerate(idx)]
        o = [_dot(q_dec[n], s_state[h]) + _dot(qk[n], v_new[h]) for h, n in enumerate(idx)]
        s_state = [s_state[h] * e_blk[c][:, SMALL_A + h:SMALL_A + h + 1] + _dot_tn(k_end[n], v_new[h])
                   for h, n in enumerate(idx)]
        for h in range(GDN_HEADS):
            z = ucur_ref[r, head_cols(OFF_CZ, h)]
            y = _rmsnorm(o[h], gdn_norm_ref[...]) * _silu(z)
            out_ref[r, head_cols(2 * GROUP_WIDTH, h)] = y.astype(out_ref.dtype)

        ys = []
        for g in range(SSM_GROUPS):
            n = c * SSM_GROUPS + g
            ys.append(y_diag[n] + _dot(c_mat[n], h_state[g]) * e_acs_x[c][:, g * gw:(g + 1) * gw])
            h_state[g] = h_state[g] * e_blk_x[c][:, g * gw:(g + 1) * gw] + s_inc[n]
        y = jnp.concatenate(ys, axis=1) + ssm_d_ref[...] * x_in[c]
        y = y * _silu(ucur_ref[r, OFF_DZ:OFF_DZ + GROUP_WIDTH])
        for g in range(SSM_GROUPS):
            yn = _rmsnorm(y[:, g * gw:(g + 1) * gw], ssm_norm_ref[:, g * gw:(g + 1) * gw])
            out_ref[r, 3 * GROUP_WIDTH + g * gw:3 * GROUP_WIDTH + (g + 1) * gw] = yn.astype(out_ref.dtype)
    for h in range(GDN_HEADS):
        s_ref[h] = s_state[h]
    for g in range(SSM_GROUPS):
        h_ref[g] = h_state[g]


def _mixers(u, batch, seq, conv_a_w, pool_w, pool_scale, gdn_conv_w, gdn_norm_g, ssm_conv_w, ssm_conv_b,
            ssm_d_x, ssm_norm_g, alog_row, dtbias_row, tile):
    n_tiles = seq // tile

    def const(shape):
        return pl.BlockSpec(shape, lambda b, i: (0,) * len(shape))

    return pl.pallas_call(
        functools.partial(_mixer_kernel, tile=tile),
        grid=(batch, n_tiles),
        in_specs=[
            pl.BlockSpec((tile, IN_COLS_PADDED), lambda b, i: (b * n_tiles + i, 0)),
            pl.BlockSpec((HALO, IN_COLS_PADDED),
                         lambda b, i: (jnp.maximum((b * n_tiles + i) * (tile // HALO) - 1, 0), 0)),
            const(conv_a_w.shape), const(pool_w.shape), const(pool_scale.shape), const(gdn_conv_w.shape),
            const(gdn_norm_g.shape), const(ssm_conv_w.shape), const(ssm_conv_b.shape), const(ssm_d_x.shape),
            const(ssm_norm_g.shape), const(alog_row.shape), const(dtbias_row.shape),
        ],
        out_specs=pl.BlockSpec((tile, D_MODEL), lambda b, i: (b * n_tiles + i, 0)),
        out_shape=jax.ShapeDtypeStruct((batch * seq, D_MODEL), BF16),
        scratch_shapes=[
            pltpu.VMEM((GDN_HEADS, GDN_HEAD_DIM, GDN_HEAD_DIM), F32),
            pltpu.VMEM((SSM_GROUPS, SSM_STATE, (SSM_HEADS // SSM_GROUPS) * SSM_HEAD_DIM), F32),
        ],
        compiler_params=pltpu.CompilerParams(
            dimension_semantics=("arbitrary", "arbitrary"), vmem_limit_bytes=VMEM_LIMIT),
        name="mixers",
    )(u, u, conv_a_w, pool_w, pool_scale, gdn_conv_w, gdn_norm_g, ssm_conv_w, ssm_conv_b, ssm_d_x,
      ssm_norm_g, alog_row, dtbias_row)


def _xattn_kernel(x_ref, g_ref, wq_ref, kv_ref, o_ref):
    xn = _rmsnorm(x_ref[...], g_ref[...]).astype(BF16)
    q = jnp.dot(xn, wq_ref[...], preferred_element_type=F32)
    for h in range(XA_HEADS):
        lo = h * XA_HEAD_DIM
        k_h = kv_ref[:, lo:lo + XA_HEAD_DIM]
        v_h = kv_ref[:, D_MODEL + lo:D_MODEL + lo + XA_HEAD_DIM]
        s = _dot_nt(q[:, lo:lo + XA_HEAD_DIM], k_h) * XA_HEAD_DIM ** -0.5
        e = jnp.exp(s - jnp.max(s, axis=-1, keepdims=True))
        p = e / jnp.sum(e, axis=-1, keepdims=True)
        o_ref[:, lo:lo + XA_HEAD_DIM] = _dot(p, v_h).astype(o_ref.dtype)


def _xattn(x, g, wq, kv, batch, seq, tq):
    n, d = x.shape
    n_tiles = seq // tq
    return pl.pallas_call(
        _xattn_kernel,
        grid=(batch, n_tiles),
        in_specs=[
            pl.BlockSpec((tq, d), lambda b, i: (b * n_tiles + i, 0)),
            pl.BlockSpec((1, d), lambda b, i: (0, 0)),
            pl.BlockSpec((d, d), lambda b, i: (0, 0)),
            pl.BlockSpec((MEM_LEN, 2 * d), lambda b, i: (b, 0)),
        ],
        out_specs=pl.BlockSpec((tq, d), lambda b, i: (b * n_tiles + i, 0)),
        out_shape=jax.ShapeDtypeStruct((n, d), BF16),
        compiler_params=pltpu.CompilerParams(
            dimension_semantics=("arbitrary", "arbitrary"), vmem_limit_bytes=VMEM_LIMIT),
        name="xattn",
    )(x, g.reshape(1, d), wq, kv)


def _ffn_kernel(x_ref, gpre_ref, wg_ref, wu_ref, wd_ref, gpost_ref, o_ref, xn_ref, acc_ref):
    j = pl.program_id(1)

    @pl.when(j == 0)
    def _():
        xn_ref[...] = _rmsnorm(x_ref[...], gpre_ref[...]).astype(BF16)
        acc_ref[...] = jnp.zeros_like(acc_ref)

    xn = xn_ref[...]
    gate = jnp.dot(xn, wg_ref[...], preferred_element_type=F32)
    up = jnp.dot(xn, wu_ref[...], preferred_element_type=F32)
    acc_ref[...] += jnp.dot((_silu(gate) * up).astype(BF16), wd_ref[...], preferred_element_type=F32)

    @pl.when(j == pl.num_programs(1) - 1)
    def _():
        o_ref[...] = x_ref[...] + _rmsnorm(acc_ref[...], gpost_ref[...])


def _ffn(x, gpre, w_gu, w_down, gpost, tm, tf):
    n, d = x.shape
    n_ff = D_FF // tf
    return pl.pallas_call(
        _ffn_kernel,
        grid=(n // tm, n_ff),
        in_specs=[
            pl.BlockSpec((tm, d), lambda i, j: (i, 0)),
            pl.BlockSpec((1, d), lambda i, j: (0, 0)),
            pl.BlockSpec((d, tf), lambda i, j: (0, j)),
            pl.BlockSpec((d, tf), lambda i, j: (0, j + n_ff)),
            pl.BlockSpec((tf, d), lambda i, j: (j, 0)),
            pl.BlockSpec((1, d), lambda i, j: (0, 0)),
        ],
        out_specs=pl.BlockSpec((tm, d), lambda i, j: (i, 0)),
        out_shape=jax.ShapeDtypeStruct((n, d), F32),
        scratch_shapes=[pltpu.VMEM((tm, d), BF16), pltpu.VMEM((tm, d), F32)],
        compiler_params=pltpu.CompilerParams(
            dimension_semantics=("arbitrary", "arbitrary"), vmem_limit_bytes=VMEM_LIMIT),
        name="ffn",
    )(x, gpre.reshape(1, d), w_gu, w_gu, w_down, gpost.reshape(1, d))


def _rearrange_w_in(w):
    c_small = 4096
    d_start = 4104
    dt_start = 5640
    pad = IN_COLS_PADDED - OFF_SMALL - 16
    return jnp.concatenate([
        w[:, :c_small], w[:, d_start:dt_start], w[:, c_small:d_start], w[:, dt_start:],
        jnp.zeros((w.shape[0], pad), w.dtype)], axis=1).astype(BF16)


def _small_row(gdn_vals, ssm_vals):
    row = jnp.zeros((1, 128), F32)
    row = row.at[0, SMALL_A:SMALL_A + GDN_HEADS].set(gdn_vals.astype(F32))
    return row.at[0, SMALL_DT:SMALL_DT + SSM_HEADS].set(ssm_vals.astype(F32))


def kernel(x, mem, norm_g, w_in, conv_a_w, pool_w, pool_scale, gdn_conv_w, gdn_A_log, gdn_dt_bias, gdn_norm_g,
           ssm_conv_w, ssm_conv_b, ssm_A_log, ssm_dt_bias, ssm_D, ssm_norm_g, w_out, xa_wq, xa_wkv, xa_wo,
           ffn_w_gu, ffn_w_down):
    batch, seq, d = x.shape
    depth = w_in.shape[0]
    xf = x.reshape(batch * seq, d)
    memf = mem.reshape(batch * MEM_LEN, d)
    for l in range(depth):
        g = norm_g[l]
        u = _norm_matmul(xf, g[0], _rearrange_w_in(w_in[l]), F32, tm=512, tn=1152)
        mixed = _mixers(
            u, batch, seq, conv_a_w[l], pool_w[l].astype(BF16), pool_scale[l].reshape(1, -1), gdn_conv_w[l],
            gdn_norm_g[l].reshape(1, -1), ssm_conv_w[l], ssm_conv_b[l].reshape(1, -1),
            jnp.repeat(ssm_D[l], SSM_HEAD_DIM).reshape(1, -1), ssm_norm_g[l].reshape(1, -1),
            _small_row(gdn_A_log[l], ssm_A_log[l]), _small_row(gdn_dt_bias[l], ssm_dt_bias[l]), tile=2 * CHUNK)
        xf = _proj_post(mixed, w_out[l].astype(BF16), g[1], xf, tm=512)
        kv = _norm_matmul(memf, g[4], xa_wkv[l].astype(BF16), BF16, tm=512, tn=1024)
        att = _xattn(xf, g[2], xa_wq[l].astype(BF16), kv, batch, seq, tq=512)
        xf = _proj_post(att, xa_wo[l].astype(BF16), g[3], xf, tm=512)
        xf = _ffn(xf, g[5], ffn_w_gu[l].astype(BF16), ffn_w_down[l].astype(BF16), g[6], tm=512, tf=512)
    return xf.reshape(batch, seq, d)
```

```python
import functools

import jax
import jax.numpy as jnp
from jax import lax
from jax.experimental import pallas as pl
from jax.experimental.pallas import tpu as pltpu

F32 = jnp.float32
BF16 = jnp.bfloat16
EPS = 1e-6

D_MODEL = 2048
GROUP_WIDTH = 512
CHUNK = 64
POOL_WINDOWS = (2, 4, 8, 16)
POOL_GROUP = 128
GDN_HEADS = 4
GDN_HEAD_DIM = 128
SSM_HEADS = 8
SSM_HEAD_DIM = 64
SSM_GROUPS = 2
SSM_STATE = 128
SSM_XBC = 1024
MEM_LEN = 256
XA_HEADS = 4
XA_HEAD_DIM = 512
D_FF = 5632

IN_COLS = 5648
HEAD_COLS = 4096
TAIL_START = 4096
TAIL_VALID = IN_COLS - TAIL_START
TAIL_COLS = 1664
TAIL_BLOCK = 2048
TAIL_SKEW = 8
OFF_A = 0
OFF_B = 1536
OFF_QKV = 2048
OFF_CZ = 3584
OFF_DZ = 4096
OFF_XBC = 4608
OFF_SMALL = 5632
SMALL_DT = 0
SMALL_A = 120
SMALL_B = 124
HALO = 16

VMEM_LIMIT = 48 * 1024 * 1024
VMEM_LIMIT_LARGE = 56 * 1024 * 1024


def _rmsnorm(x, g):
    return x * lax.rsqrt(jnp.mean(x * x, axis=-1, keepdims=True) + EPS) * g


def _silu(x):
    return x * jax.nn.sigmoid(x)


def _softplus(x):
    return jnp.maximum(x, 0.0) + jnp.log1p(jnp.exp(-jnp.abs(x)))


def _dot(a, b):
    return jnp.dot(a.astype(BF16), b.astype(BF16), preferred_element_type=F32)


def _dot_nt(a, b):
    return lax.dot_general(a.astype(BF16), b.astype(BF16), (((1,), (1,)), ((), ())),
                           preferred_element_type=F32)


def _dot_tn(a, b):
    return lax.dot_general(a.astype(BF16), b.astype(BF16), (((0,), (0,)), ((), ())),
                           preferred_element_type=F32)


def _hdot(a, b):
    return jnp.dot(a, b, preferred_element_type=F32, precision=lax.Precision.HIGHEST)


def _split_bf16(a):
    hi = a.astype(BF16)
    return hi, (a - hi.astype(F32)).astype(BF16)


def _dot3(a, b):
    return (jnp.dot(a[0], b[0], preferred_element_type=F32)
            + jnp.dot(a[0], b[1], preferred_element_type=F32)
            + jnp.dot(a[1], b[0], preferred_element_type=F32))


def _norm_matmul_kernel(x_ref, g_ref, w_ref, o_ref, xn_ref):
    @pl.when(pl.program_id(1) == 0)
    def _():
        xn_ref[...] = _rmsnorm(x_ref[...], g_ref[...]).astype(BF16)

    o_ref[...] = jnp.dot(xn_ref[...], w_ref[...].astype(BF16), preferred_element_type=F32).astype(o_ref.dtype)


def _norm_matmul(x, g, w, layer, n_cols, out_dtype, tm, tn):
    n, k = x.shape
    return pl.pallas_call(
        _norm_matmul_kernel,
        grid=(n // tm, n_cols // tn),
        in_specs=[
            pl.BlockSpec((tm, k), lambda i, j: (i, 0)),
            pl.BlockSpec((1, k), lambda i, j: (0, 0)),
            pl.BlockSpec((None, k, tn), lambda i, j: (layer, 0, j)),
        ],
        out_specs=pl.BlockSpec((tm, tn), lambda i, j: (i, j)),
        out_shape=jax.ShapeDtypeStruct((n, n_cols), out_dtype),
        scratch_shapes=[pltpu.VMEM((tm, k), BF16)],
        compiler_params=pltpu.CompilerParams(
            dimension_semantics=("arbitrary", "arbitrary"), vmem_limit_bytes=VMEM_LIMIT),
        name="norm_matmul",
    )(x, g.reshape(1, k), w)


def _norm_matmul_tail_kernel(x_ref, g_ref, w_ref, o_ref, wb_ref):
    @pl.when(pl.program_id(0) == 0)
    def _():
        wb_ref[...] = w_ref[:, :TAIL_COLS].astype(BF16)

    xn = _rmsnorm(x_ref[...], g_ref[...]).astype(BF16)
    y = jnp.dot(xn, wb_ref[...], preferred_element_type=F32)
    y = pltpu.roll(y, TAIL_COLS - TAIL_SKEW, axis=1)
    lane = lax.broadcasted_iota(jnp.int32, y.shape, 1)
    stale = (lane >= TAIL_VALID - TAIL_SKEW) & (lane < TAIL_COLS - TAIL_SKEW)
    o_ref[...] = jnp.where(stale, 0.0, y)


def _norm_matmul_tail(x, g, w, layer, tm):
    n, k = x.shape
    return pl.pallas_call(
        _norm_matmul_tail_kernel,
        grid=(n // tm,),
        in_specs=[
            pl.BlockSpec((tm, k), lambda i: (i, 0)),
            pl.BlockSpec((1, k), lambda i: (0, 0)),
            pl.BlockSpec((None, k, TAIL_BLOCK), lambda i: (layer, 0, TAIL_START // TAIL_BLOCK),
                         pipeline_mode=pl.Buffered(1)),
        ],
        out_specs=pl.BlockSpec((tm, TAIL_COLS), lambda i: (i, 0)),
        out_shape=jax.ShapeDtypeStruct((n, TAIL_COLS), F32),
        scratch_shapes=[pltpu.VMEM((k, TAIL_COLS), BF16)],
        compiler_params=pltpu.CompilerParams(
            dimension_semantics=("arbitrary",), vmem_limit_bytes=VMEM_LIMIT),
        name="norm_matmul_tail",
    )(x, g.reshape(1, k), w)


def _proj_post_kernel(a_ref, w_ref, g_ref, x_ref, o_ref, wb_ref):
    @pl.when(pl.program_id(0) == 0)
    def _():
        wb_ref[...] = w_ref[...].astype(BF16)

    y = jnp.dot(a_ref[...], wb_ref[...], preferred_element_type=F32)
    o_ref[...] = x_ref[...] + _rmsnorm(y, g_ref[...])


def _proj_post(a, w, layer, g, x, tm):
    n, k = a.shape
    d = w.shape[2]
    return pl.pallas_call(
        _proj_post_kernel,
        grid=(n // tm,),
        in_specs=[
            pl.BlockSpec((tm, k), lambda i: (i, 0)),
            pl.BlockSpec((None, k, d), lambda i: (layer, 0, 0), pipeline_mode=pl.Buffered(1)),
            pl.BlockSpec((1, d), lambda i: (0, 0)),
            pl.BlockSpec((tm, d), lambda i: (i, 0)),
        ],
        out_specs=pl.BlockSpec((tm, d), lambda i: (i, 0)),
        out_shape=jax.ShapeDtypeStruct((n, d), F32),
        scratch_shapes=[pltpu.VMEM((k, d), BF16)],
        compiler_params=pltpu.CompilerParams(
            dimension_semantics=("arbitrary",), vmem_limit_bytes=VMEM_LIMIT_LARGE),
        name="proj_post",
    )(a, w, g.reshape(1, d), x)


def _shift_rows(x, k):
    return pltpu.roll(x, k, axis=0) if k else x


def _causal_conv(x_ext, w_ref):
    width = w_ref.shape[0]
    acc = x_ext * w_ref[width - 1:width, :]
    for k in range(width - 1):
        acc = acc + _shift_rows(x_ext, width - 1 - k) * w_ref[k:k + 1, :]
    return acc


def _expand_heads(cols, first, n_heads, width):
    rows = cols.shape[0]
    per_vreg = 128 // width
    lane = lax.broadcasted_iota(jnp.int32, (rows, 128), 1)
    pieces = []
    for m in range(n_heads // per_vreg):
        piece = jnp.broadcast_to(cols[:, first + m * per_vreg:first + m * per_vreg + 1], (rows, 128))
        for j in range(1, per_vreg):
            c = first + m * per_vreg + j
            piece = jnp.where(lane >= j * width, jnp.broadcast_to(cols[:, c:c + 1], (rows, 128)), piece)
        pieces.append(piece)
    return jnp.concatenate(pieces, axis=1)


def _unit_lower_inverse(ms, row, col):
    same16 = (row >> 4) == (col >> 4)
    same32 = (row >> 5) == (col >> 5)
    off16 = same32 & jnp.logical_not(same16)
    eye = (row == col).astype(F32)
    split = lambda xs: [_split_bf16(x) for x in xs]
    d = [jnp.where(same16, m, 0.0) for m in ms]
    ds = split(d)
    d2 = split([_dot3(x, x) for x in ds])
    p = [eye - x for x in d]
    p = [x + _dot3(xs, y) for x, xs, y in zip(p, split(p), d2)]
    d4 = split([_dot3(x, x) for x in d2])
    p = [x + _dot3(xs, y) for x, xs, y in zip(p, split(p), d4)]
    d8 = split([_dot3(x, x) for x in d4])
    p = [x + _dot3(xs, y) for x, xs, y in zip(p, split(p), d8)]
    ps = split(p)
    t = split([_dot3(xs, _split_bf16(jnp.where(off16, m, 0.0))) for xs, m in zip(ps, ms)])
    q = [x - _dot3(y, xs) for x, xs, y in zip(p, ps, t)]
    qs = split(q)
    t = split([_dot3(xs, _split_bf16(jnp.where(same32, 0.0, m))) for xs, m in zip(qs, ms)])
    return [x - _dot3(y, xs) for x, xs, y in zip(q, qs, t)]


def _mixer_kernel(head_ref, head_prev_ref, tail_ref, tail_prev_ref, conv_a_ref, pool_w_ref, pool_scale_ref, gdn_conv_ref,
                  gdn_norm_ref, ssm_conv_ref, ssm_bias_ref, ssm_d_ref, ssm_norm_ref,
                  alog_ref, dtbias_ref, out_ref, s_ref, h_ref, *, tile):
    i = pl.program_id(1)
    n_chunks = tile // CHUNK
    hg = SSM_HEADS // SSM_GROUPS
    gw = hg * SSM_HEAD_DIM

    @pl.when(i == 0)
    def _():
        s_ref[...] = jnp.zeros_like(s_ref)
        h_ref[...] = jnp.zeros_like(h_ref)

    keep_prev = (i > 0).astype(F32)

    def cur(off, width, rows=slice(None)):
        if off < OFF_DZ:
            return head_ref[rows, off:off + width]
        return tail_ref[rows, off - OFF_DZ:off - OFF_DZ + width]

    def ext(off, width):
        if off < OFF_DZ:
            prev = head_prev_ref[:, off:off + width]
        else:
            prev = tail_prev_ref[:, off - OFF_DZ:off - OFF_DZ + width]
        return jnp.concatenate([prev * keep_prev, cur(off, width)], axis=0)

    p = ext(OFF_A + GROUP_WIDTH, GROUP_WIDTH) * ext(OFF_A + 2 * GROUP_WIDTH, GROUP_WIDTH)
    ya = cur(OFF_A, GROUP_WIDTH) * _causal_conv(p, conv_a_ref)[HALO:]
    out_ref[:, 0:GROUP_WIDTH] = ya.astype(out_ref.dtype)

    pos = (i * tile + 1 + lax.broadcasted_iota(jnp.int32, (tile, 1), 0)).astype(F32)
    for gi, window in enumerate(POOL_WINDOWS):
        off = OFF_B + gi * POOL_GROUP
        s = ext(off, POOL_GROUP)
        k = 1
        while k < window:
            s = s + _shift_rows(s, k)
            k *= 2
        pooled = s[HALO:] / jnp.minimum(pos, float(window)) - cur(off, POOL_GROUP)
        yb = _dot(pooled, pool_w_ref[gi]) * pool_scale_ref[:, gi * POOL_GROUP:(gi + 1) * POOL_GROUP]
        out_ref[:, GROUP_WIDTH + gi * POOL_GROUP:GROUP_WIDTH + (gi + 1) * POOL_GROUP] = yb.astype(out_ref.dtype)

    small = cur(OFF_SMALL, 128)
    sp = _softplus(small + dtbias_ref[...])
    rate = -jnp.exp(alog_ref[...]) * sp
    beta_all = jax.nn.sigmoid(small)

    qkv = _silu(_causal_conv(ext(OFF_QKV, 3 * GROUP_WIDTH), gdn_conv_ref)[HALO:])
    xbc = _silu(_causal_conv(ext(OFF_XBC, SSM_XBC), ssm_conv_ref)[HALO:] + ssm_bias_ref[...])

    row = lax.broadcasted_iota(jnp.int32, (CHUNK, CHUNK), 0)
    col = lax.broadcasted_iota(jnp.int32, (CHUNK, CHUNK), 1)
    incl = row >= col
    strict = row > col
    ltri = incl.astype(F32)
    lane_head = lax.broadcasted_iota(jnp.int32, (CHUNK, gw), 1) // SSM_HEAD_DIM

    def masked_decay(cum, cum_t, ch):
        diff = cum[:, ch:ch + 1] - cum_t[ch:ch + 1, :]
        return jnp.where(incl, jnp.exp(jnp.where(incl, diff, 0.0)), 0.0)

    chunks = range(n_chunks)
    rows = [slice(c * CHUNK, (c + 1) * CHUNK) for c in chunks]
    acs = [_hdot(ltri, rate[r]) for r in rows]
    acs_t = [a.T for a in acs]
    e_acs = [jnp.exp(a) for a in acs]
    e_end = [jnp.exp(a[CHUNK - 1:CHUNK, :] - a) for a in acs]
    e_blk = [jnp.exp(a[CHUNK - 1:CHUNK, :]) for a in acs]

    items = [(c, h) for c in chunks for h in range(GDN_HEADS)]

    def head_cols(base, h):
        return slice(base + h * GDN_HEAD_DIM, base + (h + 1) * GDN_HEAD_DIM)

    q_all = [qkv[rows[c], head_cols(0, h)] for c, h in items]
    k_all = [qkv[rows[c], head_cols(GROUP_WIDTH, h)] for c, h in items]
    v_all = [qkv[rows[c], head_cols(2 * GROUP_WIDTH, h)] for c, h in items]
    q_all = [x * lax.rsqrt(jnp.sum(x * x, axis=-1, keepdims=True) + EPS) * GDN_HEAD_DIM ** -0.5 for x in q_all]
    k_all = [x * lax.rsqrt(jnp.sum(x * x, axis=-1, keepdims=True) + EPS) for x in k_all]
    beta = [beta_all[rows[c], SMALL_B + h:SMALL_B + h + 1] for c, h in items]
    decay = [masked_decay(acs[c], acs_t[c], SMALL_A + h) for c, h in items]
    e_g = [e_acs[c][:, SMALL_A + h:SMALL_A + h + 1] for c, h in items]
    kb = [x * b for x, b in zip(k_all, beta)]
    m_low = [jnp.where(strict, _dot_nt(x, y) * dc, 0.0) for x, y, dc in zip(kb, k_all, decay)]
    qk = [_dot_nt(x, y) * dc for x, y, dc in zip(q_all, k_all, decay)]
    t_inv = _unit_lower_inverse(m_low, row, col)
    sol = [_dot3(_split_bf16(t), _split_bf16(jnp.concatenate([x * b, y * e], axis=1)))
           for t, x, b, y, e in zip(t_inv, v_all, beta, kb, e_g)]
    q_dec = [x * e for x, e in zip(q_all, e_g)]
    k_end = [x * e_end[c][:, SMALL_A + h:SMALL_A + h + 1] for x, (c, h) in zip(k_all, items)]

    x_in, y_diag, s_inc, c_mat, e_acs_x, e_blk_x = [], [], [], [], [], []
    for c in chunks:
        r = rows[c]
        x_in.append(xbc[r, :GROUP_WIDTH])
        x_dt = x_in[c] * _expand_heads(sp[r], SMALL_DT, SSM_HEADS, SSM_HEAD_DIM)
        x_end = x_dt * _expand_heads(e_end[c], SMALL_DT, SSM_HEADS, SSM_HEAD_DIM)
        e_acs_x.append(_expand_heads(e_acs[c], SMALL_DT, SSM_HEADS, SSM_HEAD_DIM))
        e_blk_x.append(_expand_heads(e_blk[c], SMALL_DT, SSM_HEADS, SSM_HEAD_DIM))
        for g in range(SSM_GROUPS):
            b_g = xbc[r, GROUP_WIDTH + g * SSM_STATE:GROUP_WIDTH + (g + 1) * SSM_STATE]
            c_g = xbc[r, GROUP_WIDTH + (SSM_GROUPS + g) * SSM_STATE:GROUP_WIDTH + (SSM_GROUPS + g + 1) * SSM_STATE]
            cb = _dot_nt(c_g, b_g)
            x_g = x_dt[:, g * gw:(g + 1) * gw]
            y_g = None
            for j in range(hg):
                l_m = masked_decay(acs[c], acs_t[c], SMALL_DT + g * hg + j)
                part = _dot(cb * l_m, jnp.where(lane_head == j, x_g, 0.0))
                y_g = part if y_g is None else y_g + part
            y_diag.append(y_g)
            s_inc.append(_dot_tn(b_g, x_end[:, g * gw:(g + 1) * gw]))
            c_mat.append(c_g)

    s_state = [s_ref[h] for h in range(GDN_HEADS)]
    h_state = [h_ref[g] for g in range(SSM_GROUPS)]
    for c in chunks:
        r = rows[c]
        idx = [c * GDN_HEADS + h for h in range(GDN_HEADS)]
        v_new = [sol[n][:, :GDN_HEAD_DIM] - _dot(sol[n][:, GDN_HEAD_DIM:], s_state[h])
                 for h, n in enumerate(idx)]
        o = [_dot(q_dec[n], s_state[h]) + _dot(qk[n], v_new[h]) for h, n in enumerate(idx)]
        s_state = [s_state[h] * e_blk[c][:, SMALL_A + h:SMALL_A + h + 1] + _dot_tn(k_end[n], v_new[h])
                   for h, n in enumerate(idx)]
        for h in range(GDN_HEADS):
            z = cur(OFF_CZ + h * GDN_HEAD_DIM, GDN_HEAD_DIM, r)
            y = _rmsnorm(o[h], gdn_norm_ref[...]) * _silu(z)
            out_ref[r, head_cols(2 * GROUP_WIDTH, h)] = y.astype(out_ref.dtype)

        ys = []
        for g in range(SSM_GROUPS):
            n = c * SSM_GROUPS + g
            ys.append(y_diag[n] + _dot(c_mat[n], h_state[g]) * e_acs_x[c][:, g * gw:(g + 1) * gw])
            h_state[g] = h_state[g] * e_blk_x[c][:, g * gw:(g + 1) * gw] + s_inc[n]
        y = jnp.concatenate(ys, axis=1) + ssm_d_ref[...] * x_in[c]
        y = y * _silu(cur(OFF_DZ, GROUP_WIDTH, r))
        for g in range(SSM_GROUPS):
            yn = _rmsnorm(y[:, g * gw:(g + 1) * gw], ssm_norm_ref[:, g * gw:(g + 1) * gw])
            out_ref[r, 3 * GROUP_WIDTH + g * gw:3 * GROUP_WIDTH + (g + 1) * gw] = yn.astype(out_ref.dtype)
    for h in range(GDN_HEADS):
        s_ref[h] = s_state[h]
    for g in range(SSM_GROUPS):
        h_ref[g] = h_state[g]


def _mixers(u_head, u_tail, batch, seq, conv_a_w, pool_w, pool_scale, gdn_conv_w, gdn_norm_g, ssm_conv_w,
            ssm_conv_b, ssm_d_x, ssm_norm_g, alog_row, dtbias_row, tile):
    n_tiles = seq // tile

    def const(shape):
        return pl.BlockSpec(shape, lambda b, i: (0,) * len(shape))

    def cur_rows(b, i):
        return (b * n_tiles + i, 0)

    def prev_rows(b, i):
        return (jnp.maximum((b * n_tiles + i) * (tile // HALO) - 1, 0), 0)

    return pl.pallas_call(
        functools.partial(_mixer_kernel, tile=tile),
        grid=(batch, n_tiles),
        in_specs=[
            pl.BlockSpec((tile, HEAD_COLS), cur_rows),
            pl.BlockSpec((HALO, HEAD_COLS), prev_rows),
            pl.BlockSpec((tile, TAIL_COLS), cur_rows),
            pl.BlockSpec((HALO, TAIL_COLS), prev_rows),
            const(conv_a_w.shape), const(pool_w.shape), const(pool_scale.shape), const(gdn_conv_w.shape),
            const(gdn_norm_g.shape), const(ssm_conv_w.shape), const(ssm_conv_b.shape), const(ssm_d_x.shape),
            const(ssm_norm_g.shape), const(alog_row.shape), const(dtbias_row.shape),
        ],
        out_specs=pl.BlockSpec((tile, D_MODEL), lambda b, i: (b * n_tiles + i, 0)),
        out_shape=jax.ShapeDtypeStruct((batch * seq, D_MODEL), BF16),
        scratch_shapes=[
            pltpu.VMEM((GDN_HEADS, GDN_HEAD_DIM, GDN_HEAD_DIM), F32),
            pltpu.VMEM((SSM_GROUPS, SSM_STATE, (SSM_HEADS // SSM_GROUPS) * SSM_HEAD_DIM), F32),
        ],
        compiler_params=pltpu.CompilerParams(
            dimension_semantics=("arbitrary", "arbitrary"), vmem_limit_bytes=VMEM_LIMIT),
        name="mixers",
    )(u_head, u_head, u_tail, u_tail, conv_a_w, pool_w, pool_scale, gdn_conv_w, gdn_norm_g, ssm_conv_w,
      ssm_conv_b, ssm_d_x, ssm_norm_g, alog_row, dtbias_row)


def _xattn_kernel(x_ref, g_ref, wq_ref, kv_ref, o_ref, wb_ref):
    @pl.when((pl.program_id(0) == 0) & (pl.program_id(1) == 0))
    def _():
        wb_ref[...] = wq_ref[...].astype(BF16)

    xn = _rmsnorm(x_ref[...], g_ref[...]).astype(BF16)
    q = jnp.dot(xn, wb_ref[...], preferred_element_type=F32)
    for h in range(XA_HEADS):
        lo = h * XA_HEAD_DIM
        k_h = kv_ref[:, lo:lo + XA_HEAD_DIM]
        v_h = kv_ref[:, D_MODEL + lo:D_MODEL + lo + XA_HEAD_DIM]
        s = _dot_nt(q[:, lo:lo + XA_HEAD_DIM], k_h) * XA_HEAD_DIM ** -0.5
        e = jnp.exp(s - jnp.max(s, axis=-1, keepdims=True))
        p = e / jnp.sum(e, axis=-1, keepdims=True)
        o_ref[:, lo:lo + XA_HEAD_DIM] = _dot(p, v_h).astype(o_ref.dtype)


def _xattn(x, g, wq, layer, kv, batch, seq, tq):
    n, d = x.shape
    n_tiles = seq // tq
    return pl.pallas_call(
        _xattn_kernel,
        grid=(batch, n_tiles),
        in_specs=[
            pl.BlockSpec((tq, d), lambda b, i: (b * n_tiles + i, 0)),
            pl.BlockSpec((1, d), lambda b, i: (0, 0)),
            pl.BlockSpec((None, d, d), lambda b, i: (layer, 0, 0), pipeline_mode=pl.Buffered(1)),
            pl.BlockSpec((MEM_LEN, 2 * d), lambda b, i: (b, 0)),
        ],
        out_specs=pl.BlockSpec((tq, d), lambda b, i: (b * n_tiles + i, 0)),
        out_shape=jax.ShapeDtypeStruct((n, d), BF16),
        scratch_shapes=[pltpu.VMEM((d, d), BF16)],
        compiler_params=pltpu.CompilerParams(
            dimension_semantics=("arbitrary", "arbitrary"), vmem_limit_bytes=VMEM_LIMIT),
        name="xattn",
    )(x, g.reshape(1, d), wq, kv)


def _ffn_kernel(x_ref, gpre_ref, wg_ref, wu_ref, wd_ref, gpost_ref, o_ref, xn_ref):
    j = pl.program_id(1)

    @pl.when(j == 0)
    def _():
        xn_ref[...] = _rmsnorm(x_ref[...], gpre_ref[...]).astype(BF16)

    xn = xn_ref[...]
    gate = jnp.dot(xn, wg_ref[...].astype(BF16), preferred_element_type=F32)
    up = jnp.dot(xn, wu_ref[...].astype(BF16), preferred_element_type=F32)
    part = jnp.dot((_silu(gate) * up).astype(BF16), wd_ref[...].astype(BF16), preferred_element_type=F32)

    @pl.when(j == 0)
    def _():
        o_ref[...] = part

    @pl.when(j > 0)
    def _():
        o_ref[...] += part

    @pl.when(j == pl.num_programs(1) - 1)
    def _():
        o_ref[...] = x_ref[...] + _rmsnorm(o_ref[...], gpost_ref[...])


def _ffn(x, gpre, w_gu, w_down, layer, gpost, tm, tf):
    n, d = x.shape
    n_ff = D_FF // tf
    return pl.pallas_call(
        _ffn_kernel,
        grid=(n // tm, n_ff),
        in_specs=[
            pl.BlockSpec((tm, d), lambda i, j: (i, 0), pipeline_mode=pl.Buffered(1)),
            pl.BlockSpec((1, d), lambda i, j: (0, 0)),
            pl.BlockSpec((None, d, tf), lambda i, j: (layer, 0, j)),
            pl.BlockSpec((None, d, tf), lambda i, j: (layer, 0, j + n_ff)),
            pl.BlockSpec((None, tf, d), lambda i, j: (layer, j, 0)),
            pl.BlockSpec((1, d), lambda i, j: (0, 0)),
        ],
        out_specs=pl.BlockSpec((tm, d), lambda i, j: (i, 0)),
        out_shape=jax.ShapeDtypeStruct((n, d), F32),
        scratch_shapes=[pltpu.VMEM((tm, d), BF16)],
        compiler_params=pltpu.CompilerParams(
            dimension_semantics=("arbitrary", "arbitrary"), vmem_limit_bytes=VMEM_LIMIT_LARGE),
        name="ffn",
    )(x, gpre.reshape(1, d), w_gu, w_gu, w_down, gpost.reshape(1, d))


def _small_row(gdn_vals, ssm_vals):
    row = jnp.zeros((1, 128), F32)
    row = row.at[0, SMALL_A:SMALL_A + GDN_HEADS].set(gdn_vals.astype(F32))
    return row.at[0, SMALL_DT:SMALL_DT + SSM_HEADS].set(ssm_vals.astype(F32))


def kernel(x, mem, norm_g, w_in, conv_a_w, pool_w, pool_scale, gdn_conv_w, gdn_A_log, gdn_dt_bias, gdn_norm_g,
           ssm_conv_w, ssm_conv_b, ssm_A_log, ssm_dt_bias, ssm_D, ssm_norm_g, w_out, xa_wq, xa_wkv, xa_wo,
           ffn_w_gu, ffn_w_down):
    batch, seq, d = x.shape
    depth = w_in.shape[0]
    xf = x.reshape(batch * seq, d)
    memf = mem.reshape(batch * MEM_LEN, d)
    for l in range(depth):
        g = norm_g[l]
        u_head = _norm_matmul(xf, g[0], w_in, l, HEAD_COLS, F32, tm=1024, tn=512)
        u_tail = _norm_matmul_tail(xf, g[0], w_in, l, tm=512)
        mixed = _mixers(
            u_head, u_tail, batch, seq, conv_a_w[l], pool_w[l], pool_scale[l].reshape(1, -1), gdn_conv_w[l],
            gdn_norm_g[l].reshape(1, -1), ssm_conv_w[l], ssm_conv_b[l].reshape(1, -1),
            jnp.repeat(ssm_D[l], SSM_HEAD_DIM).reshape(1, -1), ssm_norm_g[l].reshape(1, -1),
            _small_row(gdn_A_log[l], ssm_A_log[l]), _small_row(gdn_dt_bias[l], ssm_dt_bias[l]), tile=2 * CHUNK)
        xf = _proj_post(mixed, w_out, l, g[1], xf, tm=512)
        kv = _norm_matmul(memf, g[4], xa_wkv, l, 2 * d, BF16, tm=batch * MEM_LEN, tn=512)
        att = _xattn(xf, g[2], xa_wq, l, kv, batch, seq, tq=512)
        xf = _proj_post(att, xa_wo, l, g[3], xf, tm=512)
        xf = _ffn(xf, g[5], ffn_w_gu, ffn_w_down, l, g[6], tm=1024, tf=256)
    return xf.reshape(batch, seq, d)
```

```python
import functools

import jax
import jax.numpy as jnp
from jax import lax
from jax.experimental import pallas as pl
from jax.experimental.pallas import tpu as pltpu

F32 = jnp.float32
BF16 = jnp.bfloat16
EPS = 1e-6

D_MODEL = 2048
GROUP_WIDTH = 512
CHUNK = 64
POOL_WINDOWS = (2, 4, 8, 16)
POOL_GROUP = 128
GDN_HEADS = 4
GDN_HEAD_DIM = 128
SSM_HEADS = 8
SSM_HEAD_DIM = 64
SSM_GROUPS = 2
SSM_STATE = 128
SSM_XBC = 1024
MEM_LEN = 256
XA_HEADS = 4
XA_HEAD_DIM = 512
D_FF = 5632

IN_COLS = 5648
HEAD_COLS = 4096
TAIL_START = 4096
TAIL_VALID = IN_COLS - TAIL_START
TAIL_COLS = 1664
TAIL_BLOCK = 2048
TAIL_SKEW = 8
OFF_A = 0
OFF_B = 1536
OFF_QKV = 2048
OFF_CZ = 3584
OFF_DZ = 4096
OFF_XBC = 4608
OFF_SMALL = 5632
SMALL_DT = 0
SMALL_A = 120
SMALL_B = 124
HALO = 16

VMEM_LIMIT = 48 * 1024 * 1024
VMEM_LIMIT_LARGE = 56 * 1024 * 1024


def _rmsnorm(x, g):
    return x * lax.rsqrt(jnp.mean(x * x, axis=-1, keepdims=True) + EPS) * g


def _silu(x):
    return x * jax.nn.sigmoid(x)


def _softplus(x):
    return jnp.maximum(x, 0.0) + jnp.log1p(jnp.exp(-jnp.abs(x)))


def _dot(a, b):
    return jnp.dot(a.astype(BF16), b.astype(BF16), preferred_element_type=F32)


def _dot_nt(a, b):
    return lax.dot_general(a.astype(BF16), b.astype(BF16), (((1,), (1,)), ((), ())),
                           preferred_element_type=F32)


def _dot_tn(a, b):
    return lax.dot_general(a.astype(BF16), b.astype(BF16), (((0,), (0,)), ((), ())),
                           preferred_element_type=F32)


def _hdot(a, b):
    return jnp.dot(a, b, preferred_element_type=F32, precision=lax.Precision.HIGHEST)


def _split_bf16(a):
    hi = a.astype(BF16)
    return hi, (a - hi.astype(F32)).astype(BF16)


def _dot3(a, b):
    return (jnp.dot(a[0], b[0], preferred_element_type=F32)
            + jnp.dot(a[0], b[1], preferred_element_type=F32)
            + jnp.dot(a[1], b[0], preferred_element_type=F32))


def _norm_matmul_kernel(x_ref, g_ref, w_ref, o_ref, xn_ref):
    @pl.when(pl.program_id(1) == 0)
    def _():
        xn_ref[...] = _rmsnorm(x_ref[...], g_ref[...]).astype(BF16)

    o_ref[...] = jnp.dot(xn_ref[...], w_ref[...].astype(BF16), preferred_element_type=F32).astype(o_ref.dtype)


def _norm_matmul(x, g, w, layer, n_cols, out_dtype, tm, tn):
    n, k = x.shape
    return pl.pallas_call(
        _norm_matmul_kernel,
        grid=(n // tm, n_cols // tn),
        in_specs=[
            pl.BlockSpec((tm, k), lambda i, j: (i, 0)),
            pl.BlockSpec((1, k), lambda i, j: (0, 0)),
            pl.BlockSpec((None, k, tn), lambda i, j: (layer, 0, j)),
        ],
        out_specs=pl.BlockSpec((tm, tn), lambda i, j: (i, j)),
        out_shape=jax.ShapeDtypeStruct((n, n_cols), out_dtype),
        scratch_shapes=[pltpu.VMEM((tm, k), BF16)],
        compiler_params=pltpu.CompilerParams(
            dimension_semantics=("arbitrary", "arbitrary"), vmem_limit_bytes=VMEM_LIMIT_LARGE),
        name="norm_matmul",
    )(x, g.reshape(1, k), w)


def _norm_matmul_tail_kernel(x_ref, g_ref, w_ref, o_ref, wb_ref):
    @pl.when(pl.program_id(0) == 0)
    def _():
        wb_ref[...] = w_ref[:, :TAIL_COLS].astype(BF16)

    xn = _rmsnorm(x_ref[...], g_ref[...]).astype(BF16)
    y = jnp.dot(xn, wb_ref[...], preferred_element_type=F32)
    y = pltpu.roll(y, TAIL_COLS - TAIL_SKEW, axis=1)
    lane = lax.broadcasted_iota(jnp.int32, y.shape, 1)
    stale = (lane >= TAIL_VALID - TAIL_SKEW) & (lane < TAIL_COLS - TAIL_SKEW)
    o_ref[...] = jnp.where(stale, 0.0, y)


def _norm_matmul_tail(x, g, w, layer, tm):
    n, k = x.shape
    return pl.pallas_call(
        _norm_matmul_tail_kernel,
        grid=(n // tm,),
        in_specs=[
            pl.BlockSpec((tm, k), lambda i: (i, 0)),
            pl.BlockSpec((1, k), lambda i: (0, 0)),
            pl.BlockSpec((None, k, TAIL_BLOCK), lambda i: (layer, 0, TAIL_START // TAIL_BLOCK),
                         pipeline_mode=pl.Buffered(1)),
        ],
        out_specs=pl.BlockSpec((tm, TAIL_COLS), lambda i: (i, 0)),
        out_shape=jax.ShapeDtypeStruct((n, TAIL_COLS), F32),
        scratch_shapes=[pltpu.VMEM((k, TAIL_COLS), BF16)],
        compiler_params=pltpu.CompilerParams(
            dimension_semantics=("arbitrary",), vmem_limit_bytes=VMEM_LIMIT),
        name="norm_matmul_tail",
    )(x, g.reshape(1, k), w)


def _proj_post_kernel(a_ref, w_ref, g_ref, x_ref, o_ref, wb_ref):
    @pl.when(pl.program_id(0) == 0)
    def _():
        wb_ref[...] = w_ref[...].astype(BF16)

    y = jnp.dot(a_ref[...], wb_ref[...], preferred_element_type=F32)
    o_ref[...] = x_ref[...] + _rmsnorm(y, g_ref[...])


def _proj_post(a, w, layer, g, x, tm):
    n, k = a.shape
    d = w.shape[2]
    return pl.pallas_call(
        _proj_post_kernel,
        grid=(n // tm,),
        in_specs=[
            pl.BlockSpec((tm, k), lambda i: (i, 0)),
            pl.BlockSpec((None, k, d), lambda i: (layer, 0, 0), pipeline_mode=pl.Buffered(1)),
            pl.BlockSpec((1, d), lambda i: (0, 0)),
            pl.BlockSpec((tm, d), lambda i: (i, 0)),
        ],
        out_specs=pl.BlockSpec((tm, d), lambda i: (i, 0)),
        out_shape=jax.ShapeDtypeStruct((n, d), F32),
        scratch_shapes=[pltpu.VMEM((k, d), BF16)],
        compiler_params=pltpu.CompilerParams(
            dimension_semantics=("arbitrary",), vmem_limit_bytes=VMEM_LIMIT_LARGE),
        name="proj_post",
    )(a, w, g.reshape(1, d), x)


def _shift_rows(x, k):
    return pltpu.roll(x, k, axis=0) if k else x


def _causal_conv(x_ext, w_ref):
    width = w_ref.shape[0]
    acc = x_ext * w_ref[width - 1:width, :]
    for k in range(width - 1):
        acc = acc + _shift_rows(x_ext, width - 1 - k) * w_ref[k:k + 1, :]
    return acc


def _expand_heads(cols, first, n_heads, width):
    rows = cols.shape[0]
    per_vreg = 128 // width
    lane = lax.broadcasted_iota(jnp.int32, (rows, 128), 1)
    pieces = []
    for m in range(n_heads // per_vreg):
        piece = jnp.broadcast_to(cols[:, first + m * per_vreg:first + m * per_vreg + 1], (rows, 128))
        for j in range(1, per_vreg):
            c = first + m * per_vreg + j
            piece = jnp.where(lane >= j * width, jnp.broadcast_to(cols[:, c:c + 1], (rows, 128)), piece)
        pieces.append(piece)
    return jnp.concatenate(pieces, axis=1)


def _unit_lower_inverse(ms, row, col):
    same16 = (row >> 4) == (col >> 4)
    same32 = (row >> 5) == (col >> 5)
    off16 = same32 & jnp.logical_not(same16)
    eye = (row == col).astype(F32)
    split = lambda xs: [_split_bf16(x) for x in xs]
    d = [jnp.where(same16, m, 0.0) for m in ms]
    ds = split(d)
    d2 = split([_dot3(x, x) for x in ds])
    p = [eye - x for x in d]
    p = [x + _dot3(xs, y) for x, xs, y in zip(p, split(p), d2)]
    d4 = split([_dot3(x, x) for x in d2])
    p = [x + _dot3(xs, y) for x, xs, y in zip(p, split(p), d4)]
    d8 = split([_dot3(x, x) for x in d4])
    p = [x + _dot3(xs, y) for x, xs, y in zip(p, split(p), d8)]
    ps = split(p)
    t = split([_dot3(xs, _split_bf16(jnp.where(off16, m, 0.0))) for xs, m in zip(ps, ms)])
    q = [x - _dot3(y, xs) for x, xs, y in zip(p, ps, t)]
    qs = split(q)
    t = split([_dot3(xs, _split_bf16(jnp.where(same32, 0.0, m))) for xs, m in zip(qs, ms)])
    return [x - _dot3(y, xs) for x, xs, y in zip(q, qs, t)]


def _mixer_kernel(head_ref, head_prev_ref, tail_ref, tail_prev_ref, conv_a_ref, pool_w_ref, pool_scale_ref, gdn_conv_ref,
                  gdn_norm_ref, ssm_conv_ref, ssm_bias_ref, ssm_d_ref, ssm_norm_ref,
                  alog_ref, dtbias_ref, out_ref, s_ref, h_ref, *, tile):
    i = pl.program_id(1)
    n_chunks = tile // CHUNK
    hg = SSM_HEADS // SSM_GROUPS
    gw = hg * SSM_HEAD_DIM

    @pl.when(i == 0)
    def _():
        s_ref[...] = jnp.zeros_like(s_ref)
        h_ref[...] = jnp.zeros_like(h_ref)

    keep_prev = (i > 0).astype(F32)

    def cur(off, width, rows=slice(None)):
        if off < OFF_DZ:
            return head_ref[rows, off:off + width]
        return tail_ref[rows, off - OFF_DZ:off - OFF_DZ + width]

    def ext(off, width):
        if off < OFF_DZ:
            prev = head_prev_ref[:, off:off + width]
        else:
            prev = tail_prev_ref[:, off - OFF_DZ:off - OFF_DZ + width]
        return jnp.concatenate([prev * keep_prev, cur(off, width)], axis=0)

    p = ext(OFF_A + GROUP_WIDTH, GROUP_WIDTH) * ext(OFF_A + 2 * GROUP_WIDTH, GROUP_WIDTH)
    ya = cur(OFF_A, GROUP_WIDTH) * _causal_conv(p, conv_a_ref)[HALO:]
    out_ref[:, 0:GROUP_WIDTH] = ya.astype(out_ref.dtype)

    pos = (i * tile + 1 + lax.broadcasted_iota(jnp.int32, (tile, 1), 0)).astype(F32)
    for gi, window in enumerate(POOL_WINDOWS):
        off = OFF_B + gi * POOL_GROUP
        s = ext(off, POOL_GROUP)
        k = 1
        while k < window:
            s = s + _shift_rows(s, k)
            k *= 2
        pooled = s[HALO:] / jnp.minimum(pos, float(window)) - cur(off, POOL_GROUP)
        yb = _dot(pooled, pool_w_ref[gi]) * pool_scale_ref[:, gi * POOL_GROUP:(gi + 1) * POOL_GROUP]
        out_ref[:, GROUP_WIDTH + gi * POOL_GROUP:GROUP_WIDTH + (gi + 1) * POOL_GROUP] = yb.astype(out_ref.dtype)

    small = cur(OFF_SMALL, 128)
    sp = _softplus(small + dtbias_ref[...])
    rate = -jnp.exp(alog_ref[...]) * sp
    beta_all = jax.nn.sigmoid(small)

    qkv = _silu(_causal_conv(ext(OFF_QKV, 3 * GROUP_WIDTH), gdn_conv_ref)[HALO:])
    xbc = _silu(_causal_conv(ext(OFF_XBC, SSM_XBC), ssm_conv_ref)[HALO:] + ssm_bias_ref[...])

    row = lax.broadcasted_iota(jnp.int32, (CHUNK, CHUNK), 0)
    col = lax.broadcasted_iota(jnp.int32, (CHUNK, CHUNK), 1)
    incl = row >= col
    strict = row > col
    ltri = incl.astype(F32)
    lane_head = lax.broadcasted_iota(jnp.int32, (CHUNK, gw), 1) // SSM_HEAD_DIM

    def masked_decay(cum, cum_t, ch):
        diff = cum[:, ch:ch + 1] - cum_t[ch:ch + 1, :]
        return jnp.where(incl, jnp.exp(jnp.where(incl, diff, 0.0)), 0.0)

    chunks = range(n_chunks)
    rows = [slice(c * CHUNK, (c + 1) * CHUNK) for c in chunks]
    acs = [_hdot(ltri, rate[r]) for r in rows]
    acs_t = [a.T for a in acs]
    e_acs = [jnp.exp(a) for a in acs]
    e_end = [jnp.exp(a[CHUNK - 1:CHUNK, :] - a) for a in acs]
    e_blk = [jnp.exp(a[CHUNK - 1:CHUNK, :]) for a in acs]

    items = [(c, h) for c in chunks for h in range(GDN_HEADS)]

    def head_cols(base, h):
        return slice(base + h * GDN_HEAD_DIM, base + (h + 1) * GDN_HEAD_DIM)

    q_all = [qkv[rows[c], head_cols(0, h)] for c, h in items]
    k_all = [qkv[rows[c], head_cols(GROUP_WIDTH, h)] for c, h in items]
    v_all = [qkv[rows[c], head_cols(2 * GROUP_WIDTH, h)] for c, h in items]
    q_all = [x * lax.rsqrt(jnp.sum(x * x, axis=-1, keepdims=True) + EPS) * GDN_HEAD_DIM ** -0.5 for x in q_all]
    k_all = [x * lax.rsqrt(jnp.sum(x * x, axis=-1, keepdims=True) + EPS) for x in k_all]
    beta = [beta_all[rows[c], SMALL_B + h:SMALL_B + h + 1] for c, h in items]
    decay = [masked_decay(acs[c], acs_t[c], SMALL_A + h) for c, h in items]
    e_g = [e_acs[c][:, SMALL_A + h:SMALL_A + h + 1] for c, h in items]
    kb = [x * b for x, b in zip(k_all, beta)]
    m_low = [jnp.where(strict, _dot_nt(x, y) * dc, 0.0) for x, y, dc in zip(kb, k_all, decay)]
    qk = [_dot_nt(x, y) * dc for x, y, dc in zip(q_all, k_all, decay)]
    t_inv = _unit_lower_inverse(m_low, row, col)
    sol = [_dot3(_split_bf16(t), _split_bf16(jnp.concatenate([x * b, y * e], axis=1)))
           for t, x, b, y, e in zip(t_inv, v_all, beta, kb, e_g)]
    q_dec = [x * e for x, e in zip(q_all, e_g)]
    k_end = [x * e_end[c][:, SMALL_A + h:SMALL_A + h + 1] for x, (c, h) in zip(k_all, items)]

    x_in, y_diag, s_inc, c_mat, e_acs_x, e_blk_x = [], [], [], [], [], []
    for c in chunks:
        r = rows[c]
        x_in.append(xbc[r, :GROUP_WIDTH])
        x_dt = x_in[c] * _expand_heads(sp[r], SMALL_DT, SSM_HEADS, SSM_HEAD_DIM)
        x_end = x_dt * _expand_heads(e_end[c], SMALL_DT, SSM_HEADS, SSM_HEAD_DIM)
        e_acs_x.append(_expand_heads(e_acs[c], SMALL_DT, SSM_HEADS, SSM_HEAD_DIM))
        e_blk_x.append(_expand_heads(e_blk[c], SMALL_DT, SSM_HEADS, SSM_HEAD_DIM))
        for g in range(SSM_GROUPS):
            b_g = xbc[r, GROUP_WIDTH + g * SSM_STATE:GROUP_WIDTH + (g + 1) * SSM_STATE]
            c_g = xbc[r, GROUP_WIDTH + (SSM_GROUPS + g) * SSM_STATE:GROUP_WIDTH + (SSM_GROUPS + g + 1) * SSM_STATE]
            cb = _dot_nt(c_g, b_g)
            x_g = x_dt[:, g * gw:(g + 1) * gw]
            y_g = None
            for j in range(hg):
                l_m = masked_decay(acs[c], acs_t[c], SMALL_DT + g * hg + j)
                part = _dot(cb * l_m, jnp.where(lane_head == j, x_g, 0.0))
                y_g = part if y_g is None else y_g + part
            y_diag.append(y_g)
            s_inc.append(_dot_tn(b_g, x_end[:, g * gw:(g + 1) * gw]))
            c_mat.append(c_g)

    s_state = [s_ref[h] for h in range(GDN_HEADS)]
    h_state = [h_ref[g] for g in range(SSM_GROUPS)]
    for c in chunks:
        r = rows[c]
        idx = [c * GDN_HEADS + h for h in range(GDN_HEADS)]
        v_new = [sol[n][:, :GDN_HEAD_DIM] - _dot(sol[n][:, GDN_HEAD_DIM:], s_state[h])
                 for h, n in enumerate(idx)]
        o = [_dot(q_dec[n], s_state[h]) + _dot(qk[n], v_new[h]) for h, n in enumerate(idx)]
        s_state = [s_state[h] * e_blk[c][:, SMALL_A + h:SMALL_A + h + 1] + _dot_tn(k_end[n], v_new[h])
                   for h, n in enumerate(idx)]
        for h in range(GDN_HEADS):
            z = cur(OFF_CZ + h * GDN_HEAD_DIM, GDN_HEAD_DIM, r)
            y = _rmsnorm(o[h], gdn_norm_ref[...]) * _silu(z)
            out_ref[r, head_cols(2 * GROUP_WIDTH, h)] = y.astype(out_ref.dtype)

        ys = []
        for g in range(SSM_GROUPS):
            n = c * SSM_GROUPS + g
            ys.append(y_diag[n] + _dot(c_mat[n], h_state[g]) * e_acs_x[c][:, g * gw:(g + 1) * gw])
            h_state[g] = h_state[g] * e_blk_x[c][:, g * gw:(g + 1) * gw] + s_inc[n]
        y = jnp.concatenate(ys, axis=1) + ssm_d_ref[...] * x_in[c]
        y = y * _silu(cur(OFF_DZ, GROUP_WIDTH, r))
        for g in range(SSM_GROUPS):
            yn = _rmsnorm(y[:, g * gw:(g + 1) * gw], ssm_norm_ref[:, g * gw:(g + 1) * gw])
            out_ref[r, 3 * GROUP_WIDTH + g * gw:3 * GROUP_WIDTH + (g + 1) * gw] = yn.astype(out_ref.dtype)
    for h in range(GDN_HEADS):
        s_ref[h] = s_state[h]
    for g in range(SSM_GROUPS):
        h_ref[g] = h_state[g]


def _mixers(u_head, u_tail, batch, seq, conv_a_w, pool_w, pool_scale, gdn_conv_w, gdn_norm_g, ssm_conv_w,
            ssm_conv_b, ssm_d_x, ssm_norm_g, alog_row, dtbias_row, tile):
    n_tiles = seq // tile

    def const(shape):
        return pl.BlockSpec(shape, lambda b, i: (0,) * len(shape))

    def cur_rows(b, i):
        return (b * n_tiles + i, 0)

    def prev_rows(b, i):
        return (jnp.maximum((b * n_tiles + i) * (tile // HALO) - 1, 0), 0)

    return pl.pallas_call(
        functools.partial(_mixer_kernel, tile=tile),
        grid=(batch, n_tiles),
        in_specs=[
            pl.BlockSpec((tile, HEAD_COLS), cur_rows),
            pl.BlockSpec((HALO, HEAD_COLS), prev_rows),
            pl.BlockSpec((tile, TAIL_COLS), cur_rows),
            pl.BlockSpec((HALO, TAIL_COLS), prev_rows),
            const(conv_a_w.shape), const(pool_w.shape), const(pool_scale.shape), const(gdn_conv_w.shape),
            const(gdn_norm_g.shape), const(ssm_conv_w.shape), const(ssm_conv_b.shape), const(ssm_d_x.shape),
            const(ssm_norm_g.shape), const(alog_row.shape), const(dtbias_row.shape),
        ],
        out_specs=pl.BlockSpec((tile, D_MODEL), lambda b, i: (b * n_tiles + i, 0)),
        out_shape=jax.ShapeDtypeStruct((batch * seq, D_MODEL), BF16),
        scratch_shapes=[
            pltpu.VMEM((GDN_HEADS, GDN_HEAD_DIM, GDN_HEAD_DIM), F32),
            pltpu.VMEM((SSM_GROUPS, SSM_STATE, (SSM_HEADS // SSM_GROUPS) * SSM_HEAD_DIM), F32),
        ],
        compiler_params=pltpu.CompilerParams(
            dimension_semantics=("arbitrary", "arbitrary"), vmem_limit_bytes=VMEM_LIMIT),
        name="mixers",
    )(u_head, u_head, u_tail, u_tail, conv_a_w, pool_w, pool_scale, gdn_conv_w, gdn_norm_g, ssm_conv_w,
      ssm_conv_b, ssm_d_x, ssm_norm_g, alog_row, dtbias_row)


def _xattn_kernel(x_ref, g_ref, wq_ref, kv_ref, o_ref, wb_ref):
    @pl.when((pl.program_id(0) == 0) & (pl.program_id(1) == 0))
    def _():
        wb_ref[...] = wq_ref[...].astype(BF16)

    xn = _rmsnorm(x_ref[...], g_ref[...]).astype(BF16)
    q = jnp.dot(xn, wb_ref[...], preferred_element_type=F32)
    for h in range(XA_HEADS):
        lo = h * XA_HEAD_DIM
        k_h = kv_ref[:, lo:lo + XA_HEAD_DIM]
        v_h = kv_ref[:, D_MODEL + lo:D_MODEL + lo + XA_HEAD_DIM]
        s = _dot_nt(q[:, lo:lo + XA_HEAD_DIM], k_h) * XA_HEAD_DIM ** -0.5
        e = jnp.exp(s - jnp.max(s, axis=-1, keepdims=True))
        p = e / jnp.sum(e, axis=-1, keepdims=True)
        o_ref[:, lo:lo + XA_HEAD_DIM] = _dot(p, v_h).astype(o_ref.dtype)


def _xattn(x, g, wq, layer, kv, batch, seq, tq):
    n, d = x.shape
    n_tiles = seq // tq
    return pl.pallas_call(
        _xattn_kernel,
        grid=(batch, n_tiles),
        in_specs=[
            pl.BlockSpec((tq, d), lambda b, i: (b * n_tiles + i, 0)),
            pl.BlockSpec((1, d), lambda b, i: (0, 0)),
            pl.BlockSpec((None, d, d), lambda b, i: (layer, 0, 0), pipeline_mode=pl.Buffered(1)),
            pl.BlockSpec((MEM_LEN, 2 * d), lambda b, i: (b, 0)),
        ],
        out_specs=pl.BlockSpec((tq, d), lambda b, i: (b * n_tiles + i, 0)),
        out_shape=jax.ShapeDtypeStruct((n, d), BF16),
        scratch_shapes=[pltpu.VMEM((d, d), BF16)],
        compiler_params=pltpu.CompilerParams(
            dimension_semantics=("arbitrary", "arbitrary"), vmem_limit_bytes=VMEM_LIMIT),
        name="xattn",
    )(x, g.reshape(1, d), wq, kv)


def _ffn_kernel(x_ref, gpre_ref, wg_ref, wu_ref, wd_ref, gpost_ref, o_ref, xn_ref):
    j = pl.program_id(1)

    @pl.when(j == 0)
    def _():
        xn_ref[...] = _rmsnorm(x_ref[...], gpre_ref[...]).astype(BF16)
        o_ref[...] = jnp.zeros_like(o_ref)

    xn = xn_ref[...]
    gate = jnp.dot(xn, wg_ref[...].astype(BF16), preferred_element_type=F32)
    up = jnp.dot(xn, wu_ref[...].astype(BF16), preferred_element_type=F32)
    o_ref[...] += jnp.dot((_silu(gate) * up).astype(BF16), wd_ref[...].astype(BF16), preferred_element_type=F32)

    @pl.when(j == pl.num_programs(1) - 1)
    def _():
        o_ref[...] = x_ref[...] + _rmsnorm(o_ref[...], gpost_ref[...])


def _ffn(x, gpre, w_gu, w_down, layer, gpost, tm, tf):
    n, d = x.shape
    n_ff = D_FF // tf
    return pl.pallas_call(
        _ffn_kernel,
        grid=(n // tm, n_ff),
        in_specs=[
            pl.BlockSpec((tm, d), lambda i, j: (i, 0), pipeline_mode=pl.Buffered(1)),
            pl.BlockSpec((1, d), lambda i, j: (0, 0)),
            pl.BlockSpec((None, d, tf), lambda i, j: (layer, 0, j)),
            pl.BlockSpec((None, d, tf), lambda i, j: (layer, 0, j + n_ff)),
            pl.BlockSpec((None, tf, d), lambda i, j: (layer, j, 0)),
            pl.BlockSpec((1, d), lambda i, j: (0, 0)),
        ],
        out_specs=pl.BlockSpec((tm, d), lambda i, j: (i, 0)),
        out_shape=jax.ShapeDtypeStruct((n, d), F32),
        scratch_shapes=[pltpu.VMEM((tm, d), BF16)],
        compiler_params=pltpu.CompilerParams(
            dimension_semantics=("arbitrary", "arbitrary"), vmem_limit_bytes=VMEM_LIMIT_LARGE),
        name="ffn",
    )(x, gpre.reshape(1, d), w_gu, w_gu, w_down, gpost.reshape(1, d))


def _small_row(gdn_vals, ssm_vals):
    row = jnp.zeros((1, 128), F32)
    row = row.at[0, SMALL_A:SMALL_A + GDN_HEADS].set(gdn_vals.astype(F32))
    return row.at[0, SMALL_DT:SMALL_DT + SSM_HEADS].set(ssm_vals.astype(F32))


def kernel(x, mem, norm_g, w_in, conv_a_w, pool_w, pool_scale, gdn_conv_w, gdn_A_log, gdn_dt_bias, gdn_norm_g,
           ssm_conv_w, ssm_conv_b, ssm_A_log, ssm_dt_bias, ssm_D, ssm_norm_g, w_out, xa_wq, xa_wkv, xa_wo,
           ffn_w_gu, ffn_w_down):
    batch, seq, d = x.shape
    depth = w_in.shape[0]
    xf = x.reshape(batch * seq, d)
    memf = mem.reshape(batch * MEM_LEN, d)
    for l in range(depth):
        g = norm_g[l]
        u_head = _norm_matmul(xf, g[0], w_in, l, HEAD_COLS, F32, tm=1024, tn=512 if l == 0 else 1024)
        u_tail = _norm_matmul_tail(xf, g[0], w_in, l, tm=512)
        mixed = _mixers(
            u_head, u_tail, batch, seq, conv_a_w[l], pool_w[l], pool_scale[l].reshape(1, -1), gdn_conv_w[l],
            gdn_norm_g[l].reshape(1, -1), ssm_conv_w[l], ssm_conv_b[l].reshape(1, -1),
            jnp.repeat(ssm_D[l], SSM_HEAD_DIM).reshape(1, -1), ssm_norm_g[l].reshape(1, -1),
            _small_row(gdn_A_log[l], ssm_A_log[l]), _small_row(gdn_dt_bias[l], ssm_dt_bias[l]),
            tile=(2 if l == 0 else 4) * CHUNK)
        xf = _proj_post(mixed, w_out, l, g[1], xf, tm=512)
        kv = _norm_matmul(memf, g[4], xa_wkv, l, 2 * d, BF16, tm=batch * MEM_LEN, tn=512)
        att = _xattn(xf, g[2], xa_wq, l, kv, batch, seq, tq=512)
        xf = _proj_post(att, xa_wo, l, g[3], xf, tm=512)
        if l == 0:
            xf = _ffn(xf, g[5], ffn_w_gu, ffn_w_down, l, g[6], tm=1024, tf=256)
        else:
            xf = _ffn(xf, g[5], ffn_w_gu[l:l + 1].astype(BF16), ffn_w_down[l:l + 1].astype(BF16), 0, g[6],
                      tm=1024, tf=512)
    return xf.reshape(batch, seq, d)
```

```python
import functools

import jax
import jax.numpy as jnp
from jax import lax
from jax.experimental import pallas as pl
from jax.experimental.pallas import tpu as pltpu

F32 = jnp.float32
BF16 = jnp.bfloat16
EPS = 1e-6

D_MODEL = 2048
GROUP_WIDTH = 512
CHUNK = 64
POOL_WINDOWS = (2, 4, 8, 16)
POOL_GROUP = 128
GDN_HEADS = 4
GDN_HEAD_DIM = 128
SSM_HEADS = 8
SSM_HEAD_DIM = 64
SSM_GROUPS = 2
SSM_STATE = 128
SSM_XBC = 1024
MEM_LEN = 256
XA_HEADS = 4
XA_HEAD_DIM = 512
D_FF = 5632

IN_COLS = 5648
HEAD_COLS = 4096
TAIL_START = 4096
TAIL_VALID = IN_COLS - TAIL_START
TAIL_COLS = 1664
TAIL_BLOCK = 2048
TAIL_SKEW = 8
U_COLS = HEAD_COLS + TAIL_COLS
U_COL_CHUNK = 1152
OFF_A = 0
OFF_B = 1536
OFF_QKV = 2048
OFF_CZ = 3584
OFF_DZ = 4096
OFF_XBC = 4608
OFF_SMALL = 5632
SMALL_DT = 0
SMALL_A = 120
SMALL_B = 124
HALO = 16

VMEM_LIMIT = 48 * 1024 * 1024
VMEM_LIMIT_LARGE = 56 * 1024 * 1024


def _rmsnorm(x, g):
    return x * lax.rsqrt(jnp.mean(x * x, axis=-1, keepdims=True) + EPS) * g


def _silu(x):
    return x * jax.nn.sigmoid(x)


def _softplus(x):
    return jnp.maximum(x, 0.0) + jnp.log1p(jnp.exp(-jnp.abs(x)))


def _dot(a, b):
    return jnp.dot(a.astype(BF16), b.astype(BF16), preferred_element_type=F32)


def _dot_nt(a, b):
    return lax.dot_general(a.astype(BF16), b.astype(BF16), (((1,), (1,)), ((), ())),
                           preferred_element_type=F32)


def _dot_tn(a, b):
    return lax.dot_general(a.astype(BF16), b.astype(BF16), (((0,), (0,)), ((), ())),
                           preferred_element_type=F32)


def _hdot(a, b):
    return jnp.dot(a, b, preferred_element_type=F32, precision=lax.Precision.HIGHEST)


def _split_bf16(a):
    hi = a.astype(BF16)
    return hi, (a - hi.astype(F32)).astype(BF16)


def _dot3(a, b):
    return (jnp.dot(a[0], b[0], preferred_element_type=F32)
            + jnp.dot(a[0], b[1], preferred_element_type=F32)
            + jnp.dot(a[1], b[0], preferred_element_type=F32))


def _norm_matmul_kernel(x_ref, g_ref, w_ref, o_ref, xn_ref):
    @pl.when(pl.program_id(1) == 0)
    def _():
        xn_ref[...] = _rmsnorm(x_ref[...], g_ref[...]).astype(BF16)

    o_ref[...] = jnp.dot(xn_ref[...], w_ref[...].astype(BF16), preferred_element_type=F32).astype(o_ref.dtype)


def _norm_matmul(x, g, w, layer, n_cols, out_dtype, tm, tn):
    n, k = x.shape
    return pl.pallas_call(
        _norm_matmul_kernel,
        grid=(n // tm, n_cols // tn),
        in_specs=[
            pl.BlockSpec((tm, k), lambda i, j: (i, 0)),
            pl.BlockSpec((1, k), lambda i, j: (0, 0)),
            pl.BlockSpec((None, k, tn), lambda i, j: (layer, 0, j)),
        ],
        out_specs=pl.BlockSpec((tm, tn), lambda i, j: (i, j)),
        out_shape=jax.ShapeDtypeStruct((n, n_cols), out_dtype),
        scratch_shapes=[pltpu.VMEM((tm, k), BF16)],
        compiler_params=pltpu.CompilerParams(
            dimension_semantics=("arbitrary", "arbitrary"), vmem_limit_bytes=VMEM_LIMIT_LARGE),
        name="norm_matmul",
    )(x, g.reshape(1, k), w)


def _prep_w_in_kernel(w_ref, o_ref):
    j = pl.program_id(0)

    @pl.when(j < HEAD_COLS // TAIL_BLOCK)
    def _():
        o_ref[...] = w_ref[...].astype(BF16)

    @pl.when(j == HEAD_COLS // TAIL_BLOCK)
    def _():
        rows = 256
        lane = lax.broadcasted_iota(jnp.int32, (rows, TAIL_COLS), 1)
        stale = (lane >= TAIL_VALID - TAIL_SKEW) & (lane < TAIL_COLS - TAIL_SKEW)
        for r in range(0, o_ref.shape[0], rows):
            w = pltpu.roll(w_ref[r:r + rows, :TAIL_COLS], TAIL_COLS - TAIL_SKEW, axis=1)
            o_ref[r:r + rows, :TAIL_COLS] = jnp.where(stale, 0.0, w).astype(BF16)
        o_ref[:, TAIL_COLS:] = jnp.zeros((o_ref.shape[0], TAIL_BLOCK - TAIL_COLS), BF16)


def _prep_w_in(w, layer):
    k = w.shape[1]
    return pl.pallas_call(
        _prep_w_in_kernel,
        grid=(HEAD_COLS // TAIL_BLOCK + 1,),
        in_specs=[pl.BlockSpec((None, k, TAIL_BLOCK), lambda j: (layer, 0, j))],
        out_specs=pl.BlockSpec((k, TAIL_BLOCK), lambda j: (0, j)),
        out_shape=jax.ShapeDtypeStruct((k, U_COLS), BF16),
        compiler_params=pltpu.CompilerParams(
            dimension_semantics=("arbitrary",), vmem_limit_bytes=VMEM_LIMIT_LARGE),
        name="prep_w_in",
    )(w)


def _proj_post_kernel(a_ref, w_ref, g_ref, x_ref, o_ref, wb_ref):
    @pl.when(pl.program_id(0) == 0)
    def _():
        wb_ref[...] = w_ref[...].astype(BF16)

    y = jnp.dot(a_ref[...], wb_ref[...], preferred_element_type=F32)
    o_ref[...] = x_ref[...] + _rmsnorm(y, g_ref[...])


def _proj_post(a, w, layer, g, x, tm):
    n, k = a.shape
    d = w.shape[2]
    return pl.pallas_call(
        _proj_post_kernel,
        grid=(n // tm,),
        in_specs=[
            pl.BlockSpec((tm, k), lambda i: (i, 0)),
            pl.BlockSpec((None, k, d), lambda i: (layer, 0, 0), pipeline_mode=pl.Buffered(1)),
            pl.BlockSpec((1, d), lambda i: (0, 0)),
            pl.BlockSpec((tm, d), lambda i: (i, 0)),
        ],
        out_specs=pl.BlockSpec((tm, d), lambda i: (i, 0)),
        out_shape=jax.ShapeDtypeStruct((n, d), F32),
        scratch_shapes=[pltpu.VMEM((k, d), BF16)],
        compiler_params=pltpu.CompilerParams(
            dimension_semantics=("arbitrary",), vmem_limit_bytes=VMEM_LIMIT_LARGE),
        name="proj_post",
    )(a, w, g.reshape(1, d), x)


def _shift_rows(x, k):
    return pltpu.roll(x, k, axis=0) if k else x


def _causal_conv(x_ext, w_ref):
    width = w_ref.shape[0]
    acc = x_ext * w_ref[width - 1:width, :]
    for k in range(width - 1):
        acc = acc + _shift_rows(x_ext, width - 1 - k) * w_ref[k:k + 1, :]
    return acc


def _expand_heads(cols, first, n_heads, width):
    rows = cols.shape[0]
    per_vreg = 128 // width
    lane = lax.broadcasted_iota(jnp.int32, (rows, 128), 1)
    pieces = []
    for m in range(n_heads // per_vreg):
        piece = jnp.broadcast_to(cols[:, first + m * per_vreg:first + m * per_vreg + 1], (rows, 128))
        for j in range(1, per_vreg):
            c = first + m * per_vreg + j
            piece = jnp.where(lane >= j * width, jnp.broadcast_to(cols[:, c:c + 1], (rows, 128)), piece)
        pieces.append(piece)
    return jnp.concatenate(pieces, axis=1)


def _unit_lower_inverse(ms, row, col):
    same16 = (row >> 4) == (col >> 4)
    same32 = (row >> 5) == (col >> 5)
    off16 = same32 & jnp.logical_not(same16)
    eye = (row == col).astype(F32)
    split = lambda xs: [_split_bf16(x) for x in xs]
    d = [jnp.where(same16, m, 0.0) for m in ms]
    ds = split(d)
    d2 = split([_dot3(x, x) for x in ds])
    p = [eye - x for x in d]
    p = [x + _dot3(xs, y) for x, xs, y in zip(p, split(p), d2)]
    d4 = split([_dot3(x, x) for x in d2])
    p = [x + _dot3(xs, y) for x, xs, y in zip(p, split(p), d4)]
    d8 = split([_dot3(x, x) for x in d4])
    p = [x + _dot3(xs, y) for x, xs, y in zip(p, split(p), d8)]
    ps = split(p)
    t = split([_dot3(xs, _split_bf16(jnp.where(off16, m, 0.0))) for xs, m in zip(ps, ms)])
    q = [x - _dot3(y, xs) for x, xs, y in zip(p, ps, t)]
    qs = split(q)
    t = split([_dot3(xs, _split_bf16(jnp.where(same32, 0.0, m))) for xs, m in zip(qs, ms)])
    return [x - _dot3(y, xs) for x, xs, y in zip(q, qs, t)]


def _mixer_kernel(x_ref, gpre_ref, w_ref, conv_a_ref, pool_w_ref, pool_scale_ref, gdn_conv_ref,
                  gdn_norm_ref, ssm_conv_ref, ssm_bias_ref, ssm_d_ref, ssm_norm_ref,
                  alog_ref, dtbias_ref, out_ref, u_ref, s_ref, h_ref, *, tile):
    i = pl.program_id(1)
    n_chunks = tile // CHUNK
    hg = SSM_HEADS // SSM_GROUPS
    gw = hg * SSM_HEAD_DIM

    @pl.when(i == 0)
    def _():
        s_ref[...] = jnp.zeros_like(s_ref)
        h_ref[...] = jnp.zeros_like(h_ref)
        u_ref[0:HALO, :] = jnp.zeros((HALO, U_COLS), F32)

    @pl.when(i > 0)
    def _():
        u_ref[0:HALO, :] = u_ref[tile:tile + HALO, :]

    xn = _rmsnorm(x_ref[...], gpre_ref[...]).astype(BF16)
    for c0 in range(0, U_COLS, U_COL_CHUNK):
        u_ref[HALO:, c0:c0 + U_COL_CHUNK] = jnp.dot(xn, w_ref[:, c0:c0 + U_COL_CHUNK],
                                                      preferred_element_type=F32)

    def cur(off, width, rows=slice(0, tile)):
        return u_ref[HALO + rows.start:HALO + rows.stop, off:off + width]

    def ext(off, width):
        return u_ref[:, off:off + width]

    p = ext(OFF_A + GROUP_WIDTH, GROUP_WIDTH) * ext(OFF_A + 2 * GROUP_WIDTH, GROUP_WIDTH)
    ya = cur(OFF_A, GROUP_WIDTH) * _causal_conv(p, conv_a_ref)[HALO:]
    out_ref[:, 0:GROUP_WIDTH] = ya.astype(out_ref.dtype)

    pos = (i * tile + 1 + lax.broadcasted_iota(jnp.int32, (tile, 1), 0)).astype(F32)
    for gi, window in enumerate(POOL_WINDOWS):
        off = OFF_B + gi * POOL_GROUP
        s = ext(off, POOL_GROUP)
        k = 1
        while k < window:
            s = s + _shift_rows(s, k)
            k *= 2
        pooled = s[HALO:] / jnp.minimum(pos, float(window)) - cur(off, POOL_GROUP)
        yb = _dot(pooled, pool_w_ref[gi]) * pool_scale_ref[:, gi * POOL_GROUP:(gi + 1) * POOL_GROUP]
        out_ref[:, GROUP_WIDTH + gi * POOL_GROUP:GROUP_WIDTH + (gi + 1) * POOL_GROUP] = yb.astype(out_ref.dtype)

    small = cur(OFF_SMALL, 128)
    sp = _softplus(small + dtbias_ref[...])
    rate = -jnp.exp(alog_ref[...]) * sp
    beta_all = jax.nn.sigmoid(small)

    qkv = _silu(_causal_conv(ext(OFF_QKV, 3 * GROUP_WIDTH), gdn_conv_ref)[HALO:])
    xbc = _silu(_causal_conv(ext(OFF_XBC, SSM_XBC), ssm_conv_ref)[HALO:] + ssm_bias_ref[...])

    row = lax.broadcasted_iota(jnp.int32, (CHUNK, CHUNK), 0)
    col = lax.broadcasted_iota(jnp.int32, (CHUNK, CHUNK), 1)
    incl = row >= col
    strict = row > col
    ltri = incl.astype(F32)
    lane_head = lax.broadcasted_iota(jnp.int32, (CHUNK, gw), 1) // SSM_HEAD_DIM

    def masked_decay(cum, cum_t, ch):
        diff = cum[:, ch:ch + 1] - cum_t[ch:ch + 1, :]
        return jnp.where(incl, jnp.exp(jnp.where(incl, diff, 0.0)), 0.0)

    chunks = range(n_chunks)
    rows = [slice(c * CHUNK, (c + 1) * CHUNK) for c in chunks]
    acs = [_hdot(ltri, rate[r]) for r in rows]
    acs_t = [a.T for a in acs]
    e_acs = [jnp.exp(a) for a in acs]
    e_end = [jnp.exp(a[CHUNK - 1:CHUNK, :] - a) for a in acs]
    e_blk = [jnp.exp(a[CHUNK - 1:CHUNK, :]) for a in acs]

    items = [(c, h) for c in chunks for h in range(GDN_HEADS)]

    def head_cols(base, h):
        return slice(base + h * GDN_HEAD_DIM, base + (h + 1) * GDN_HEAD_DIM)

    q_all = [qkv[rows[c], head_cols(0, h)] for c, h in items]
    k_all = [qkv[rows[c], head_cols(GROUP_WIDTH, h)] for c, h in items]
    v_all = [qkv[rows[c], head_cols(2 * GROUP_WIDTH, h)] for c, h in items]
    q_all = [x * lax.rsqrt(jnp.sum(x * x, axis=-1, keepdims=True) + EPS) * GDN_HEAD_DIM ** -0.5 for x in q_all]
    k_all = [x * lax.rsqrt(jnp.sum(x * x, axis=-1, keepdims=True) + EPS) for x in k_all]
    beta = [beta_all[rows[c], SMALL_B + h:SMALL_B + h + 1] for c, h in items]
    decay = [masked_decay(acs[c], acs_t[c], SMALL_A + h) for c, h in items]
    e_g = [e_acs[c][:, SMALL_A + h:SMALL_A + h + 1] for c, h in items]
    kb = [x * b for x, b in zip(k_all, beta)]
    m_low = [jnp.where(strict, _dot_nt(x, y) * dc, 0.0) for x, y, dc in zip(kb, k_all, decay)]
    qk = [_dot_nt(x, y) * dc for x, y, dc in zip(q_all, k_all, decay)]
    t_inv = _unit_lower_inverse(m_low, row, col)
    sol = [_dot3(_split_bf16(t), _split_bf16(jnp.concatenate([x * b, y * e], axis=1)))
           for t, x, b, y, e in zip(t_inv, v_all, beta, kb, e_g)]
    q_dec = [x * e for x, e in zip(q_all, e_g)]
    k_end = [x * e_end[c][:, SMALL_A + h:SMALL_A + h + 1] for x, (c, h) in zip(k_all, items)]

    x_in, y_diag, s_inc, c_mat, e_acs_x, e_blk_x = [], [], [], [], [], []
    for c in chunks:
        r = rows[c]
        x_in.append(xbc[r, :GROUP_WIDTH])
        x_dt = x_in[c] * _expand_heads(sp[r], SMALL_DT, SSM_HEADS, SSM_HEAD_DIM)
        x_end = x_dt * _expand_heads(e_end[c], SMALL_DT, SSM_HEADS, SSM_HEAD_DIM)
        e_acs_x.append(_expand_heads(e_acs[c], SMALL_DT, SSM_HEADS, SSM_HEAD_DIM))
        e_blk_x.append(_expand_heads(e_blk[c], SMALL_DT, SSM_HEADS, SSM_HEAD_DIM))
        for g in range(SSM_GROUPS):
            b_g = xbc[r, GROUP_WIDTH + g * SSM_STATE:GROUP_WIDTH + (g + 1) * SSM_STATE]
            c_g = xbc[r, GROUP_WIDTH + (SSM_GROUPS + g) * SSM_STATE:GROUP_WIDTH + (SSM_GROUPS + g + 1) * SSM_STATE]
            cb = _dot_nt(c_g, b_g)
            x_g = x_dt[:, g * gw:(g + 1) * gw]
            y_g = None
            for j in range(hg):
                l_m = masked_decay(acs[c], acs_t[c], SMALL_DT + g * hg + j)
                part = _dot(cb * l_m, jnp.where(lane_head == j, x_g, 0.0))
                y_g = part if y_g is None else y_g + part
            y_diag.append(y_g)
            s_inc.append(_dot_tn(b_g, x_end[:, g * gw:(g + 1) * gw]))
            c_mat.append(c_g)

    s_state = [s_ref[h] for h in range(GDN_HEADS)]
    h_state = [h_ref[g] for g in range(SSM_GROUPS)]
    for c in chunks:
        r = rows[c]
        idx = [c * GDN_HEADS + h for h in range(GDN_HEADS)]
        v_new = [sol[n][:, :GDN_HEAD_DIM] - _dot(sol[n][:, GDN_HEAD_DIM:], s_state[h])
                 for h, n in enumerate(idx)]
        o = [_dot(q_dec[n], s_state[h]) + _dot(qk[n], v_new[h]) for h, n in enumerate(idx)]
        s_state = [s_state[h] * e_blk[c][:, SMALL_A + h:SMALL_A + h + 1] + _dot_tn(k_end[n], v_new[h])
                   for h, n in enumerate(idx)]
        for h in range(GDN_HEADS):
            z = cur(OFF_CZ + h * GDN_HEAD_DIM, GDN_HEAD_DIM, r)
            y = _rmsnorm(o[h], gdn_norm_ref[...]) * _silu(z)
            out_ref[r, head_cols(2 * GROUP_WIDTH, h)] = y.astype(out_ref.dtype)

        ys = []
        for g in range(SSM_GROUPS):
            n = c * SSM_GROUPS + g
            ys.append(y_diag[n] + _dot(c_mat[n], h_state[g]) * e_acs_x[c][:, g * gw:(g + 1) * gw])
            h_state[g] = h_state[g] * e_blk_x[c][:, g * gw:(g + 1) * gw] + s_inc[n]
        y = jnp.concatenate(ys, axis=1) + ssm_d_ref[...] * x_in[c]
        y = y * _silu(cur(OFF_DZ, GROUP_WIDTH, r))
        for g in range(SSM_GROUPS):
            yn = _rmsnorm(y[:, g * gw:(g + 1) * gw], ssm_norm_ref[:, g * gw:(g + 1) * gw])
            out_ref[r, 3 * GROUP_WIDTH + g * gw:3 * GROUP_WIDTH + (g + 1) * gw] = yn.astype(out_ref.dtype)
    for h in range(GDN_HEADS):
        s_ref[h] = s_state[h]
    for g in range(SSM_GROUPS):
        h_ref[g] = h_state[g]


def _mixers(x, g_pre, w_in_bf16, batch, seq, conv_a_w, pool_w, pool_scale, gdn_conv_w, gdn_norm_g, ssm_conv_w,
            ssm_conv_b, ssm_d_x, ssm_norm_g, alog_row, dtbias_row, tile):
    n_tiles = seq // tile
    d = x.shape[1]

    def const(shape):
        return pl.BlockSpec(shape, lambda b, i: (0,) * len(shape))

    return pl.pallas_call(
        functools.partial(_mixer_kernel, tile=tile),
        grid=(batch, n_tiles),
        in_specs=[
            pl.BlockSpec((tile, d), lambda b, i: (b * n_tiles + i, 0)),
            const((1, d)),
            pl.BlockSpec((d, U_COLS), lambda b, i: (0, 0), pipeline_mode=pl.Buffered(1)),
            const(conv_a_w.shape), const(pool_w.shape), const(pool_scale.shape), const(gdn_conv_w.shape),
            const(gdn_norm_g.shape), const(ssm_conv_w.shape), const(ssm_conv_b.shape), const(ssm_d_x.shape),
            const(ssm_norm_g.shape), const(alog_row.shape), const(dtbias_row.shape),
        ],
        out_specs=pl.BlockSpec((tile, D_MODEL), lambda b, i: (b * n_tiles + i, 0)),
        out_shape=jax.ShapeDtypeStruct((batch * seq, D_MODEL), BF16),
        scratch_shapes=[
            pltpu.VMEM((HALO + tile, U_COLS), F32),
            pltpu.VMEM((GDN_HEADS, GDN_HEAD_DIM, GDN_HEAD_DIM), F32),
            pltpu.VMEM((SSM_GROUPS, SSM_STATE, (SSM_HEADS // SSM_GROUPS) * SSM_HEAD_DIM), F32),
        ],
        compiler_params=pltpu.CompilerParams(
            dimension_semantics=("arbitrary", "arbitrary"), vmem_limit_bytes=VMEM_LIMIT_LARGE),
        name="mixers",
    )(x, g_pre.reshape(1, d), w_in_bf16, conv_a_w, pool_w, pool_scale, gdn_conv_w, gdn_norm_g, ssm_conv_w,
      ssm_conv_b, ssm_d_x, ssm_norm_g, alog_row, dtbias_row)


def _xattn_kernel(x_ref, g_ref, wq_ref, kv_ref, o_ref, wb_ref):
    @pl.when((pl.program_id(0) == 0) & (pl.program_id(1) == 0))
    def _():
        wb_ref[...] = wq_ref[...].astype(BF16)

    xn = _rmsnorm(x_ref[...], g_ref[...]).astype(BF16)
    q = jnp.dot(xn, wb_ref[...], preferred_element_type=F32)
    for h in range(XA_HEADS):
        lo = h * XA_HEAD_DIM
        k_h = kv_ref[:, lo:lo + XA_HEAD_DIM]
        v_h = kv_ref[:, D_MODEL + lo:D_MODEL + lo + XA_HEAD_DIM]
        s = _dot_nt(q[:, lo:lo + XA_HEAD_DIM], k_h) * XA_HEAD_DIM ** -0.5
        e = jnp.exp(s - jnp.max(s, axis=-1, keepdims=True))
        p = e / jnp.sum(e, axis=-1, keepdims=True)
        o_ref[:, lo:lo + XA_HEAD_DIM] = _dot(p, v_h).astype(o_ref.dtype)


def _xattn(x, g, wq, layer, kv, batch, seq, tq):
    n, d = x.shape
    n_tiles = seq // tq
    return pl.pallas_call(
        _xattn_kernel,
        grid=(batch, n_tiles),
        in_specs=[
            pl.BlockSpec((tq, d), lambda b, i: (b * n_tiles + i, 0)),
            pl.BlockSpec((1, d), lambda b, i: (0, 0)),
            pl.BlockSpec((None, d, d), lambda b, i: (layer, 0, 0), pipeline_mode=pl.Buffered(1)),
            pl.BlockSpec((MEM_LEN, 2 * d), lambda b, i: (b, 0)),
        ],
        out_specs=pl.BlockSpec((tq, d), lambda b, i: (b * n_tiles + i, 0)),
        out_shape=jax.ShapeDtypeStruct((n, d), BF16),
        scratch_shapes=[pltpu.VMEM((d, d), BF16)],
        compiler_params=pltpu.CompilerParams(
            dimension_semantics=("arbitrary", "arbitrary"), vmem_limit_bytes=VMEM_LIMIT),
        name="xattn",
    )(x, g.reshape(1, d), wq, kv)


def _ffn_kernel(x_ref, gpre_ref, wg_ref, wu_ref, wd_ref, gpost_ref, o_ref, xn_ref):
    j = pl.program_id(1)

    @pl.when(j == 0)
    def _():
        xn_ref[...] = _rmsnorm(x_ref[...], gpre_ref[...]).astype(BF16)
        o_ref[...] = jnp.zeros_like(o_ref)

    xn = xn_ref[...]
    gate = jnp.dot(xn, wg_ref[...].astype(BF16), preferred_element_type=F32)
    up = jnp.dot(xn, wu_ref[...].astype(BF16), preferred_element_type=F32)
    o_ref[...] += jnp.dot((_silu(gate) * up).astype(BF16), wd_ref[...].astype(BF16), preferred_element_type=F32)

    @pl.when(j == pl.num_programs(1) - 1)
    def _():
        o_ref[...] = x_ref[...] + _rmsnorm(o_ref[...], gpost_ref[...])


def _ffn(x, gpre, w_gu, w_down, layer, gpost, tm, tf):
    n, d = x.shape
    n_ff = D_FF // tf
    return pl.pallas_call(
        _ffn_kernel,
        grid=(n // tm, n_ff),
        in_specs=[
            pl.BlockSpec((tm, d), lambda i, j: (i, 0), pipeline_mode=pl.Buffered(1)),
            pl.BlockSpec((1, d), lambda i, j: (0, 0)),
            pl.BlockSpec((None, d, tf), lambda i, j: (layer, 0, j)),
            pl.BlockSpec((None, d, tf), lambda i, j: (layer, 0, j + n_ff)),
            pl.BlockSpec((None, tf, d), lambda i, j: (layer, j, 0)),
            pl.BlockSpec((1, d), lambda i, j: (0, 0)),
        ],
        out_specs=pl.BlockSpec((tm, d), lambda i, j: (i, 0)),
        out_shape=jax.ShapeDtypeStruct((n, d), F32),
        scratch_shapes=[pltpu.VMEM((tm, d), BF16)],
        compiler_params=pltpu.CompilerParams(
            dimension_semantics=("arbitrary", "arbitrary"), vmem_limit_bytes=VMEM_LIMIT_LARGE),
        name="ffn",
    )(x, gpre.reshape(1, d), w_gu, w_gu, w_down, gpost.reshape(1, d))


def _small_row(gdn_vals, ssm_vals):
    row = jnp.zeros((1, 128), F32)
    row = row.at[0, SMALL_A:SMALL_A + GDN_HEADS].set(gdn_vals.astype(F32))
    return row.at[0, SMALL_DT:SMALL_DT + SSM_HEADS].set(ssm_vals.astype(F32))


def kernel(x, mem, norm_g, w_in, conv_a_w, pool_w, pool_scale, gdn_conv_w, gdn_A_log, gdn_dt_bias, gdn_norm_g,
           ssm_conv_w, ssm_conv_b, ssm_A_log, ssm_dt_bias, ssm_D, ssm_norm_g, w_out, xa_wq, xa_wkv, xa_wo,
           ffn_w_gu, ffn_w_down):
    batch, seq, d = x.shape
    depth = w_in.shape[0]
    xf = x.reshape(batch * seq, d)
    memf = mem.reshape(batch * MEM_LEN, d)
    for l in range(depth):
        g = norm_g[l]
        mixed = _mixers(
            xf, g[0], _prep_w_in(w_in, l), batch, seq, conv_a_w[l], pool_w[l], pool_scale[l].reshape(1, -1),
            gdn_conv_w[l], gdn_norm_g[l].reshape(1, -1), ssm_conv_w[l], ssm_conv_b[l].reshape(1, -1),
            jnp.repeat(ssm_D[l], SSM_HEAD_DIM).reshape(1, -1), ssm_norm_g[l].reshape(1, -1),
            _small_row(gdn_A_log[l], ssm_A_log[l]), _small_row(gdn_dt_bias[l], ssm_dt_bias[l]),
            tile=4 * CHUNK)
        xf = _proj_post(mixed, w_out, l, g[1], xf, tm=512)
        kv = _norm_matmul(memf, g[4], xa_wkv, l, 2 * d, BF16, tm=batch * MEM_LEN, tn=512)
        att = _xattn(xf, g[2], xa_wq, l, kv, batch, seq, tq=512)
        xf = _proj_post(att, xa_wo, l, g[3], xf, tm=512)
        xf = _ffn(xf, g[5], ffn_w_gu, ffn_w_down, l, g[6], tm=1024, tf=256)
    return xf.reshape(batch, seq, d)
```

```python
import functools

import jax
import jax.numpy as jnp
from jax import lax
from jax.experimental import pallas as pl
from jax.experimental.pallas import tpu as pltpu

F32 = jnp.float32
BF16 = jnp.bfloat16
EPS = 1e-6

D_MODEL = 2048
GROUP_WIDTH = 512
CHUNK = 64
POOL_WINDOWS = (2, 4, 8, 16)
POOL_GROUP = 128
GDN_HEADS = 4
GDN_HEAD_DIM = 128
SSM_HEADS = 8
SSM_HEAD_DIM = 64
SSM_GROUPS = 2
SSM_STATE = 128
SSM_XBC = 1024
MEM_LEN = 256
XA_HEADS = 4
XA_HEAD_DIM = 512
D_FF = 5632

IN_COLS = 5648
HEAD_COLS = 4096
TAIL_START = 4096
TAIL_VALID = IN_COLS - TAIL_START
TAIL_COLS = 1664
TAIL_BLOCK = 2048
TAIL_SKEW = 8
U_COLS = HEAD_COLS + TAIL_COLS
U_COL_CHUNK = 512
OFF_A = 0
OFF_B = 1536
OFF_QKV = 2048
OFF_CZ = 3584
OFF_DZ = 4096
OFF_XBC = 4608
OFF_SMALL = 5632
SMALL_DT = 0
SMALL_A = 120
SMALL_B = 124
HALO = 16

VMEM_LIMIT = 48 * 1024 * 1024
VMEM_LIMIT_LARGE = 56 * 1024 * 1024


def _rmsnorm(x, g):
    return x * lax.rsqrt(jnp.mean(x * x, axis=-1, keepdims=True) + EPS) * g


def _silu(x):
    return x * jax.nn.sigmoid(x)


def _softplus(x):
    return jnp.maximum(x, 0.0) + jnp.log1p(jnp.exp(-jnp.abs(x)))


def _dot(a, b):
    return jnp.dot(a.astype(BF16), b.astype(BF16), preferred_element_type=F32)


def _dot_nt(a, b):
    return lax.dot_general(a.astype(BF16), b.astype(BF16), (((1,), (1,)), ((), ())),
                           preferred_element_type=F32)


def _dot_tn(a, b):
    return lax.dot_general(a.astype(BF16), b.astype(BF16), (((0,), (0,)), ((), ())),
                           preferred_element_type=F32)


def _hdot(a, b):
    return jnp.dot(a, b, preferred_element_type=F32, precision=lax.Precision.HIGHEST)


def _split_bf16(a):
    hi = a.astype(BF16)
    return hi, (a - hi.astype(F32)).astype(BF16)


def _dot3(a, b):
    return (jnp.dot(a[0], b[0], preferred_element_type=F32)
            + jnp.dot(a[0], b[1], preferred_element_type=F32)
            + jnp.dot(a[1], b[0], preferred_element_type=F32))


def _norm_matmul_kernel(x_ref, g_ref, w_ref, o_ref, xn_ref):
    @pl.when(pl.program_id(1) == 0)
    def _():
        xn_ref[...] = _rmsnorm(x_ref[...], g_ref[...]).astype(BF16)

    o_ref[...] = jnp.dot(xn_ref[...], w_ref[...].astype(BF16), preferred_element_type=F32).astype(o_ref.dtype)


def _norm_matmul(x, g, w, layer, n_cols, out_dtype, tm, tn):
    n, k = x.shape
    return pl.pallas_call(
        _norm_matmul_kernel,
        grid=(n // tm, n_cols // tn),
        in_specs=[
            pl.BlockSpec((tm, k), lambda i, j: (i, 0)),
            pl.BlockSpec((1, k), lambda i, j: (0, 0)),
            pl.BlockSpec((None, k, tn), lambda i, j: (layer, 0, j)),
        ],
        out_specs=pl.BlockSpec((tm, tn), lambda i, j: (i, j)),
        out_shape=jax.ShapeDtypeStruct((n, n_cols), out_dtype),
        scratch_shapes=[pltpu.VMEM((tm, k), BF16)],
        compiler_params=pltpu.CompilerParams(
            dimension_semantics=("arbitrary", "arbitrary"), vmem_limit_bytes=VMEM_LIMIT_LARGE),
        name="norm_matmul",
    )(x, g.reshape(1, k), w)


def _prep_w_in_kernel(w_ref, o_ref):
    j = pl.program_id(0)

    @pl.when(j < HEAD_COLS // TAIL_BLOCK)
    def _():
        o_ref[...] = w_ref[...].astype(BF16)

    @pl.when(j == HEAD_COLS // TAIL_BLOCK)
    def _():
        rows = 256
        lane = lax.broadcasted_iota(jnp.int32, (rows, TAIL_COLS), 1)
        stale = (lane >= TAIL_VALID - TAIL_SKEW) & (lane < TAIL_COLS - TAIL_SKEW)
        for r in range(0, o_ref.shape[0], rows):
            w = pltpu.roll(w_ref[r:r + rows, :TAIL_COLS], TAIL_COLS - TAIL_SKEW, axis=1)
            o_ref[r:r + rows, :TAIL_COLS] = jnp.where(stale, 0.0, w).astype(BF16)
        o_ref[:, TAIL_COLS:] = jnp.zeros((o_ref.shape[0], TAIL_BLOCK - TAIL_COLS), BF16)


def _prep_w_in(w, layer):
    k = w.shape[1]
    return pl.pallas_call(
        _prep_w_in_kernel,
        grid=(HEAD_COLS // TAIL_BLOCK + 1,),
        in_specs=[pl.BlockSpec((None, k, TAIL_BLOCK), lambda j: (layer, 0, j))],
        out_specs=pl.BlockSpec((k, TAIL_BLOCK), lambda j: (0, j)),
        out_shape=jax.ShapeDtypeStruct((k, U_COLS), BF16),
        compiler_params=pltpu.CompilerParams(
            dimension_semantics=("arbitrary",), vmem_limit_bytes=VMEM_LIMIT_LARGE),
        name="prep_w_in",
    )(w)


def _proj_post_kernel(a_ref, w_ref, g_ref, x_ref, o_ref, wb_ref):
    @pl.when(pl.program_id(0) == 0)
    def _():
        wb_ref[...] = w_ref[...].astype(BF16)

    y = jnp.dot(a_ref[...], wb_ref[...], preferred_element_type=F32)
    o_ref[...] = x_ref[...] + _rmsnorm(y, g_ref[...])


def _proj_post(a, w, layer, g, x, tm):
    n, k = a.shape
    d = w.shape[2]
    return pl.pallas_call(
        _proj_post_kernel,
        grid=(n // tm,),
        in_specs=[
            pl.BlockSpec((tm, k), lambda i: (i, 0)),
            pl.BlockSpec((None, k, d), lambda i: (layer, 0, 0), pipeline_mode=pl.Buffered(1)),
            pl.BlockSpec((1, d), lambda i: (0, 0)),
            pl.BlockSpec((tm, d), lambda i: (i, 0)),
        ],
        out_specs=pl.BlockSpec((tm, d), lambda i: (i, 0)),
        out_shape=jax.ShapeDtypeStruct((n, d), F32),
        scratch_shapes=[pltpu.VMEM((k, d), BF16)],
        compiler_params=pltpu.CompilerParams(
            dimension_semantics=("arbitrary",), vmem_limit_bytes=VMEM_LIMIT_LARGE),
        name="proj_post",
    )(a, w, g.reshape(1, d), x)


def _shift_rows(x, k):
    return pltpu.roll(x, k, axis=0) if k else x


def _causal_conv(x_ext, w_ref):
    width = w_ref.shape[0]
    acc = x_ext * w_ref[width - 1:width, :]
    for k in range(width - 1):
        acc = acc + _shift_rows(x_ext, width - 1 - k) * w_ref[k:k + 1, :]
    return acc


def _expand_heads(cols, first, n_heads, width):
    rows = cols.shape[0]
    per_vreg = 128 // width
    lane = lax.broadcasted_iota(jnp.int32, (rows, 128), 1)
    pieces = []
    for m in range(n_heads // per_vreg):
        piece = jnp.broadcast_to(cols[:, first + m * per_vreg:first + m * per_vreg + 1], (rows, 128))
        for j in range(1, per_vreg):
            c = first + m * per_vreg + j
            piece = jnp.where(lane >= j * width, jnp.broadcast_to(cols[:, c:c + 1], (rows, 128)), piece)
        pieces.append(piece)
    return jnp.concatenate(pieces, axis=1)


def _unit_lower_inverse(ms, row, col, tick):
    same16 = (row >> 4) == (col >> 4)
    same32 = (row >> 5) == (col >> 5)
    off16 = same32 & jnp.logical_not(same16)
    eye = (row == col).astype(F32)
    split = lambda xs: [_split_bf16(x) for x in xs]
    d = [jnp.where(same16, m, 0.0) for m in ms]
    ds = split(d)
    d2 = split([_dot3(x, x) for x in ds])
    p = [eye - x for x in d]
    p = [x + _dot3(xs, y) for x, xs, y in zip(p, split(p), d2)]
    tick()
    d4 = split([_dot3(x, x) for x in d2])
    p = [x + _dot3(xs, y) for x, xs, y in zip(p, split(p), d4)]
    tick()
    d8 = split([_dot3(x, x) for x in d4])
    p = [x + _dot3(xs, y) for x, xs, y in zip(p, split(p), d8)]
    tick()
    ps = split(p)
    t = split([_dot3(xs, _split_bf16(jnp.where(off16, m, 0.0))) for xs, m in zip(ps, ms)])
    q = [x - _dot3(y, xs) for x, xs, y in zip(p, ps, t)]
    tick()
    qs = split(q)
    t = split([_dot3(xs, _split_bf16(jnp.where(same32, 0.0, m))) for xs, m in zip(qs, ms)])
    return [x - _dot3(y, xs) for x, xs, y in zip(q, qs, t)]


def _projection_steps(x_ref, gpre_ref, w_ref, u_ref):
    xn = _rmsnorm(x_ref[...], gpre_ref[...]).astype(BF16)

    def chunk(c0):
        c1 = min(c0 + U_COL_CHUNK, U_COLS)
        u_ref[HALO:, c0:c1] = jnp.dot(xn, w_ref[:, c0:c1], preferred_element_type=F32)

    return [functools.partial(chunk, c0) for c0 in range(0, U_COLS, U_COL_CHUNK)]


def _mixer_kernel(x_first_ref, x_next_ref, gpre_ref, w_ref, conv_a_ref, pool_w_ref, pool_scale_ref, gdn_conv_ref,
                  gdn_norm_ref, ssm_conv_ref, ssm_bias_ref, ssm_d_ref, ssm_norm_ref,
                  alog_ref, dtbias_ref, out_ref, u_even_ref, u_odd_ref, s_ref, h_ref, *, tile):
    i = pl.program_id(1)
    n_tiles = pl.num_programs(1)
    step = pl.program_id(0) * n_tiles + i
    params = (conv_a_ref, pool_w_ref, pool_scale_ref, gdn_conv_ref, gdn_norm_ref, ssm_conv_ref, ssm_bias_ref,
              ssm_d_ref, ssm_norm_ref, alog_ref, dtbias_ref)

    @pl.when(step == 0)
    def _():
        u_even_ref[0:HALO, :] = jnp.zeros((HALO, U_COLS), F32)
        for thunk in _projection_steps(x_first_ref, gpre_ref, w_ref, u_even_ref):
            thunk()

    def run(u_cur_ref, u_next_ref):
        carry = jnp.where(i == n_tiles - 1, 0.0, 1.0)
        u_next_ref[0:HALO, :] = u_cur_ref[tile:tile + HALO, :] * carry
        pending = _projection_steps(x_next_ref, gpre_ref, w_ref, u_next_ref)

        def tick():
            if pending:
                pending.pop(0)()

        _mix_tile(u_cur_ref, *params, out_ref, s_ref, h_ref, tile=tile, tick=tick)
        while pending:
            tick()

    @pl.when(step % 2 == 0)
    def _():
        run(u_even_ref, u_odd_ref)

    @pl.when(step % 2 == 1)
    def _():
        run(u_odd_ref, u_even_ref)


def _mix_tile(u_ref, conv_a_ref, pool_w_ref, pool_scale_ref, gdn_conv_ref, gdn_norm_ref, ssm_conv_ref,
              ssm_bias_ref, ssm_d_ref, ssm_norm_ref, alog_ref, dtbias_ref, out_ref, s_ref, h_ref, *, tile, tick):
    i = pl.program_id(1)
    n_chunks = tile // CHUNK
    hg = SSM_HEADS // SSM_GROUPS
    gw = hg * SSM_HEAD_DIM

    @pl.when(i == 0)
    def _():
        s_ref[...] = jnp.zeros_like(s_ref)
        h_ref[...] = jnp.zeros_like(h_ref)

    def cur(off, width, rows=slice(0, tile)):
        return u_ref[HALO + rows.start:HALO + rows.stop, off:off + width]

    def ext(off, width):
        return u_ref[:, off:off + width]

    p = ext(OFF_A + GROUP_WIDTH, GROUP_WIDTH) * ext(OFF_A + 2 * GROUP_WIDTH, GROUP_WIDTH)
    ya = cur(OFF_A, GROUP_WIDTH) * _causal_conv(p, conv_a_ref)[HALO:]
    out_ref[:, 0:GROUP_WIDTH] = ya.astype(out_ref.dtype)
    tick()

    pos = (i * tile + 1 + lax.broadcasted_iota(jnp.int32, (tile, 1), 0)).astype(F32)
    for gi, window in enumerate(POOL_WINDOWS):
        off = OFF_B + gi * POOL_GROUP
        s = ext(off, POOL_GROUP)
        k = 1
        while k < window:
            s = s + _shift_rows(s, k)
            k *= 2
        pooled = s[HALO:] / jnp.minimum(pos, float(window)) - cur(off, POOL_GROUP)
        yb = _dot(pooled, pool_w_ref[gi]) * pool_scale_ref[:, gi * POOL_GROUP:(gi + 1) * POOL_GROUP]
        out_ref[:, GROUP_WIDTH + gi * POOL_GROUP:GROUP_WIDTH + (gi + 1) * POOL_GROUP] = yb.astype(out_ref.dtype)

    tick()

    small = cur(OFF_SMALL, 128)
    sp = _softplus(small + dtbias_ref[...])
    rate = -jnp.exp(alog_ref[...]) * sp
    beta_all = jax.nn.sigmoid(small)

    qkv = _silu(_causal_conv(ext(OFF_QKV, 3 * GROUP_WIDTH), gdn_conv_ref)[HALO:])
    tick()
    xbc = _silu(_causal_conv(ext(OFF_XBC, SSM_XBC), ssm_conv_ref)[HALO:] + ssm_bias_ref[...])
    tick()

    row = lax.broadcasted_iota(jnp.int32, (CHUNK, CHUNK), 0)
    col = lax.broadcasted_iota(jnp.int32, (CHUNK, CHUNK), 1)
    incl = row >= col
    strict = row > col
    ltri = incl.astype(F32)
    lane_head = lax.broadcasted_iota(jnp.int32, (CHUNK, gw), 1) // SSM_HEAD_DIM

    def masked_decay(cum, cum_t, ch):
        diff = cum[:, ch:ch + 1] - cum_t[ch:ch + 1, :]
        return jnp.where(incl, jnp.exp(jnp.where(incl, diff, 0.0)), 0.0)

    chunks = range(n_chunks)
    rows = [slice(c * CHUNK, (c + 1) * CHUNK) for c in chunks]
    acs = [_hdot(ltri, rate[r]) for r in rows]
    acs_t = [a.T for a in acs]
    e_acs = [jnp.exp(a) for a in acs]
    e_end = [jnp.exp(a[CHUNK - 1:CHUNK, :] - a) for a in acs]
    e_blk = [jnp.exp(a[CHUNK - 1:CHUNK, :]) for a in acs]

    items = [(c, h) for c in chunks for h in range(GDN_HEADS)]

    def head_cols(base, h):
        return slice(base + h * GDN_HEAD_DIM, base + (h + 1) * GDN_HEAD_DIM)

    q_all = [qkv[rows[c], head_cols(0, h)] for c, h in items]
    k_all = [qkv[rows[c], head_cols(GROUP_WIDTH, h)] for c, h in items]
    v_all = [qkv[rows[c], head_cols(2 * GROUP_WIDTH, h)] for c, h in items]
    q_all = [x * lax.rsqrt(jnp.sum(x * x, axis=-1, keepdims=True) + EPS) * GDN_HEAD_DIM ** -0.5 for x in q_all]
    k_all = [x * lax.rsqrt(jnp.sum(x * x, axis=-1, keepdims=True) + EPS) for x in k_all]
    beta = [beta_all[rows[c], SMALL_B + h:SMALL_B + h + 1] for c, h in items]
    decay = [masked_decay(acs[c], acs_t[c], SMALL_A + h) for c, h in items]
    e_g = [e_acs[c][:, SMALL_A + h:SMALL_A + h + 1] for c, h in items]
    kb = [x * b for x, b in zip(k_all, beta)]
    m_low = [jnp.where(strict, _dot_nt(x, y) * dc, 0.0) for x, y, dc in zip(kb, k_all, decay)]
    qk = [_dot_nt(x, y) * dc for x, y, dc in zip(q_all, k_all, decay)]
    tick()
    t_inv = _unit_lower_inverse(m_low, row, col, tick)
    sol = [_dot3(_split_bf16(t), _split_bf16(jnp.concatenate([x * b, y * e], axis=1)))
           for t, x, b, y, e in zip(t_inv, v_all, beta, kb, e_g)]
    tick()
    q_dec = [x * e for x, e in zip(q_all, e_g)]
    k_end = [x * e_end[c][:, SMALL_A + h:SMALL_A + h + 1] for x, (c, h) in zip(k_all, items)]

    x_in, y_diag, s_inc, c_mat, e_acs_x, e_blk_x = [], [], [], [], [], []
    for c in chunks:
        r = rows[c]
        x_in.append(xbc[r, :GROUP_WIDTH])
        x_dt = x_in[c] * _expand_heads(sp[r], SMALL_DT, SSM_HEADS, SSM_HEAD_DIM)
        x_end = x_dt * _expand_heads(e_end[c], SMALL_DT, SSM_HEADS, SSM_HEAD_DIM)
        e_acs_x.append(_expand_heads(e_acs[c], SMALL_DT, SSM_HEADS, SSM_HEAD_DIM))
        e_blk_x.append(_expand_heads(e_blk[c], SMALL_DT, SSM_HEADS, SSM_HEAD_DIM))
        for g in range(SSM_GROUPS):
            b_g = xbc[r, GROUP_WIDTH + g * SSM_STATE:GROUP_WIDTH + (g + 1) * SSM_STATE]
            c_g = xbc[r, GROUP_WIDTH + (SSM_GROUPS + g) * SSM_STATE:GROUP_WIDTH + (SSM_GROUPS + g + 1) * SSM_STATE]
            cb = _dot_nt(c_g, b_g)
            x_g = x_dt[:, g * gw:(g + 1) * gw]
            y_g = None
            for j in range(hg):
                l_m = masked_decay(acs[c], acs_t[c], SMALL_DT + g * hg + j)
                part = _dot(cb * l_m, jnp.where(lane_head == j, x_g, 0.0))
                y_g = part if y_g is None else y_g + part
            y_diag.append(y_g)
            s_inc.append(_dot_tn(b_g, x_end[:, g * gw:(g + 1) * gw]))
            c_mat.append(c_g)

    s_state = [s_ref[h] for h in range(GDN_HEADS)]
    h_state = [h_ref[g] for g in range(SSM_GROUPS)]
    for c in chunks:
        tick()
        r = rows[c]
        idx = [c * GDN_HEADS + h for h in range(GDN_HEADS)]
        v_new = [sol[n][:, :GDN_HEAD_DIM] - _dot(sol[n][:, GDN_HEAD_DIM:], s_state[h])
                 for h, n in enumerate(idx)]
        o = [_dot(q_dec[n], s_state[h]) + _dot(qk[n], v_new[h]) for h, n in enumerate(idx)]
        s_state = [s_state[h] * e_blk[c][:, SMALL_A + h:SMALL_A + h + 1] + _dot_tn(k_end[n], v_new[h])
                   for h, n in enumerate(idx)]
        for h in range(GDN_HEADS):
            z = cur(OFF_CZ + h * GDN_HEAD_DIM, GDN_HEAD_DIM, r)
            y = _rmsnorm(o[h], gdn_norm_ref[...]) * _silu(z)
            out_ref[r, head_cols(2 * GROUP_WIDTH, h)] = y.astype(out_ref.dtype)

        ys = []
        for g in range(SSM_GROUPS):
            n = c * SSM_GROUPS + g
            ys.append(y_diag[n] + _dot(c_mat[n], h_state[g]) * e_acs_x[c][:, g * gw:(g + 1) * gw])
            h_state[g] = h_state[g] * e_blk_x[c][:, g * gw:(g + 1) * gw] + s_inc[n]
        y = jnp.concatenate(ys, axis=1) + ssm_d_ref[...] * x_in[c]
        y = y * _silu(cur(OFF_DZ, GROUP_WIDTH, r))
        for g in range(SSM_GROUPS):
            yn = _rmsnorm(y[:, g * gw:(g + 1) * gw], ssm_norm_ref[:, g * gw:(g + 1) * gw])
            out_ref[r, 3 * GROUP_WIDTH + g * gw:3 * GROUP_WIDTH + (g + 1) * gw] = yn.astype(out_ref.dtype)
    for h in range(GDN_HEADS):
        s_ref[h] = s_state[h]
    for g in range(SSM_GROUPS):
        h_ref[g] = h_state[g]


def _mixers(x, g_pre, w_in_bf16, batch, seq, conv_a_w, pool_w, pool_scale, gdn_conv_w, gdn_norm_g, ssm_conv_w,
            ssm_conv_b, ssm_d_x, ssm_norm_g, alog_row, dtbias_row, tile):
    n_tiles = seq // tile
    d = x.shape[1]

    def const(shape):
        return pl.BlockSpec(shape, lambda b, i: (0,) * len(shape))

    return pl.pallas_call(
        functools.partial(_mixer_kernel, tile=tile),
        grid=(batch, n_tiles),
        in_specs=[
            pl.BlockSpec((tile, d), lambda b, i: (0, 0)),
            pl.BlockSpec((tile, d), lambda b, i: (jnp.minimum(b * n_tiles + i + 1, batch * n_tiles - 1), 0)),
            const((1, d)),
            pl.BlockSpec((d, U_COLS), lambda b, i: (0, 0), pipeline_mode=pl.Buffered(1)),
            const(conv_a_w.shape), const(pool_w.shape), const(pool_scale.shape), const(gdn_conv_w.shape),
            const(gdn_norm_g.shape), const(ssm_conv_w.shape), const(ssm_conv_b.shape), const(ssm_d_x.shape),
            const(ssm_norm_g.shape), const(alog_row.shape), const(dtbias_row.shape),
        ],
        out_specs=pl.BlockSpec((tile, D_MODEL), lambda b, i: (b * n_tiles + i, 0)),
        out_shape=jax.ShapeDtypeStruct((batch * seq, D_MODEL), BF16),
        scratch_shapes=[
            pltpu.VMEM((HALO + tile, U_COLS), F32),
            pltpu.VMEM((HALO + tile, U_COLS), F32),
            pltpu.VMEM((GDN_HEADS, GDN_HEAD_DIM, GDN_HEAD_DIM), F32),
            pltpu.VMEM((SSM_GROUPS, SSM_STATE, (SSM_HEADS // SSM_GROUPS) * SSM_HEAD_DIM), F32),
        ],
        compiler_params=pltpu.CompilerParams(
            dimension_semantics=("arbitrary", "arbitrary"), vmem_limit_bytes=VMEM_LIMIT_LARGE),
        name="mixers",
    )(x, x, g_pre.reshape(1, d), w_in_bf16, conv_a_w, pool_w, pool_scale, gdn_conv_w, gdn_norm_g, ssm_conv_w,
      ssm_conv_b, ssm_d_x, ssm_norm_g, alog_row, dtbias_row)


def _xattn_kernel(x_ref, g_ref, wq_ref, kv_ref, o_ref, wb_ref):
    @pl.when((pl.program_id(0) == 0) & (pl.program_id(1) == 0))
    def _():
        wb_ref[...] = wq_ref[...].astype(BF16)

    xn = _rmsnorm(x_ref[...], g_ref[...]).astype(BF16)
    q = jnp.dot(xn, wb_ref[...], preferred_element_type=F32)
    for h in range(XA_HEADS):
        lo = h * XA_HEAD_DIM
        k_h = kv_ref[:, lo:lo + XA_HEAD_DIM]
        v_h = kv_ref[:, D_MODEL + lo:D_MODEL + lo + XA_HEAD_DIM]
        s = _dot_nt(q[:, lo:lo + XA_HEAD_DIM], k_h) * XA_HEAD_DIM ** -0.5
        e = jnp.exp(s - jnp.max(s, axis=-1, keepdims=True))
        p = e / jnp.sum(e, axis=-1, keepdims=True)
        o_ref[:, lo:lo + XA_HEAD_DIM] = _dot(p, v_h).astype(o_ref.dtype)


def _xattn(x, g, wq, layer, kv, batch, seq, tq):
    n, d = x.shape
    n_tiles = seq // tq
    return pl.pallas_call(
        _xattn_kernel,
        grid=(batch, n_tiles),
        in_specs=[
            pl.BlockSpec((tq, d), lambda b, i: (b * n_tiles + i, 0)),
            pl.BlockSpec((1, d), lambda b, i: (0, 0)),
            pl.BlockSpec((None, d, d), lambda b, i: (layer, 0, 0), pipeline_mode=pl.Buffered(1)),
            pl.BlockSpec((MEM_LEN, 2 * d), lambda b, i: (b, 0)),
        ],
        out_specs=pl.BlockSpec((tq, d), lambda b, i: (b * n_tiles + i, 0)),
        out_shape=jax.ShapeDtypeStruct((n, d), BF16),
        scratch_shapes=[pltpu.VMEM((d, d), BF16)],
        compiler_params=pltpu.CompilerParams(
            dimension_semantics=("arbitrary", "arbitrary"), vmem_limit_bytes=VMEM_LIMIT),
        name="xattn",
    )(x, g.reshape(1, d), wq, kv)


def _ffn_kernel(x_ref, gpre_ref, wg_ref, wu_ref, wd_ref, gpost_ref, o_ref, xn_ref):
    j = pl.program_id(1)

    @pl.when(j == 0)
    def _():
        xn_ref[...] = _rmsnorm(x_ref[...], gpre_ref[...]).astype(BF16)
        o_ref[...] = jnp.zeros_like(o_ref)

    xn = xn_ref[...]
    gate = jnp.dot(xn, wg_ref[...].astype(BF16), preferred_element_type=F32)
    up = jnp.dot(xn, wu_ref[...].astype(BF16), preferred_element_type=F32)
    o_ref[...] += jnp.dot((_silu(gate) * up).astype(BF16), wd_ref[...].astype(BF16), preferred_element_type=F32)

    @pl.when(j == pl.num_programs(1) - 1)
    def _():
        o_ref[...] = x_ref[...] + _rmsnorm(o_ref[...], gpost_ref[...])


def _ffn(x, gpre, w_gu, w_down, layer, gpost, tm, tf):
    n, d = x.shape
    n_ff = D_FF // tf
    return pl.pallas_call(
        _ffn_kernel,
        grid=(n // tm, n_ff),
        in_specs=[
            pl.BlockSpec((tm, d), lambda i, j: (i, 0), pipeline_mode=pl.Buffered(1)),
            pl.BlockSpec((1, d), lambda i, j: (0, 0)),
            pl.BlockSpec((None, d, tf), lambda i, j: (layer, 0, j)),
            pl.BlockSpec((None, d, tf), lambda i, j: (layer, 0, j + n_ff)),
            pl.BlockSpec((None, tf, d), lambda i, j: (layer, j, 0)),
            pl.BlockSpec((1, d), lambda i, j: (0, 0)),
        ],
        out_specs=pl.BlockSpec((tm, d), lambda i, j: (i, 0)),
        out_shape=jax.ShapeDtypeStruct((n, d), F32),
        scratch_shapes=[pltpu.VMEM((tm, d), BF16)],
        compiler_params=pltpu.CompilerParams(
            dimension_semantics=("arbitrary", "arbitrary"), vmem_limit_bytes=VMEM_LIMIT_LARGE),
        name="ffn",
    )(x, gpre.reshape(1, d), w_gu, w_gu, w_down, gpost.reshape(1, d))


def _small_row(gdn_vals, ssm_vals):
    row = jnp.zeros((1, 128), F32)
    row = row.at[0, SMALL_A:SMALL_A + GDN_HEADS].set(gdn_vals.astype(F32))
    return row.at[0, SMALL_DT:SMALL_DT + SSM_HEADS].set(ssm_vals.astype(F32))


def kernel(x, mem, norm_g, w_in, conv_a_w, pool_w, pool_scale, gdn_conv_w, gdn_A_log, gdn_dt_bias, gdn_norm_g,
           ssm_conv_w, ssm_conv_b, ssm_A_log, ssm_dt_bias, ssm_D, ssm_norm_g, w_out, xa_wq, xa_wkv, xa_wo,
           ffn_w_gu, ffn_w_down):
    batch, seq, d = x.shape
    depth = w_in.shape[0]
    xf = x.reshape(batch * seq, d)
    memf = mem.reshape(batch * MEM_LEN, d)
    for l in range(depth):
        g = norm_g[l]
        mixed = _mixers(
            xf, g[0], _prep_w_in(w_in, l), batch, seq, conv_a_w[l], pool_w[l], pool_scale[l].reshape(1, -1),
            gdn_conv_w[l], gdn_norm_g[l].reshape(1, -1), ssm_conv_w[l], ssm_conv_b[l].reshape(1, -1),
            jnp.repeat(ssm_D[l], SSM_HEAD_DIM).reshape(1, -1), ssm_norm_g[l].reshape(1, -1),
            _small_row(gdn_A_log[l], ssm_A_log[l]), _small_row(gdn_dt_bias[l], ssm_dt_bias[l]),
            tile=4 * CHUNK)
        xf = _proj_post(mixed, w_out, l, g[1], xf, tm=512)
        kv = _norm_matmul(memf, g[4], xa_wkv, l, 2 * d, BF16, tm=batch * MEM_LEN, tn=512)
        att = _xattn(xf, g[2], xa_wq, l, kv, batch, seq, tq=512)
        xf = _proj_post(att, xa_wo, l, g[3], xf, tm=512)
        xf = _ffn(xf, g[5], ffn_w_gu, ffn_w_down, l, g[6], tm=1024, tf=256)
    return xf.reshape(batch, seq, d)
```

```python
import functools

import jax
import jax.numpy as jnp
from jax import lax
from jax.experimental import pallas as pl
from jax.experimental.pallas import tpu as pltpu

F32 = jnp.float32
BF16 = jnp.bfloat16
EPS = 1e-6

D_MODEL = 2048
GROUP_WIDTH = 512
CHUNK = 64
POOL_WINDOWS = (2, 4, 8, 16)
POOL_GROUP = 128
GDN_HEADS = 4
GDN_HEAD_DIM = 128
SSM_HEADS = 8
SSM_HEAD_DIM = 64
SSM_GROUPS = 2
SSM_STATE = 128
SSM_XBC = 1024
MEM_LEN = 256
XA_HEADS = 4
XA_HEAD_DIM = 512
D_FF = 5632

IN_COLS = 5648
HEAD_COLS = 4096
TAIL_START = 4096
TAIL_VALID = IN_COLS - TAIL_START
TAIL_COLS = 1664
TAIL_BLOCK = 2048
TAIL_SKEW = 8
U_COLS = HEAD_COLS + TAIL_COLS
U_COL_CHUNK = 1152
OFF_A = 0
OFF_B = 1536
OFF_QKV = 2048
OFF_CZ = 3584
OFF_DZ = 4096
OFF_XBC = 4608
OFF_SMALL = 5632
SMALL_DT = 0
SMALL_A = 120
SMALL_B = 124
HALO = 16

VMEM_LIMIT = 48 * 1024 * 1024
VMEM_LIMIT_LARGE = 56 * 1024 * 1024
VMEM_LIMIT_FFN = 60000 * 1024


def _rmsnorm(x, g):
    return x * lax.rsqrt(jnp.mean(x * x, axis=-1, keepdims=True) + EPS) * g


def _silu(x):
    return x * jax.nn.sigmoid(x)


def _softplus(x):
    return jnp.maximum(x, 0.0) + jnp.log1p(jnp.exp(-jnp.abs(x)))


def _dot(a, b):
    return jnp.dot(a.astype(BF16), b.astype(BF16), preferred_element_type=F32)


def _dot_nt(a, b):
    return lax.dot_general(a.astype(BF16), b.astype(BF16), (((1,), (1,)), ((), ())),
                           preferred_element_type=F32)


def _dot_tn(a, b):
    return lax.dot_general(a.astype(BF16), b.astype(BF16), (((0,), (0,)), ((), ())),
                           preferred_element_type=F32)


def _split_bf16(a):
    hi = a.astype(BF16)
    return hi, (a - hi.astype(F32)).astype(BF16)


def _dot3(a, b):
    return (jnp.dot(a[0], b[0], preferred_element_type=F32)
            + jnp.dot(a[0], b[1], preferred_element_type=F32)
            + jnp.dot(a[1], b[0], preferred_element_type=F32))


def _norm_matmul_kernel(x_ref, g_ref, w_ref, o_ref, xn_ref):
    @pl.when(pl.program_id(1) == 0)
    def _():
        xn_ref[...] = _rmsnorm(x_ref[...], g_ref[...]).astype(BF16)

    o_ref[...] = jnp.dot(xn_ref[...], w_ref[...].astype(BF16), preferred_element_type=F32).astype(o_ref.dtype)


def _norm_matmul(x, g, w, layer, n_cols, out_dtype, tm, tn):
    n, k = x.shape
    return pl.pallas_call(
        _norm_matmul_kernel,
        grid=(n // tm, n_cols // tn),
        in_specs=[
            pl.BlockSpec((tm, k), lambda i, j: (i, 0)),
            pl.BlockSpec((1, k), lambda i, j: (0, 0)),
            pl.BlockSpec((None, k, tn), lambda i, j: (layer, 0, j)),
        ],
        out_specs=pl.BlockSpec((tm, tn), lambda i, j: (i, j)),
        out_shape=jax.ShapeDtypeStruct((n, n_cols), out_dtype),
        scratch_shapes=[pltpu.VMEM((tm, k), BF16)],
        compiler_params=pltpu.CompilerParams(
            dimension_semantics=("arbitrary", "arbitrary"), vmem_limit_bytes=VMEM_LIMIT_LARGE),
        name="norm_matmul",
    )(x, g.reshape(1, k), w)


def _prep_w_in_kernel(w_ref, o_ref):
    j = pl.program_id(0)

    @pl.when(j < HEAD_COLS // TAIL_BLOCK)
    def _():
        o_ref[...] = w_ref[...].astype(BF16)

    @pl.when(j == HEAD_COLS // TAIL_BLOCK)
    def _():
        rows = 256
        lane = lax.broadcasted_iota(jnp.int32, (rows, TAIL_COLS), 1)
        stale = (lane >= TAIL_VALID - TAIL_SKEW) & (lane < TAIL_COLS - TAIL_SKEW)
        for r in range(0, o_ref.shape[0], rows):
            w = pltpu.roll(w_ref[r:r + rows, :TAIL_COLS], TAIL_COLS - TAIL_SKEW, axis=1)
            o_ref[r:r + rows, :TAIL_COLS] = jnp.where(stale, 0.0, w).astype(BF16)
        o_ref[:, TAIL_COLS:] = jnp.zeros((o_ref.shape[0], TAIL_BLOCK - TAIL_COLS), BF16)


def _prep_w_in(w, layer):
    k = w.shape[1]
    return pl.pallas_call(
        _prep_w_in_kernel,
        grid=(HEAD_COLS // TAIL_BLOCK + 1,),
        in_specs=[pl.BlockSpec((None, k, TAIL_BLOCK), lambda j: (layer, 0, j))],
        out_specs=pl.BlockSpec((k, TAIL_BLOCK), lambda j: (0, j)),
        out_shape=jax.ShapeDtypeStruct((k, U_COLS), BF16),
        compiler_params=pltpu.CompilerParams(
            dimension_semantics=("arbitrary",), vmem_limit_bytes=VMEM_LIMIT_LARGE),
        name="prep_w_in",
    )(w)


def _proj_post_kernel(a_ref, w_ref, g_ref, x_ref, o_ref, wb_ref):
    @pl.when(pl.program_id(0) == 0)
    def _():
        wb_ref[...] = w_ref[...].astype(BF16)

    y = jnp.dot(a_ref[...], wb_ref[...], preferred_element_type=F32)
    o_ref[...] = x_ref[...] + _rmsnorm(y, g_ref[...])


def _proj_post(a, w, layer, g, x, tm):
    n, k = a.shape
    d = w.shape[2]
    return pl.pallas_call(
        _proj_post_kernel,
        grid=(n // tm,),
        in_specs=[
            pl.BlockSpec((tm, k), lambda i: (i, 0)),
            pl.BlockSpec((None, k, d), lambda i: (layer, 0, 0), pipeline_mode=pl.Buffered(1)),
            pl.BlockSpec((1, d), lambda i: (0, 0)),
            pl.BlockSpec((tm, d), lambda i: (i, 0)),
        ],
        out_specs=pl.BlockSpec((tm, d), lambda i: (i, 0)),
        out_shape=jax.ShapeDtypeStruct((n, d), F32),
        scratch_shapes=[pltpu.VMEM((k, d), BF16)],
        compiler_params=pltpu.CompilerParams(
            dimension_semantics=("arbitrary",), vmem_limit_bytes=VMEM_LIMIT_LARGE),
        name="proj_post",
    )(a, w, g.reshape(1, d), x)


def _shift_rows(x, k):
    return pltpu.roll(x, k, axis=0) if k else x


def _causal_conv(x_ext, w_ref):
    width = w_ref.shape[0]
    acc = x_ext * w_ref[width - 1:width, :]
    for k in range(width - 1):
        acc = acc + _shift_rows(x_ext, width - 1 - k) * w_ref[k:k + 1, :]
    return acc


def _expand_heads(cols, first, n_heads, width):
    rows = cols.shape[0]
    per_vreg = 128 // width
    lane = lax.broadcasted_iota(jnp.int32, (rows, 128), 1)
    pieces = []
    for m in range(n_heads // per_vreg):
        piece = jnp.broadcast_to(cols[:, first + m * per_vreg:first + m * per_vreg + 1], (rows, 128))
        for j in range(1, per_vreg):
            c = first + m * per_vreg + j
            piece = jnp.where(lane >= j * width, jnp.broadcast_to(cols[:, c:c + 1], (rows, 128)), piece)
        pieces.append(piece)
    return jnp.concatenate(pieces, axis=1)


def _block_diag(y, lane_block, n_blocks):
    return jnp.concatenate([jnp.where(lane_block == b, y, jnp.zeros_like(y)) for b in range(n_blocks)], axis=0)


def _split_block_diag(y, lane_block, n_blocks):
    hi, lo = _split_bf16(y)
    return _block_diag(hi, lane_block, n_blocks), _block_diag(lo, lane_block, n_blocks)


def _unit_lower_inverse(ms, row, col, lane_block, n_blocks):
    same16 = (row >> 4) == (col >> 4)
    same32 = (row >> 5) == (col >> 5)
    off16 = same32 & jnp.logical_not(same16)
    eye = (row == col).astype(F32)
    split = lambda xs: [_split_bf16(x) for x in xs]
    split_bd = lambda xs: [_split_block_diag(x, lane_block, n_blocks) for x in xs]
    d = [jnp.where(same16, m, 0.0) for m in ms]
    d2 = [_dot3(x, y) for x, y in zip(split(d), split_bd(d))]
    p = [eye - x for x in d]
    d2_bd = split_bd(d2)
    p = [x + _dot3(xs, y) for x, xs, y in zip(p, split(p), d2_bd)]
    d4 = [_dot3(x, y) for x, y in zip(split(d2), d2_bd)]
    d4_bd = split_bd(d4)
    p = [x + _dot3(xs, y) for x, xs, y in zip(p, split(p), d4_bd)]
    d8_bd = split_bd([_dot3(x, y) for x, y in zip(split(d4), d4_bd)])
    p = [x + _dot3(xs, y) for x, xs, y in zip(p, split(p), d8_bd)]
    ps, p_bd = split(p), split_bd(p)
    t = split([_dot3(xs, _split_block_diag(jnp.where(off16, m, 0.0), lane_block, n_blocks)) for xs, m in zip(ps, ms)])
    q = [x - _dot3(y, xb) for x, xb, y in zip(p, p_bd, t)]
    qs, q_bd = split(q), split_bd(q)
    t = split([_dot3(xs, _split_block_diag(jnp.where(same32, 0.0, m), lane_block, n_blocks)) for xs, m in zip(qs, ms)])
    return [x - _dot3(y, xb) for x, xb, y in zip(q, q_bd, t)]


def _dot_ltri(ltri, x):
    x1 = x.astype(BF16)
    r1 = x - x1.astype(F32)
    x2 = r1.astype(BF16)
    x3 = (r1 - x2.astype(F32)).astype(BF16)
    return (jnp.dot(ltri, x1, preferred_element_type=F32) + jnp.dot(ltri, x2, preferred_element_type=F32)
            + jnp.dot(ltri, x3, preferred_element_type=F32))


def _mixer_kernel(x_ref, gpre_ref, w_ref, conv_a_ref, pool_w_ref, pool_scale_ref, gdn_conv_ref,
                  gdn_norm_ref, ssm_conv_ref, ssm_bias_ref, ssm_d_ref, ssm_norm_ref,
                  alog_ref, dtbias_ref, out_ref, u_ref, s_ref, h_ref, *, tile):
    i = pl.program_id(1)
    n_chunks = tile // CHUNK
    hg = SSM_HEADS // SSM_GROUPS
    gw = hg * SSM_HEAD_DIM

    @pl.when(i == 0)
    def _():
        s_ref[...] = jnp.zeros_like(s_ref)
        h_ref[...] = jnp.zeros_like(h_ref)
        u_ref[0:HALO, :] = jnp.zeros((HALO, U_COLS), F32)

    @pl.when(i > 0)
    def _():
        u_ref[0:HALO, :] = u_ref[tile:tile + HALO, :]

    xn = _rmsnorm(x_ref[...], gpre_ref[...]).astype(BF16)
    for c0 in range(0, U_COLS, U_COL_CHUNK):
        u_ref[HALO:, c0:c0 + U_COL_CHUNK] = jnp.dot(xn, w_ref[:, c0:c0 + U_COL_CHUNK],
                                                      preferred_element_type=F32)

    def cur(off, width, rows=slice(0, tile)):
        return u_ref[HALO + rows.start:HALO + rows.stop, off:off + width]

    def ext(off, width):
        return u_ref[:, off:off + width]

    p = ext(OFF_A + GROUP_WIDTH, GROUP_WIDTH) * ext(OFF_A + 2 * GROUP_WIDTH, GROUP_WIDTH)
    ya = cur(OFF_A, GROUP_WIDTH) * _causal_conv(p, conv_a_ref)[HALO:]
    out_ref[:, 0:GROUP_WIDTH] = ya.astype(out_ref.dtype)

    pos = (i * tile + 1 + lax.broadcasted_iota(jnp.int32, (tile, 1), 0)).astype(F32)
    for gi, window in enumerate(POOL_WINDOWS):
        off = OFF_B + gi * POOL_GROUP
        s = ext(off, POOL_GROUP)
        k = 1
        while k < window:
            s = s + _shift_rows(s, k)
            k *= 2
        pooled = s[HALO:] / jnp.minimum(pos, float(window)) - cur(off, POOL_GROUP)
        yb = _dot(pooled, pool_w_ref[gi]) * pool_scale_ref[:, gi * POOL_GROUP:(gi + 1) * POOL_GROUP]
        out_ref[:, GROUP_WIDTH + gi * POOL_GROUP:GROUP_WIDTH + (gi + 1) * POOL_GROUP] = yb.astype(out_ref.dtype)

    small = cur(OFF_SMALL, 128)
    sp = _softplus(small + dtbias_ref[...])
    rate = -jnp.exp(alog_ref[...]) * sp
    beta_all = jax.nn.sigmoid(small)

    qkv = _silu(_causal_conv(ext(OFF_QKV, 3 * GROUP_WIDTH), gdn_conv_ref)[HALO:])
    xbc = _silu(_causal_conv(ext(OFF_XBC, SSM_XBC), ssm_conv_ref)[HALO:] + ssm_bias_ref[...])

    n4 = GDN_HEADS * CHUNK
    row = lax.broadcasted_iota(jnp.int32, (CHUNK, n4), 0)
    lane = lax.broadcasted_iota(jnp.int32, (CHUNK, n4), 1)
    col = lane & (CHUNK - 1)
    lane_block = lane >> 6
    incl = row >= col
    strict = row > col
    ltri = (lax.broadcasted_iota(jnp.int32, (CHUNK, CHUNK), 0)
            >= lax.broadcasted_iota(jnp.int32, (CHUNK, CHUNK), 1)).astype(BF16)

    def decay_blocks(rate_c, first, n_heads):
        out = []
        for m in range(n_heads // GDN_HEADS):
            r = _expand_heads(rate_c, first + m * GDN_HEADS, GDN_HEADS, CHUNK)
            diff = _dot_ltri(ltri, jnp.where(strict, r, 0.0))
            out.append(jnp.where(incl, jnp.exp(jnp.where(incl, diff, 0.0)), 0.0))
        return out

    chunks = range(n_chunks)
    rows = [slice(c * CHUNK, (c + 1) * CHUNK) for c in chunks]
    acs = [_dot_ltri(ltri, rate[r]) for r in rows]
    e_acs = [jnp.exp(a) for a in acs]
    e_end = [jnp.exp(a[CHUNK - 1:CHUNK, :] - a) for a in acs]
    e_blk = [jnp.exp(a[CHUNK - 1:CHUNK, :]) for a in acs]

    def head_cols(base, h):
        return slice(base + h * GDN_HEAD_DIM, base + (h + 1) * GDN_HEAD_DIM)

    def l2norm_heads(x, scale):
        return jnp.concatenate(
            [x[:, head_cols(0, h)] * (lax.rsqrt(jnp.sum(x[:, head_cols(0, h)] * x[:, head_cols(0, h)],
                                                        axis=-1, keepdims=True) + EPS) * scale)
             for h in range(GDN_HEADS)], axis=1)

    zeros_head = jnp.zeros((CHUNK, GDN_HEAD_DIM), BF16)
    q_all = [l2norm_heads(qkv[r, 0:GROUP_WIDTH], GDN_HEAD_DIM ** -0.5) for r in rows]
    k_all = [l2norm_heads(qkv[r, GROUP_WIDTH:2 * GROUP_WIDTH], 1.0) for r in rows]
    v_all = [qkv[r, 2 * GROUP_WIDTH:3 * GROUP_WIDTH] for r in rows]
    beta_x = [_expand_heads(beta_all[r], SMALL_B, GDN_HEADS, GDN_HEAD_DIM) for r in rows]
    e_g_x = [_expand_heads(e, SMALL_A, GDN_HEADS, GDN_HEAD_DIM) for e in e_acs]
    kb = [x * b for x, b in zip(k_all, beta_x)]
    k_rows = []
    for x in k_all:
        xb = x.astype(BF16)
        k_rows.append(jnp.concatenate(
            [jnp.concatenate([xb[:, head_cols(0, h)] if h == g else zeros_head for h in range(GDN_HEADS)], axis=1)
             for g in range(GDN_HEADS)], axis=0))
    decay = [decay_blocks(rate[r], SMALL_A, GDN_HEADS)[0] for r in rows]
    m_low = [jnp.where(strict, _dot_nt(x, y) * dc, 0.0) for x, y, dc in zip(kb, k_rows, decay)]
    qk = [_dot_nt(x, y) * dc for x, y, dc in zip(q_all, k_rows, decay)]
    t_inv = _unit_lower_inverse(m_low, row, col, lane_block, GDN_HEADS)
    zeros_rhs = jnp.zeros((CHUNK, 2 * GDN_HEAD_DIM), F32)
    sol = []
    for c in chunks:
        vb, kg = v_all[c] * beta_x[c], kb[c] * e_g_x[c]
        rhs = jnp.concatenate(
            [jnp.concatenate([jnp.concatenate([vb[:, head_cols(0, h)], kg[:, head_cols(0, h)]], axis=1)
                              if h == g else zeros_rhs for h in range(GDN_HEADS)], axis=1)
             for g in range(GDN_HEADS)], axis=0)
        sol.append(_dot3(_split_bf16(t_inv[c]), _split_bf16(rhs)))
    q_dec = [x * e for x, e in zip(q_all, e_g_x)]
    k_end = [x * _expand_heads(e, SMALL_A, GDN_HEADS, GDN_HEAD_DIM) for x, e in zip(k_all, e_end)]

    x_in, y_diag, s_inc, c_mat, e_acs_x, e_blk_x = [], [], [], [], [], []
    for c in chunks:
        r = rows[c]
        x_in.append(xbc[r, :GROUP_WIDTH])
        x_dt = x_in[c] * _expand_heads(sp[r], SMALL_DT, SSM_HEADS, SSM_HEAD_DIM)
        x_end = x_dt * _expand_heads(e_end[c], SMALL_DT, SSM_HEADS, SSM_HEAD_DIM)
        e_acs_x.append(_expand_heads(e_acs[c], SMALL_DT, SSM_HEADS, SSM_HEAD_DIM))
        e_blk_x.append(_expand_heads(e_blk[c], SMALL_DT, SSM_HEADS, SSM_HEAD_DIM))
        l_m = decay_blocks(rate[r], SMALL_DT, SSM_HEADS)
        for g in range(SSM_GROUPS):
            b_g = xbc[r, GROUP_WIDTH + g * SSM_STATE:GROUP_WIDTH + (g + 1) * SSM_STATE]
            c_g = xbc[r, GROUP_WIDTH + (SSM_GROUPS + g) * SSM_STATE:GROUP_WIDTH + (SSM_GROUPS + g + 1) * SSM_STATE]
            b_bf = b_g.astype(BF16)
            cb = _dot_nt(c_g, jnp.concatenate([b_bf] * hg, axis=0))
            x_g = x_dt[:, g * gw:(g + 1) * gw]
            y_diag.append(_dot(cb * l_m[g], _block_diag(x_g.astype(BF16), lane_block, hg)))
            s_inc.append(_dot_tn(b_bf, x_end[:, g * gw:(g + 1) * gw]))
            c_mat.append(c_g)

    s_state = [s_ref[h] for h in range(GDN_HEADS)]
    h_state = [h_ref[g] for g in range(SSM_GROUPS)]
    zeros_v = jnp.zeros((CHUNK, GDN_HEAD_DIM), BF16)
    for c in chunks:
        r = rows[c]
        hd = 2 * GDN_HEAD_DIM
        v_new = [sol[c][:, h * hd:h * hd + GDN_HEAD_DIM]
                 - _dot(sol[c][:, h * hd + GDN_HEAD_DIM:(h + 1) * hd], s_state[h]) for h in range(GDN_HEADS)]
        v_rows = jnp.concatenate(
            [jnp.concatenate([v_new[h].astype(BF16) if h == g else zeros_v for h in range(GDN_HEADS)], axis=1)
             for g in range(GDN_HEADS)], axis=0)
        o_intra = _dot(qk[c], v_rows)
        o = [_dot(q_dec[c][:, head_cols(0, h)], s_state[h]) + o_intra[:, head_cols(0, h)]
             for h in range(GDN_HEADS)]
        s_state = [s_state[h] * e_blk[c][:, SMALL_A + h:SMALL_A + h + 1]
                   + _dot_tn(k_end[c][:, head_cols(0, h)], v_new[h]) for h in range(GDN_HEADS)]
        for h in range(GDN_HEADS):
            z = cur(OFF_CZ + h * GDN_HEAD_DIM, GDN_HEAD_DIM, r)
            y = _rmsnorm(o[h], gdn_norm_ref[...]) * _silu(z)
            out_ref[r, head_cols(2 * GROUP_WIDTH, h)] = y.astype(out_ref.dtype)

        ys = []
        for g in range(SSM_GROUPS):
            n = c * SSM_GROUPS + g
            ys.append(y_diag[n] + _dot(c_mat[n], h_state[g]) * e_acs_x[c][:, g * gw:(g + 1) * gw])
            h_state[g] = h_state[g] * e_blk_x[c][:, g * gw:(g + 1) * gw] + s_inc[n]
        y = jnp.concatenate(ys, axis=1) + ssm_d_ref[...] * x_in[c]
        y = y * _silu(cur(OFF_DZ, GROUP_WIDTH, r))
        for g in range(SSM_GROUPS):
            yn = _rmsnorm(y[:, g * gw:(g + 1) * gw], ssm_norm_ref[:, g * gw:(g + 1) * gw])
            out_ref[r, 3 * GROUP_WIDTH + g * gw:3 * GROUP_WIDTH + (g + 1) * gw] = yn.astype(out_ref.dtype)
    for h in range(GDN_HEADS):
        s_ref[h] = s_state[h]
    for g in range(SSM_GROUPS):
        h_ref[g] = h_state[g]


def _mixers(x, g_pre, w_in_bf16, batch, seq, conv_a_w, pool_w, pool_scale, gdn_conv_w, gdn_norm_g, ssm_conv_w,
            ssm_conv_b, ssm_d_x, ssm_norm_g, alog_row, dtbias_row, tile):
    n_tiles = seq // tile
    d = x.shape[1]

    def const(shape):
        return pl.BlockSpec(shape, lambda b, i: (0,) * len(shape))

    return pl.pallas_call(
        functools.partial(_mixer_kernel, tile=tile),
        grid=(batch, n_tiles),
        in_specs=[
            pl.BlockSpec((tile, d), lambda b, i: (b * n_tiles + i, 0)),
            const((1, d)),
            pl.BlockSpec((d, U_COLS), lambda b, i: (0, 0), pipeline_mode=pl.Buffered(1)),
            const(conv_a_w.shape), const(pool_w.shape), const(pool_scale.shape), const(gdn_conv_w.shape),
            const(gdn_norm_g.shape), const(ssm_conv_w.shape), const(ssm_conv_b.shape), const(ssm_d_x.shape),
            const(ssm_norm_g.shape), const(alog_row.shape), const(dtbias_row.shape),
        ],
        out_specs=pl.BlockSpec((tile, D_MODEL), lambda b, i: (b * n_tiles + i, 0)),
        out_shape=jax.ShapeDtypeStruct((batch * seq, D_MODEL), BF16),
        scratch_shapes=[
            pltpu.VMEM((HALO + tile, U_COLS), F32),
            pltpu.VMEM((GDN_HEADS, GDN_HEAD_DIM, GDN_HEAD_DIM), F32),
            pltpu.VMEM((SSM_GROUPS, SSM_STATE, (SSM_HEADS // SSM_GROUPS) * SSM_HEAD_DIM), F32),
        ],
        compiler_params=pltpu.CompilerParams(
            dimension_semantics=("arbitrary", "arbitrary"), vmem_limit_bytes=VMEM_LIMIT_LARGE),
        name="mixers",
    )(x, g_pre.reshape(1, d), w_in_bf16, conv_a_w, pool_w, pool_scale, gdn_conv_w, gdn_norm_g, ssm_conv_w,
      ssm_conv_b, ssm_d_x, ssm_norm_g, alog_row, dtbias_row)


def _xattn_kernel(x_ref, g_ref, wq_ref, kv_ref, o_ref, wb_ref):
    @pl.when((pl.program_id(0) == 0) & (pl.program_id(1) == 0))
    def _():
        wb_ref[...] = wq_ref[...].astype(BF16)

    xn = _rmsnorm(x_ref[...], g_ref[...]).astype(BF16)
    q = jnp.dot(xn, wb_ref[...], preferred_element_type=F32)
    for h in range(XA_HEADS):
        lo = h * XA_HEAD_DIM
        k_h = kv_ref[:, lo:lo + XA_HEAD_DIM]
        v_h = kv_ref[:, D_MODEL + lo:D_MODEL + lo + XA_HEAD_DIM]
        s = _dot_nt(q[:, lo:lo + XA_HEAD_DIM], k_h) * XA_HEAD_DIM ** -0.5
        e = jnp.exp(s - jnp.max(s, axis=-1, keepdims=True))
        p = e / jnp.sum(e, axis=-1, keepdims=True)
        o_ref[:, lo:lo + XA_HEAD_DIM] = _dot(p, v_h).astype(o_ref.dtype)


def _xattn(x, g, wq, layer, kv, batch, seq, tq):
    n, d = x.shape
    n_tiles = seq // tq
    return pl.pallas_call(
        _xattn_kernel,
        grid=(batch, n_tiles),
        in_specs=[
            pl.BlockSpec((tq, d), lambda b, i: (b * n_tiles + i, 0)),
            pl.BlockSpec((1, d), lambda b, i: (0, 0)),
            pl.BlockSpec((None, d, d), lambda b, i: (layer, 0, 0), pipeline_mode=pl.Buffered(1)),
            pl.BlockSpec((MEM_LEN, 2 * d), lambda b, i: (b, 0)),
        ],
        out_specs=pl.BlockSpec((tq, d), lambda b, i: (b * n_tiles + i, 0)),
        out_shape=jax.ShapeDtypeStruct((n, d), BF16),
        scratch_shapes=[pltpu.VMEM((d, d), BF16)],
        compiler_params=pltpu.CompilerParams(
            dimension_semantics=("arbitrary", "arbitrary"), vmem_limit_bytes=VMEM_LIMIT),
        name="xattn",
    )(x, g.reshape(1, d), wq, kv)


def _ffn_kernel(x_ref, gpre_ref, wg_ref, wu_ref, wd_ref, gpost_ref, o_ref, xn_ref):
    j = pl.program_id(1)

    @pl.when(j == 0)
    def _():
        xn_ref[...] = _rmsnorm(x_ref[...], gpre_ref[...]).astype(BF16)
        o_ref[...] = jnp.zeros_like(o_ref)

    xn = xn_ref[...]
    gate = jnp.dot(xn, wg_ref[...].astype(BF16), preferred_element_type=F32)
    up = jnp.dot(xn, wu_ref[...].astype(BF16), preferred_element_type=F32)
    o_ref[...] += jnp.dot((_silu(gate) * up).astype(BF16), wd_ref[...].astype(BF16), preferred_element_type=F32)

    @pl.when(j == pl.num_programs(1) - 1)
    def _():
        o_ref[...] = x_ref[...] + _rmsnorm(o_ref[...], gpost_ref[...])


def _ffn(x, gpre, w_gu, w_down, layer, gpost, tm, tf, out_buffers):
    n, d = x.shape
    n_ff = D_FF // tf
    return pl.pallas_call(
        _ffn_kernel,
        grid=(n // tm, n_ff),
        in_specs=[
            pl.BlockSpec((tm, d), lambda i, j: (i, 0), pipeline_mode=pl.Buffered(1)),
            pl.BlockSpec((1, d), lambda i, j: (0, 0)),
            pl.BlockSpec((None, d, tf), lambda i, j: (layer, 0, j)),
            pl.BlockSpec((None, d, tf), lambda i, j: (layer, 0, j + n_ff)),
            pl.BlockSpec((None, tf, d), lambda i, j: (layer, j, 0)),
            pl.BlockSpec((1, d), lambda i, j: (0, 0)),
        ],
        out_specs=pl.BlockSpec((tm, d), lambda i, j: (i, 0), pipeline_mode=pl.Buffered(out_buffers)),
        out_shape=jax.ShapeDtypeStruct((n, d), F32),
        scratch_shapes=[pltpu.VMEM((tm, d), BF16)],
        compiler_params=pltpu.CompilerParams(
            dimension_semantics=("arbitrary", "arbitrary"), vmem_limit_bytes=VMEM_LIMIT_FFN),
        name="ffn",
    )(x, gpre.reshape(1, d), w_gu, w_gu, w_down, gpost.reshape(1, d))


def _small_row(gdn_vals, ssm_vals):
    row = jnp.zeros((1, 128), F32)
    row = row.at[0, SMALL_A:SMALL_A + GDN_HEADS].set(gdn_vals.astype(F32))
    return row.at[0, SMALL_DT:SMALL_DT + SSM_HEADS].set(ssm_vals.astype(F32))


def kernel(x, mem, norm_g, w_in, conv_a_w, pool_w, pool_scale, gdn_conv_w, gdn_A_log, gdn_dt_bias, gdn_norm_g,
           ssm_conv_w, ssm_conv_b, ssm_A_log, ssm_dt_bias, ssm_D, ssm_norm_g, w_out, xa_wq, xa_wkv, xa_wo,
           ffn_w_gu, ffn_w_down):
    batch, seq, d = x.shape
    depth = w_in.shape[0]
    xf = x.reshape(batch * seq, d)
    memf = mem.reshape(batch * MEM_LEN, d)
    for l in range(depth):
        g = norm_g[l]
        mixed = _mixers(
            xf, g[0], _prep_w_in(w_in, l), batch, seq, conv_a_w[l], pool_w[l], pool_scale[l].reshape(1, -1),
            gdn_conv_w[l], gdn_norm_g[l].reshape(1, -1), ssm_conv_w[l], ssm_conv_b[l].reshape(1, -1),
            jnp.repeat(ssm_D[l], SSM_HEAD_DIM).reshape(1, -1), ssm_norm_g[l].reshape(1, -1),
            _small_row(gdn_A_log[l], ssm_A_log[l]), _small_row(gdn_dt_bias[l], ssm_dt_bias[l]),
            tile=4 * CHUNK)
        xf = _proj_post(mixed, w_out, l, g[1], xf, tm=512)
        kv = _norm_matmul(memf, g[4], xa_wkv, l, 2 * d, BF16, tm=batch * MEM_LEN, tn=512)
        att = _xattn(xf, g[2], xa_wq, l, kv, batch, seq, tq=512)
        xf = _proj_post(att, xa_wo, l, g[3], xf, tm=512)
        if l == 0:
            xf = _ffn(xf, g[5], ffn_w_gu, ffn_w_down, l, g[6], tm=1024, tf=512, out_buffers=1)
        else:
            xf = _ffn(xf, g[5], ffn_w_gu, ffn_w_down, l, g[6], tm=1024, tf=256, out_buffers=2)
    return xf.reshape(batch, seq, d)
```

```python
import functools

import jax
import jax.numpy as jnp
from jax import lax
from jax.experimental import pallas as pl
from jax.experimental.pallas import tpu as pltpu

F32 = jnp.float32
BF16 = jnp.bfloat16
EPS = 1e-6

D_MODEL = 2048
GROUP_WIDTH = 512
CHUNK = 64
POOL_WINDOWS = (2, 4, 8, 16)
POOL_GROUP = 128
GDN_HEADS = 4
GDN_HEAD_DIM = 128
SSM_HEADS = 8
SSM_HEAD_DIM = 64
SSM_GROUPS = 2
SSM_STATE = 128
SSM_XBC = 1024
MEM_LEN = 256
XA_HEADS = 4
XA_HEAD_DIM = 512
D_FF = 5632

IN_COLS = 5648
HEAD_COLS = 4096
TAIL_START = 4096
TAIL_VALID = IN_COLS - TAIL_START
TAIL_COLS = 1664
TAIL_BLOCK = 2048
TAIL_SKEW = 8
U_COLS = HEAD_COLS + TAIL_COLS
U_COL_CHUNK = 1024
OFF_A = 0
OFF_B = 1536
OFF_QKV = 2048
OFF_CZ = 3584
OFF_DZ = 4096
OFF_XBC = 4608
OFF_SMALL = 5632
SMALL_DT = 0
SMALL_A = 120
SMALL_B = 124
HALO = 16

VMEM_LIMIT = 48 * 1024 * 1024
VMEM_LIMIT_LARGE = 56 * 1024 * 1024
VMEM_LIMIT_FFN = 60000 * 1024


def _rmsnorm(x, g):
    return x * lax.rsqrt(jnp.mean(x * x, axis=-1, keepdims=True) + EPS) * g


def _silu(x):
    return x * jax.nn.sigmoid(x)


def _softplus(x):
    return jnp.maximum(x, 0.0) + jnp.log1p(jnp.exp(-jnp.abs(x)))


def _dot(a, b):
    return jnp.dot(a.astype(BF16), b.astype(BF16), preferred_element_type=F32)


def _dot_nt(a, b):
    return lax.dot_general(a.astype(BF16), b.astype(BF16), (((1,), (1,)), ((), ())),
                           preferred_element_type=F32)


def _dot_tn(a, b):
    return lax.dot_general(a.astype(BF16), b.astype(BF16), (((0,), (0,)), ((), ())),
                           preferred_element_type=F32)


def _split_bf16(a):
    hi = a.astype(BF16)
    return hi, (a - hi.astype(F32)).astype(BF16)


def _dot3(a, b):
    return (jnp.dot(a[0], b[0], preferred_element_type=F32)
            + jnp.dot(a[0], b[1], preferred_element_type=F32)
            + jnp.dot(a[1], b[0], preferred_element_type=F32))


def _norm_matmul_kernel(x_ref, g_ref, w_ref, o_ref, xn_ref):
    @pl.when(pl.program_id(1) == 0)
    def _():
        xn_ref[...] = _rmsnorm(x_ref[...], g_ref[...]).astype(BF16)

    o_ref[...] = jnp.dot(xn_ref[...], w_ref[...].astype(BF16), preferred_element_type=F32).astype(o_ref.dtype)


def _norm_matmul(x, g, w, layer, n_cols, out_dtype, tm, tn):
    n, k = x.shape
    return pl.pallas_call(
        _norm_matmul_kernel,
        grid=(n // tm, n_cols // tn),
        in_specs=[
            pl.BlockSpec((tm, k), lambda i, j: (i, 0)),
            pl.BlockSpec((1, k), lambda i, j: (0, 0)),
            pl.BlockSpec((None, k, tn), lambda i, j: (layer, 0, j)),
        ],
        out_specs=pl.BlockSpec((tm, tn), lambda i, j: (i, j)),
        out_shape=jax.ShapeDtypeStruct((n, n_cols), out_dtype),
        scratch_shapes=[pltpu.VMEM((tm, k), BF16)],
        compiler_params=pltpu.CompilerParams(
            dimension_semantics=("arbitrary", "arbitrary"), vmem_limit_bytes=VMEM_LIMIT_LARGE),
        name="norm_matmul",
    )(x, g.reshape(1, k), w)


PREP_ROWS = 512


def _prep_head_kernel(wt_ref, o_ref):
    o_ref[...] = wt_ref[...].T.astype(BF16)


def _prep_tail_kernel(wt_ref, o_ref):
    dt0 = TAIL_VALID - TAIL_SKEW
    body = TAIL_COLS - 128
    for r in range(0, body, PREP_ROWS):
        o_ref[:, r:r + PREP_ROWS] = wt_ref[TAIL_SKEW + r:TAIL_SKEW + r + PREP_ROWS, :].T.astype(BF16)
    last = jnp.concatenate([wt_ref[dt0:TAIL_VALID, :],
                            jnp.zeros((128 - 2 * TAIL_SKEW, wt_ref.shape[1]), F32),
                            wt_ref[0:TAIL_SKEW, :]], axis=0)
    o_ref[:, body:] = last.T.astype(BF16)


def _prep_w_in(wt, layer):
    k = wt.shape[2]
    head = pl.pallas_call(
        _prep_head_kernel,
        grid=(HEAD_COLS // PREP_ROWS,),
        in_specs=[pl.BlockSpec((None, PREP_ROWS, k), lambda j: (layer, j, 0))],
        out_specs=pl.BlockSpec((k, PREP_ROWS), lambda j: (0, j)),
        out_shape=jax.ShapeDtypeStruct((k, HEAD_COLS), BF16),
        compiler_params=pltpu.CompilerParams(dimension_semantics=("arbitrary",), vmem_limit_bytes=VMEM_LIMIT),
        name="prep_w_in_head",
    )(wt)
    tail = pl.pallas_call(
        _prep_tail_kernel,
        grid=(1,),
        in_specs=[pl.BlockSpec((None, TAIL_BLOCK, k), lambda j: (layer, TAIL_START // TAIL_BLOCK, 0),
                               pipeline_mode=pl.Buffered(1))],
        out_specs=pl.BlockSpec((k, TAIL_COLS), lambda j: (0, 0)),
        out_shape=jax.ShapeDtypeStruct((k, TAIL_COLS), BF16),
        compiler_params=pltpu.CompilerParams(dimension_semantics=("arbitrary",), vmem_limit_bytes=VMEM_LIMIT_LARGE),
        name="prep_w_in_tail",
    )(wt)
    return head, tail


def _proj_post_kernel(a_ref, w_ref, g_ref, x_ref, o_ref, wb_ref):
    @pl.when(pl.program_id(0) == 0)
    def _():
        wb_ref[...] = w_ref[...].astype(BF16)

    y = jnp.dot(a_ref[...], wb_ref[...], preferred_element_type=F32)
    o_ref[...] = x_ref[...] + _rmsnorm(y, g_ref[...])


def _proj_post(a, w, layer, g, x, tm):
    n, k = a.shape
    d = w.shape[2]
    return pl.pallas_call(
        _proj_post_kernel,
        grid=(n // tm,),
        in_specs=[
            pl.BlockSpec((tm, k), lambda i: (i, 0)),
            pl.BlockSpec((None, k, d), lambda i: (layer, 0, 0), pipeline_mode=pl.Buffered(1)),
            pl.BlockSpec((1, d), lambda i: (0, 0)),
            pl.BlockSpec((tm, d), lambda i: (i, 0)),
        ],
        out_specs=pl.BlockSpec((tm, d), lambda i: (i, 0)),
        out_shape=jax.ShapeDtypeStruct((n, d), F32),
        scratch_shapes=[pltpu.VMEM((k, d), BF16)],
        compiler_params=pltpu.CompilerParams(
            dimension_semantics=("arbitrary",), vmem_limit_bytes=VMEM_LIMIT_LARGE),
        name="proj_post",
    )(a, w, g.reshape(1, d), x)


def _shift_rows(x, k):
    return pltpu.roll(x, k, axis=0) if k else x


def _causal_conv(x_ext, w_ref):
    width = w_ref.shape[0]
    acc = x_ext * w_ref[width - 1:width, :]
    for k in range(width - 1):
        acc = acc + _shift_rows(x_ext, width - 1 - k) * w_ref[k:k + 1, :]
    return acc


def _expand_heads(cols, first, n_heads, width):
    rows = cols.shape[0]
    per_vreg = 128 // width
    lane = lax.broadcasted_iota(jnp.int32, (rows, 128), 1)
    pieces = []
    for m in range(n_heads // per_vreg):
        piece = jnp.broadcast_to(cols[:, first + m * per_vreg:first + m * per_vreg + 1], (rows, 128))
        for j in range(1, per_vreg):
            c = first + m * per_vreg + j
            piece = jnp.where(lane >= j * width, jnp.broadcast_to(cols[:, c:c + 1], (rows, 128)), piece)
        pieces.append(piece)
    return jnp.concatenate(pieces, axis=1)


def _block_diag(y, lane_block, n_blocks):
    return jnp.concatenate([jnp.where(lane_block == b, y, jnp.zeros_like(y)) for b in range(n_blocks)], axis=0)


def _split_block_diag(y, lane_block, n_blocks):
    hi, lo = _split_bf16(y)
    return _block_diag(hi, lane_block, n_blocks), _block_diag(lo, lane_block, n_blocks)


def _unit_lower_inverse(ms, row, col, lane_block, n_blocks):
    same16 = (row >> 4) == (col >> 4)
    same32 = (row >> 5) == (col >> 5)
    off16 = same32 & jnp.logical_not(same16)
    eye = (row == col).astype(F32)
    split = lambda xs: [_split_bf16(x) for x in xs]
    split_bd = lambda xs: [_split_block_diag(x, lane_block, n_blocks) for x in xs]
    d = [jnp.where(same16, m, 0.0) for m in ms]
    d2 = [_dot3(x, y) for x, y in zip(split(d), split_bd(d))]
    p = [eye - x for x in d]
    d2_bd = split_bd(d2)
    p = [x + _dot3(xs, y) for x, xs, y in zip(p, split(p), d2_bd)]
    d4 = [_dot3(x, y) for x, y in zip(split(d2), d2_bd)]
    d4_bd = split_bd(d4)
    p = [x + _dot3(xs, y) for x, xs, y in zip(p, split(p), d4_bd)]
    d8_bd = split_bd([_dot3(x, y) for x, y in zip(split(d4), d4_bd)])
    p = [x + _dot3(xs, y) for x, xs, y in zip(p, split(p), d8_bd)]
    ps, p_bd = split(p), split_bd(p)
    t = split([_dot3(xs, _split_block_diag(jnp.where(off16, m, 0.0), lane_block, n_blocks)) for xs, m in zip(ps, ms)])
    q = [x - _dot3(y, xb) for x, xb, y in zip(p, p_bd, t)]
    qs, q_bd = split(q), split_bd(q)
    t = split([_dot3(xs, _split_block_diag(jnp.where(same32, 0.0, m), lane_block, n_blocks)) for xs, m in zip(qs, ms)])
    return [x - _dot3(y, xb) for x, xb, y in zip(q, q_bd, t)]


def _dot_ltri(ltri, x):
    x1 = x.astype(BF16)
    r1 = x - x1.astype(F32)
    x2 = r1.astype(BF16)
    x3 = (r1 - x2.astype(F32)).astype(BF16)
    return (jnp.dot(ltri, x1, preferred_element_type=F32) + jnp.dot(ltri, x2, preferred_element_type=F32)
            + jnp.dot(ltri, x3, preferred_element_type=F32))


def _mixer_kernel(x_ref, gpre_ref, w_head_ref, w_tail_ref, conv_a_ref, pool_w_ref, pool_scale_ref, gdn_conv_ref,
                  gdn_norm_ref, ssm_conv_ref, ssm_bias_ref, ssm_d_ref, ssm_norm_ref,
                  alog_ref, dtbias_ref, out_ref, u_ref, s_ref, h_ref, *, tile):
    i = pl.program_id(1)
    n_chunks = tile // CHUNK
    hg = SSM_HEADS // SSM_GROUPS
    gw = hg * SSM_HEAD_DIM

    @pl.when(i == 0)
    def _():
        s_ref[...] = jnp.zeros_like(s_ref)
        h_ref[...] = jnp.zeros_like(h_ref)
        u_ref[0:HALO, :] = jnp.zeros((HALO, U_COLS), F32)

    @pl.when(i > 0)
    def _():
        u_ref[0:HALO, :] = u_ref[tile:tile + HALO, :]

    xn = _rmsnorm(x_ref[...], gpre_ref[...]).astype(BF16)
    for c0 in range(0, HEAD_COLS, U_COL_CHUNK):
        u_ref[HALO:, c0:c0 + U_COL_CHUNK] = jnp.dot(xn, w_head_ref[:, c0:c0 + U_COL_CHUNK],
                                                      preferred_element_type=F32)
    u_ref[HALO:, HEAD_COLS:] = jnp.dot(xn, w_tail_ref[...], preferred_element_type=F32)

    def cur(off, width, rows=slice(0, tile)):
        return u_ref[HALO + rows.start:HALO + rows.stop, off:off + width]

    def ext(off, width):
        return u_ref[:, off:off + width]

    p = ext(OFF_A + GROUP_WIDTH, GROUP_WIDTH) * ext(OFF_A + 2 * GROUP_WIDTH, GROUP_WIDTH)
    ya = cur(OFF_A, GROUP_WIDTH) * _causal_conv(p, conv_a_ref)[HALO:]
    out_ref[:, 0:GROUP_WIDTH] = ya.astype(out_ref.dtype)

    pos = (i * tile + 1 + lax.broadcasted_iota(jnp.int32, (tile, 1), 0)).astype(F32)
    for gi, window in enumerate(POOL_WINDOWS):
        off = OFF_B + gi * POOL_GROUP
        s = ext(off, POOL_GROUP)
        k = 1
        while k < window:
            s = s + _shift_rows(s, k)
            k *= 2
        pooled = s[HALO:] / jnp.minimum(pos, float(window)) - cur(off, POOL_GROUP)
        yb = _dot(pooled, pool_w_ref[gi]) * pool_scale_ref[:, gi * POOL_GROUP:(gi + 1) * POOL_GROUP]
        out_ref[:, GROUP_WIDTH + gi * POOL_GROUP:GROUP_WIDTH + (gi + 1) * POOL_GROUP] = yb.astype(out_ref.dtype)

    small = cur(OFF_SMALL, 128)
    sp = _softplus(small + dtbias_ref[...])
    rate = -jnp.exp(alog_ref[...]) * sp
    beta_all = jax.nn.sigmoid(small)

    qkv = _silu(_causal_conv(ext(OFF_QKV, 3 * GROUP_WIDTH), gdn_conv_ref)[HALO:])
    xbc = _silu(_causal_conv(ext(OFF_XBC, SSM_XBC), ssm_conv_ref)[HALO:] + ssm_bias_ref[...])

    n4 = GDN_HEADS * CHUNK
    row = lax.broadcasted_iota(jnp.int32, (CHUNK, n4), 0)
    lane = lax.broadcasted_iota(jnp.int32, (CHUNK, n4), 1)
    col = lane & (CHUNK - 1)
    lane_block = lane >> 6
    incl = row >= col
    strict = row > col
    ltri = (lax.broadcasted_iota(jnp.int32, (CHUNK, CHUNK), 0)
            >= lax.broadcasted_iota(jnp.int32, (CHUNK, CHUNK), 1)).astype(BF16)

    def decay_blocks(rate_c, first, n_heads):
        out = []
        for m in range(n_heads // GDN_HEADS):
            r = _expand_heads(rate_c, first + m * GDN_HEADS, GDN_HEADS, CHUNK)
            diff = _dot_ltri(ltri, jnp.where(strict, r, 0.0))
            out.append(jnp.where(incl, jnp.exp(jnp.where(incl, diff, 0.0)), 0.0))
        return out

    chunks = range(n_chunks)
    rows = [slice(c * CHUNK, (c + 1) * CHUNK) for c in chunks]
    acs = [_dot_ltri(ltri, rate[r]) for r in rows]
    e_acs = [jnp.exp(a) for a in acs]
    e_end = [jnp.exp(a[CHUNK - 1:CHUNK, :] - a) for a in acs]
    e_blk = [jnp.exp(a[CHUNK - 1:CHUNK, :]) for a in acs]

    def head_cols(base, h):
        return slice(base + h * GDN_HEAD_DIM, base + (h + 1) * GDN_HEAD_DIM)

    def l2norm_heads(x, scale):
        return jnp.concatenate(
            [x[:, head_cols(0, h)] * (lax.rsqrt(jnp.sum(x[:, head_cols(0, h)] * x[:, head_cols(0, h)],
                                                        axis=-1, keepdims=True) + EPS) * scale)
             for h in range(GDN_HEADS)], axis=1)

    zeros_head = jnp.zeros((CHUNK, GDN_HEAD_DIM), BF16)
    q_all = [l2norm_heads(qkv[r, 0:GROUP_WIDTH], GDN_HEAD_DIM ** -0.5) for r in rows]
    k_all = [l2norm_heads(qkv[r, GROUP_WIDTH:2 * GROUP_WIDTH], 1.0) for r in rows]
    v_all = [qkv[r, 2 * GROUP_WIDTH:3 * GROUP_WIDTH] for r in rows]
    beta_x = [_expand_heads(beta_all[r], SMALL_B, GDN_HEADS, GDN_HEAD_DIM) for r in rows]
    e_g_x = [_expand_heads(e, SMALL_A, GDN_HEADS, GDN_HEAD_DIM) for e in e_acs]
    kb = [x * b for x, b in zip(k_all, beta_x)]
    k_rows = []
    for x in k_all:
        xb = x.astype(BF16)
        k_rows.append(jnp.concatenate(
            [jnp.concatenate([xb[:, head_cols(0, h)] if h == g else zeros_head for h in range(GDN_HEADS)], axis=1)
             for g in range(GDN_HEADS)], axis=0))
    decay = [decay_blocks(rate[r], SMALL_A, GDN_HEADS)[0] for r in rows]
    m_low = [jnp.where(strict, _dot_nt(x, y) * dc, 0.0) for x, y, dc in zip(kb, k_rows, decay)]
    qk = [_dot_nt(x, y) * dc for x, y, dc in zip(q_all, k_rows, decay)]
    t_inv = _unit_lower_inverse(m_low, row, col, lane_block, GDN_HEADS)
    zeros_rhs = jnp.zeros((CHUNK, 2 * GDN_HEAD_DIM), F32)
    sol = []
    for c in chunks:
        vb, kg = v_all[c] * beta_x[c], kb[c] * e_g_x[c]
        rhs = jnp.concatenate(
            [jnp.concatenate([jnp.concatenate([vb[:, head_cols(0, h)], kg[:, head_cols(0, h)]], axis=1)
                              if h == g else zeros_rhs for h in range(GDN_HEADS)], axis=1)
             for g in range(GDN_HEADS)], axis=0)
        sol.append(_dot3(_split_bf16(t_inv[c]), _split_bf16(rhs)))
    q_dec = [x * e for x, e in zip(q_all, e_g_x)]
    k_end = [x * _expand_heads(e, SMALL_A, GDN_HEADS, GDN_HEAD_DIM) for x, e in zip(k_all, e_end)]

    x_in, y_diag, s_inc, c_mat, e_acs_x, e_blk_x = [], [], [], [], [], []
    for c in chunks:
        r = rows[c]
        x_in.append(xbc[r, :GROUP_WIDTH])
        x_dt = x_in[c] * _expand_heads(sp[r], SMALL_DT, SSM_HEADS, SSM_HEAD_DIM)
        x_end = x_dt * _expand_heads(e_end[c], SMALL_DT, SSM_HEADS, SSM_HEAD_DIM)
        e_acs_x.append(_expand_heads(e_acs[c], SMALL_DT, SSM_HEADS, SSM_HEAD_DIM))
        e_blk_x.append(_expand_heads(e_blk[c], SMALL_DT, SSM_HEADS, SSM_HEAD_DIM))
        l_m = decay_blocks(rate[r], SMALL_DT, SSM_HEADS)
        for g in range(SSM_GROUPS):
            b_g = xbc[r, GROUP_WIDTH + g * SSM_STATE:GROUP_WIDTH + (g + 1) * SSM_STATE]
            c_g = xbc[r, GROUP_WIDTH + (SSM_GROUPS + g) * SSM_STATE:GROUP_WIDTH + (SSM_GROUPS + g + 1) * SSM_STATE]
            b_bf = b_g.astype(BF16)
            cb = _dot_nt(c_g, jnp.concatenate([b_bf] * hg, axis=0))
            x_g = x_dt[:, g * gw:(g + 1) * gw]
            y_diag.append(_dot(cb * l_m[g], _block_diag(x_g.astype(BF16), lane_block, hg)))
            s_inc.append(_dot_tn(b_bf, x_end[:, g * gw:(g + 1) * gw]))
            c_mat.append(c_g)

    s_state = [s_ref[h] for h in range(GDN_HEADS)]
    h_state = [h_ref[g] for g in range(SSM_GROUPS)]
    zeros_v = jnp.zeros((CHUNK, GDN_HEAD_DIM), BF16)
    for c in chunks:
        r = rows[c]
        hd = 2 * GDN_HEAD_DIM
        v_new = [sol[c][:, h * hd:h * hd + GDN_HEAD_DIM]
                 - _dot(sol[c][:, h * hd + GDN_HEAD_DIM:(h + 1) * hd], s_state[h]) for h in range(GDN_HEADS)]
        v_rows = jnp.concatenate(
            [jnp.concatenate([v_new[h].astype(BF16) if h == g else zeros_v for h in range(GDN_HEADS)], axis=1)
             for g in range(GDN_HEADS)], axis=0)
        o_intra = _dot(qk[c], v_rows)
        o = [_dot(q_dec[c][:, head_cols(0, h)], s_state[h]) + o_intra[:, head_cols(0, h)]
             for h in range(GDN_HEADS)]
        s_state = [s_state[h] * e_blk[c][:, SMALL_A + h:SMALL_A + h + 1]
                   + _dot_tn(k_end[c][:, head_cols(0, h)], v_new[h]) for h in range(GDN_HEADS)]
        for h in range(GDN_HEADS):
            z = cur(OFF_CZ + h * GDN_HEAD_DIM, GDN_HEAD_DIM, r)
            y = _rmsnorm(o[h], gdn_norm_ref[...]) * _silu(z)
            out_ref[r, head_cols(2 * GROUP_WIDTH, h)] = y.astype(out_ref.dtype)

        ys = []
        for g in range(SSM_GROUPS):
            n = c * SSM_GROUPS + g
            ys.append(y_diag[n] + _dot(c_mat[n], h_state[g]) * e_acs_x[c][:, g * gw:(g + 1) * gw])
            h_state[g] = h_state[g] * e_blk_x[c][:, g * gw:(g + 1) * gw] + s_inc[n]
        y = jnp.concatenate(ys, axis=1) + ssm_d_ref[...] * x_in[c]
        y = y * _silu(cur(OFF_DZ, GROUP_WIDTH, r))
        for g in range(SSM_GROUPS):
            yn = _rmsnorm(y[:, g * gw:(g + 1) * gw], ssm_norm_ref[:, g * gw:(g + 1) * gw])
            out_ref[r, 3 * GROUP_WIDTH + g * gw:3 * GROUP_WIDTH + (g + 1) * gw] = yn.astype(out_ref.dtype)
    for h in range(GDN_HEADS):
        s_ref[h] = s_state[h]
    for g in range(SSM_GROUPS):
        h_ref[g] = h_state[g]


def _mixers(x, g_pre, w_head, w_tail, batch, seq, conv_a_w, pool_w, pool_scale, gdn_conv_w, gdn_norm_g, ssm_conv_w,
            ssm_conv_b, ssm_d_x, ssm_norm_g, alog_row, dtbias_row, tile):
    n_tiles = seq // tile
    d = x.shape[1]

    def const(shape):
        return pl.BlockSpec(shape, lambda b, i: (0,) * len(shape))

    return pl.pallas_call(
        functools.partial(_mixer_kernel, tile=tile),
        grid=(batch, n_tiles),
        in_specs=[
            pl.BlockSpec((tile, d), lambda b, i: (b * n_tiles + i, 0)),
            const((1, d)),
            pl.BlockSpec((d, HEAD_COLS), lambda b, i: (0, 0), pipeline_mode=pl.Buffered(1)),
            pl.BlockSpec((d, TAIL_COLS), lambda b, i: (0, 0), pipeline_mode=pl.Buffered(1)),
            const(conv_a_w.shape), const(pool_w.shape), const(pool_scale.shape), const(gdn_conv_w.shape),
            const(gdn_norm_g.shape), const(ssm_conv_w.shape), const(ssm_conv_b.shape), const(ssm_d_x.shape),
            const(ssm_norm_g.shape), const(alog_row.shape), const(dtbias_row.shape),
        ],
        out_specs=pl.BlockSpec((tile, D_MODEL), lambda b, i: (b * n_tiles + i, 0)),
        out_shape=jax.ShapeDtypeStruct((batch * seq, D_MODEL), BF16),
        scratch_shapes=[
            pltpu.VMEM((HALO + tile, U_COLS), F32),
            pltpu.VMEM((GDN_HEADS, GDN_HEAD_DIM, GDN_HEAD_DIM), F32),
            pltpu.VMEM((SSM_GROUPS, SSM_STATE, (SSM_HEADS // SSM_GROUPS) * SSM_HEAD_DIM), F32),
        ],
        compiler_params=pltpu.CompilerParams(
            dimension_semantics=("arbitrary", "arbitrary"), vmem_limit_bytes=VMEM_LIMIT_LARGE),
        name="mixers",
    )(x, g_pre.reshape(1, d), w_head, w_tail, conv_a_w, pool_w, pool_scale, gdn_conv_w, gdn_norm_g, ssm_conv_w,
      ssm_conv_b, ssm_d_x, ssm_norm_g, alog_row, dtbias_row)


def _xattn_kernel(x_ref, g_ref, wq_ref, kv_ref, o_ref, wb_ref):
    @pl.when((pl.program_id(0) == 0) & (pl.program_id(1) == 0))
    def _():
        wb_ref[...] = wq_ref[...].astype(BF16)

    xn = _rmsnorm(x_ref[...], g_ref[...]).astype(BF16)
    q = jnp.dot(xn, wb_ref[...], preferred_element_type=F32)
    for h in range(XA_HEADS):
        lo = h * XA_HEAD_DIM
        k_h = kv_ref[:, lo:lo + XA_HEAD_DIM]
        v_h = kv_ref[:, D_MODEL + lo:D_MODEL + lo + XA_HEAD_DIM]
        s = _dot_nt(q[:, lo:lo + XA_HEAD_DIM], k_h) * XA_HEAD_DIM ** -0.5
        e = jnp.exp(s - jnp.max(s, axis=-1, keepdims=True))
        p = e / jnp.sum(e, axis=-1, keepdims=True)
        o_ref[:, lo:lo + XA_HEAD_DIM] = _dot(p, v_h).astype(o_ref.dtype)


def _xattn(x, g, wq, layer, kv, batch, seq, tq):
    n, d = x.shape
    n_tiles = seq // tq
    return pl.pallas_call(
        _xattn_kernel,
        grid=(batch, n_tiles),
        in_specs=[
            pl.BlockSpec((tq, d), lambda b, i: (b * n_tiles + i, 0)),
            pl.BlockSpec((1, d), lambda b, i: (0, 0)),
            pl.BlockSpec((None, d, d), lambda b, i: (layer, 0, 0), pipeline_mode=pl.Buffered(1)),
            pl.BlockSpec((MEM_LEN, 2 * d), lambda b, i: (b, 0)),
        ],
        out_specs=pl.BlockSpec((tq, d), lambda b, i: (b * n_tiles + i, 0)),
        out_shape=jax.ShapeDtypeStruct((n, d), BF16),
        scratch_shapes=[pltpu.VMEM((d, d), BF16)],
        compiler_params=pltpu.CompilerParams(
            dimension_semantics=("arbitrary", "arbitrary"), vmem_limit_bytes=VMEM_LIMIT),
        name="xattn",
    )(x, g.reshape(1, d), wq, kv)


def _ffn_kernel(x_ref, gpre_ref, wg_ref, wu_ref, wd_ref, gpost_ref, o_ref, xn_ref, *, row_split):
    j = pl.program_id(1)

    @pl.when(j == 0)
    def _():
        xn_ref[...] = _rmsnorm(x_ref[...], gpre_ref[...]).astype(BF16)
        o_ref[...] = jnp.zeros_like(o_ref)

    wg = wg_ref[...].astype(BF16)
    wu = wu_ref[...].astype(BF16)
    wd = wd_ref[...].astype(BF16)
    rows = o_ref.shape[0] // row_split
    for r in range(0, o_ref.shape[0], rows):
        xn = xn_ref[r:r + rows, :]
        gate = jnp.dot(xn, wg, preferred_element_type=F32)
        up = jnp.dot(xn, wu, preferred_element_type=F32)
        o_ref[r:r + rows, :] += jnp.dot((_silu(gate) * up).astype(BF16), wd, preferred_element_type=F32)

    @pl.when(j == pl.num_programs(1) - 1)
    def _():
        o_ref[...] = x_ref[...] + _rmsnorm(o_ref[...], gpost_ref[...])


def _ffn(x, gpre, w_gu, w_down, layer, gpost, tm, tf, row_split):
    n, d = x.shape
    n_ff = D_FF // tf
    return pl.pallas_call(
        functools.partial(_ffn_kernel, row_split=row_split),
        grid=(n // tm, n_ff),
        in_specs=[
            pl.BlockSpec((tm, d), lambda i, j: (i, 0), pipeline_mode=pl.Buffered(1)),
            pl.BlockSpec((1, d), lambda i, j: (0, 0)),
            pl.BlockSpec((None, d, tf), lambda i, j: (layer, 0, j)),
            pl.BlockSpec((None, d, tf), lambda i, j: (layer, 0, j + n_ff)),
            pl.BlockSpec((None, tf, d), lambda i, j: (layer, j, 0)),
            pl.BlockSpec((1, d), lambda i, j: (0, 0)),
        ],
        out_specs=pl.BlockSpec((tm, d), lambda i, j: (i, 0), pipeline_mode=pl.Buffered(1)),
        out_shape=jax.ShapeDtypeStruct((n, d), F32),
        scratch_shapes=[pltpu.VMEM((tm, d), BF16)],
        compiler_params=pltpu.CompilerParams(
            dimension_semantics=("arbitrary", "arbitrary"), vmem_limit_bytes=VMEM_LIMIT_FFN),
        name="ffn",
    )(x, gpre.reshape(1, d), w_gu, w_gu, w_down, gpost.reshape(1, d))


def _small_row(gdn_vals, ssm_vals):
    row = jnp.zeros((1, 128), F32)
    row = row.at[0, SMALL_A:SMALL_A + GDN_HEADS].set(gdn_vals.astype(F32))
    return row.at[0, SMALL_DT:SMALL_DT + SSM_HEADS].set(ssm_vals.astype(F32))


def kernel(x, mem, norm_g, w_in, conv_a_w, pool_w, pool_scale, gdn_conv_w, gdn_A_log, gdn_dt_bias, gdn_norm_g,
           ssm_conv_w, ssm_conv_b, ssm_A_log, ssm_dt_bias, ssm_D, ssm_norm_g, w_out, xa_wq, xa_wkv, xa_wo,
           ffn_w_gu, ffn_w_down):
    batch, seq, d = x.shape
    depth = w_in.shape[0]
    xf = x.reshape(batch * seq, d)
    memf = mem.reshape(batch * MEM_LEN, d)
    w_in_t = jnp.swapaxes(w_in, 1, 2)
    for l in range(depth):
        g = norm_g[l]
        mixed = _mixers(
            xf, g[0], *_prep_w_in(w_in_t, l), batch, seq, conv_a_w[l], pool_w[l], pool_scale[l].reshape(1, -1),
            gdn_conv_w[l], gdn_norm_g[l].reshape(1, -1), ssm_conv_w[l], ssm_conv_b[l].reshape(1, -1),
            jnp.repeat(ssm_D[l], SSM_HEAD_DIM).reshape(1, -1), ssm_norm_g[l].reshape(1, -1),
            _small_row(gdn_A_log[l], ssm_A_log[l]), _small_row(gdn_dt_bias[l], ssm_dt_bias[l]),
            tile=4 * CHUNK)
        xf = _proj_post(mixed, w_out, l, g[1], xf, tm=512)
        kv = _norm_matmul(memf, g[4], xa_wkv, l, 2 * d, BF16, tm=batch * MEM_LEN, tn=512)
        att = _xattn(xf, g[2], xa_wq, l, kv, batch, seq, tq=512)
        xf = _proj_post(att, xa_wo, l, g[3], xf, tm=512)
        xf = _ffn(xf, g[5], ffn_w_gu, ffn_w_down, l, g[6], tm=1024, tf=512, row_split=2 if l == 0 else 1)
    return xf.reshape(batch, seq, d)
```

```python
import functools

import jax
import jax.numpy as jnp
from jax import lax
from jax.experimental import pallas as pl
from jax.experimental.pallas import tpu as pltpu

F32 = jnp.float32
BF16 = jnp.bfloat16
EPS = 1e-6

D_MODEL = 2048
GROUP_WIDTH = 512
CHUNK = 64
POOL_WINDOWS = (2, 4, 8, 16)
POOL_GROUP = 128
GDN_HEADS = 4
GDN_HEAD_DIM = 128
SSM_HEADS = 8
SSM_HEAD_DIM = 64
SSM_GROUPS = 2
SSM_STATE = 128
SSM_XBC = 1024
MEM_LEN = 256
XA_HEADS = 4
XA_HEAD_DIM = 512
D_FF = 5632

IN_COLS = 5648
HEAD_COLS = 4096
TAIL_START = 4096
TAIL_VALID = IN_COLS - TAIL_START
TAIL_COLS = 1664
TAIL_BLOCK = 2048
TAIL_SKEW = 8
U_COLS = HEAD_COLS + TAIL_COLS
OFF_A = 0
OFF_B = 1536
OFF_QKV = 2048
OFF_CZ = 3584
OFF_DZ = 4096
OFF_XBC = 4608
OFF_SMALL = 5632
SMALL_DT = 0
SMALL_A = 120
SMALL_B = 124
HALO = 16

VMEM_LIMIT = 48 * 1024 * 1024
VMEM_LIMIT_LARGE = 56 * 1024 * 1024
VMEM_LIMIT_FFN = 60000 * 1024


def _rmsnorm(x, g):
    return x * lax.rsqrt(jnp.mean(x * x, axis=-1, keepdims=True) + EPS) * g


def _silu(x):
    return x * jax.nn.sigmoid(x)


def _softplus(x):
    return jnp.maximum(x, 0.0) + jnp.log1p(jnp.exp(-jnp.abs(x)))


def _dot(a, b):
    return jnp.dot(a.astype(BF16), b.astype(BF16), preferred_element_type=F32)


def _dot_nt(a, b):
    return lax.dot_general(a.astype(BF16), b.astype(BF16), (((1,), (1,)), ((), ())),
                           preferred_element_type=F32)


def _dot_tn(a, b):
    return lax.dot_general(a.astype(BF16), b.astype(BF16), (((0,), (0,)), ((), ())),
                           preferred_element_type=F32)


def _split_bf16(a):
    hi = a.astype(BF16)
    return hi, (a - hi.astype(F32)).astype(BF16)


def _dot3(a, b):
    return (jnp.dot(a[0], b[0], preferred_element_type=F32)
            + jnp.dot(a[0], b[1], preferred_element_type=F32)
            + jnp.dot(a[1], b[0], preferred_element_type=F32))


def _norm_matmul_kernel(x_ref, g_ref, w_ref, o_ref, xn_ref):
    @pl.when(pl.program_id(1) == 0)
    def _():
        xn_ref[...] = _rmsnorm(x_ref[...], g_ref[...]).astype(BF16)

    o_ref[...] = jnp.dot(xn_ref[...], w_ref[...].astype(BF16), preferred_element_type=F32).astype(o_ref.dtype)


def _norm_matmul(x, g, w, layer, n_cols, out_dtype, tm, tn):
    n, k = x.shape
    return pl.pallas_call(
        _norm_matmul_kernel,
        grid=(n // tm, n_cols // tn),
        in_specs=[
            pl.BlockSpec((tm, k), lambda i, j: (i, 0)),
            pl.BlockSpec((1, k), lambda i, j: (0, 0)),
            pl.BlockSpec((None, k, tn), lambda i, j: (layer, 0, j)),
        ],
        out_specs=pl.BlockSpec((tm, tn), lambda i, j: (i, j)),
        out_shape=jax.ShapeDtypeStruct((n, n_cols), out_dtype),
        scratch_shapes=[pltpu.VMEM((tm, k), BF16)],
        compiler_params=pltpu.CompilerParams(
            dimension_semantics=("arbitrary", "arbitrary"), vmem_limit_bytes=VMEM_LIMIT_LARGE),
        name="norm_matmul",
    )(x, g.reshape(1, k), w)


PREP_ROWS = 512


def _prep_head_kernel(wt_ref, o_ref):
    o_ref[...] = wt_ref[...].T.astype(BF16)


def _prep_tail_kernel(wt_ref, o_ref):
    dt0 = TAIL_VALID - TAIL_SKEW
    body = TAIL_COLS - 128
    for r in range(0, body, PREP_ROWS):
        o_ref[:, r:r + PREP_ROWS] = wt_ref[TAIL_SKEW + r:TAIL_SKEW + r + PREP_ROWS, :].T.astype(BF16)
    last = jnp.concatenate([wt_ref[dt0:TAIL_VALID, :],
                            jnp.zeros((128 - 2 * TAIL_SKEW, wt_ref.shape[1]), F32),
                            wt_ref[0:TAIL_SKEW, :]], axis=0)
    o_ref[:, body:] = last.T.astype(BF16)


def _prep_w_in(wt, layer):
    k = wt.shape[2]
    head = pl.pallas_call(
        _prep_head_kernel,
        grid=(HEAD_COLS // PREP_ROWS,),
        in_specs=[pl.BlockSpec((None, PREP_ROWS, k), lambda j: (layer, j, 0))],
        out_specs=pl.BlockSpec((k, PREP_ROWS), lambda j: (0, j)),
        out_shape=jax.ShapeDtypeStruct((k, HEAD_COLS), BF16),
        compiler_params=pltpu.CompilerParams(dimension_semantics=("arbitrary",), vmem_limit_bytes=VMEM_LIMIT),
        name="prep_w_in_head",
    )(wt)
    tail = pl.pallas_call(
        _prep_tail_kernel,
        grid=(1,),
        in_specs=[pl.BlockSpec((None, TAIL_BLOCK, k), lambda j: (layer, TAIL_START // TAIL_BLOCK, 0),
                               pipeline_mode=pl.Buffered(1))],
        out_specs=pl.BlockSpec((k, TAIL_COLS), lambda j: (0, 0)),
        out_shape=jax.ShapeDtypeStruct((k, TAIL_COLS), BF16),
        compiler_params=pltpu.CompilerParams(dimension_semantics=("arbitrary",), vmem_limit_bytes=VMEM_LIMIT_LARGE),
        name="prep_w_in_tail",
    )(wt)
    return head, tail


def _proj_post_kernel(a_ref, w_ref, g_ref, x_ref, o_ref, wb_ref):
    @pl.when(pl.program_id(0) == 0)
    def _():
        wb_ref[...] = w_ref[...].astype(BF16)

    y = jnp.dot(a_ref[...], wb_ref[...], preferred_element_type=F32)
    o_ref[...] = x_ref[...] + _rmsnorm(y, g_ref[...])


def _proj_post(a, w, layer, g, x, tm):
    n, k = a.shape
    d = w.shape[2]
    return pl.pallas_call(
        _proj_post_kernel,
        grid=(n // tm,),
        in_specs=[
            pl.BlockSpec((tm, k), lambda i: (i, 0)),
            pl.BlockSpec((None, k, d), lambda i: (layer, 0, 0), pipeline_mode=pl.Buffered(1)),
            pl.BlockSpec((1, d), lambda i: (0, 0)),
            pl.BlockSpec((tm, d), lambda i: (i, 0)),
        ],
        out_specs=pl.BlockSpec((tm, d), lambda i: (i, 0)),
        out_shape=jax.ShapeDtypeStruct((n, d), F32),
        scratch_shapes=[pltpu.VMEM((k, d), BF16)],
        compiler_params=pltpu.CompilerParams(
            dimension_semantics=("arbitrary",), vmem_limit_bytes=VMEM_LIMIT_LARGE),
        name="proj_post",
    )(a, w, g.reshape(1, d), x)


def _shift_rows(x, k):
    return pltpu.roll(x, k, axis=0) if k else x


def _causal_conv(x_ext, w_ref):
    width = w_ref.shape[0]
    if width == 4:
        x1 = _shift_rows(x_ext, 1)
        older = _shift_rows(x_ext * w_ref[1:2, :] + x1 * w_ref[0:1, :], 2)
        return x_ext * w_ref[3:4, :] + x1 * w_ref[2:3, :] + older
    acc = x_ext * w_ref[width - 1:width, :]
    for k in range(width - 1):
        acc = acc + _shift_rows(x_ext, width - 1 - k) * w_ref[k:k + 1, :]
    return acc


def _expand_heads(cols, first, n_heads, width):
    rows = cols.shape[0]
    per_vreg = 128 // width
    lane = lax.broadcasted_iota(jnp.int32, (rows, 128), 1)
    pieces = []
    for m in range(n_heads // per_vreg):
        piece = jnp.broadcast_to(cols[:, first + m * per_vreg:first + m * per_vreg + 1], (rows, 128))
        for j in range(1, per_vreg):
            c = first + m * per_vreg + j
            piece = jnp.where(lane >= j * width, jnp.broadcast_to(cols[:, c:c + 1], (rows, 128)), piece)
        pieces.append(piece)
    return jnp.concatenate(pieces, axis=1)


def _block_diag(y, lane_block, n_blocks):
    return jnp.concatenate([jnp.where(lane_block == b, y, jnp.zeros_like(y)) for b in range(n_blocks)], axis=0)


def _split_block_diag(y, lane_block, n_blocks):
    hi, lo = _split_bf16(y)
    return _block_diag(hi, lane_block, n_blocks), _block_diag(lo, lane_block, n_blocks)


def _unit_lower_inverse(ms, row, col, lane_block, n_blocks):
    same16 = (row >> 4) == (col >> 4)
    same32 = (row >> 5) == (col >> 5)
    off16 = same32 & jnp.logical_not(same16)
    eye = (row == col).astype(F32)
    split = lambda xs: [_split_bf16(x) for x in xs]
    split_bd = lambda xs: [_split_block_diag(x, lane_block, n_blocks) for x in xs]
    d = [jnp.where(same16, m, 0.0) for m in ms]
    d2 = [_dot3(x, y) for x, y in zip(split(d), split_bd(d))]
    p = [eye - x for x in d]
    d2_bd = split_bd(d2)
    p = [x + _dot3(xs, y) for x, xs, y in zip(p, split(p), d2_bd)]
    d4 = [_dot3(x, y) for x, y in zip(split(d2), d2_bd)]
    d4_bd = split_bd(d4)
    p = [x + _dot3(xs, y) for x, xs, y in zip(p, split(p), d4_bd)]
    d8_bd = split_bd([_dot3(x, y) for x, y in zip(split(d4), d4_bd)])
    p = [x + _dot3(xs, y) for x, xs, y in zip(p, split(p), d8_bd)]
    ps, p_bd = split(p), split_bd(p)
    t = split([_dot3(xs, _split_block_diag(jnp.where(off16, m, 0.0), lane_block, n_blocks)) for xs, m in zip(ps, ms)])
    q = [x - _dot3(y, xb) for x, xb, y in zip(p, p_bd, t)]
    qs, q_bd = split(q), split_bd(q)
    t = split([_dot3(xs, _split_block_diag(jnp.where(same32, 0.0, m), lane_block, n_blocks)) for xs, m in zip(qs, ms)])
    return [x - _dot3(y, xb) for x, xb, y in zip(q, q_bd, t)]


def _dot_ltri(ltri, x):
    x1 = x.astype(BF16)
    r1 = x - x1.astype(F32)
    x2 = r1.astype(BF16)
    x3 = (r1 - x2.astype(F32)).astype(BF16)
    return (jnp.dot(ltri, x1, preferred_element_type=F32) + jnp.dot(ltri, x2, preferred_element_type=F32)
            + jnp.dot(ltri, x3, preferred_element_type=F32))


def _mixer_kernel(x_ref, gpre_ref, w_head_ref, w_tail_ref, conv_a_ref, pool_w_ref, pool_scale_ref, gdn_conv_ref,
                  gdn_norm_ref, ssm_conv_ref, ssm_bias_ref, ssm_d_ref, ssm_norm_ref,
                  alog_ref, dtbias_ref, out_ref, u_ref, s_ref, h_ref, *, tile, col_chunk):
    i = pl.program_id(1)
    n_chunks = tile // CHUNK
    hg = SSM_HEADS // SSM_GROUPS
    gw = hg * SSM_HEAD_DIM

    @pl.when(i == 0)
    def _():
        s_ref[...] = jnp.zeros_like(s_ref)
        h_ref[...] = jnp.zeros_like(h_ref)
        u_ref[0:HALO, :] = jnp.zeros((HALO, U_COLS), F32)

    @pl.when(i > 0)
    def _():
        u_ref[0:HALO, :] = u_ref[tile:tile + HALO, :]

    xn = _rmsnorm(x_ref[...], gpre_ref[...]).astype(BF16)
    for c0 in range(0, HEAD_COLS, col_chunk):
        u_ref[HALO:, c0:c0 + col_chunk] = jnp.dot(xn, w_head_ref[:, c0:c0 + col_chunk],
                                                    preferred_element_type=F32)
    u_ref[HALO:, HEAD_COLS:] = jnp.dot(xn, w_tail_ref[...], preferred_element_type=F32)

    def cur(off, width, rows=slice(0, tile)):
        return u_ref[HALO + rows.start:HALO + rows.stop, off:off + width]

    def ext(off, width):
        return u_ref[:, off:off + width]

    p = ext(OFF_A + GROUP_WIDTH, GROUP_WIDTH) * ext(OFF_A + 2 * GROUP_WIDTH, GROUP_WIDTH)
    ya = cur(OFF_A, GROUP_WIDTH) * _causal_conv(p, conv_a_ref)[HALO:]
    out_ref[:, 0:GROUP_WIDTH] = ya.astype(out_ref.dtype)

    pos = (i * tile + 1 + lax.broadcasted_iota(jnp.int32, (tile, 1), 0)).astype(F32)
    for gi, window in enumerate(POOL_WINDOWS):
        off = OFF_B + gi * POOL_GROUP
        s = ext(off, POOL_GROUP)
        k = 1
        while k < window:
            s = s + _shift_rows(s, k)
            k *= 2
        pooled = s[HALO:] / jnp.minimum(pos, float(window)) - cur(off, POOL_GROUP)
        yb = _dot(pooled, pool_w_ref[gi]) * pool_scale_ref[:, gi * POOL_GROUP:(gi + 1) * POOL_GROUP]
        out_ref[:, GROUP_WIDTH + gi * POOL_GROUP:GROUP_WIDTH + (gi + 1) * POOL_GROUP] = yb.astype(out_ref.dtype)

    small = cur(OFF_SMALL, 128)
    sp = _softplus(small + dtbias_ref[...])
    rate = -jnp.exp(alog_ref[...]) * sp
    beta_all = jax.nn.sigmoid(small)

    qkv = _silu(_causal_conv(ext(OFF_QKV, 3 * GROUP_WIDTH), gdn_conv_ref)[HALO:])
    xbc = _silu(_causal_conv(ext(OFF_XBC, SSM_XBC), ssm_conv_ref)[HALO:] + ssm_bias_ref[...])

    n4 = GDN_HEADS * CHUNK
    row = lax.broadcasted_iota(jnp.int32, (CHUNK, n4), 0)
    lane = lax.broadcasted_iota(jnp.int32, (CHUNK, n4), 1)
    col = lane & (CHUNK - 1)
    lane_block = lane >> 6
    incl = row >= col
    strict = row > col
    ltri = (lax.broadcasted_iota(jnp.int32, (CHUNK, CHUNK), 0)
            >= lax.broadcasted_iota(jnp.int32, (CHUNK, CHUNK), 1)).astype(BF16)

    def decay_blocks(rate_c, first, n_heads):
        out = []
        for m in range(n_heads // GDN_HEADS):
            r = _expand_heads(rate_c, first + m * GDN_HEADS, GDN_HEADS, CHUNK)
            diff = _dot_ltri(ltri, jnp.where(strict, r, 0.0))
            out.append(jnp.where(incl, jnp.exp(jnp.where(incl, diff, 0.0)), 0.0))
        return out

    chunks = range(n_chunks)
    rows = [slice(c * CHUNK, (c + 1) * CHUNK) for c in chunks]
    acs = [_dot_ltri(ltri, rate[r]) for r in rows]
    e_acs = [jnp.exp(a) for a in acs]
    e_end = [jnp.exp(a[CHUNK - 1:CHUNK, :] - a) for a in acs]
    e_blk = [jnp.exp(a[CHUNK - 1:CHUNK, :]) for a in acs]

    def head_cols(base, h):
        return slice(base + h * GDN_HEAD_DIM, base + (h + 1) * GDN_HEAD_DIM)

    def l2norm_heads(x, scale):
        return jnp.concatenate(
            [x[:, head_cols(0, h)] * (lax.rsqrt(jnp.sum(x[:, head_cols(0, h)] * x[:, head_cols(0, h)],
                                                        axis=-1, keepdims=True) + EPS) * scale)
             for h in range(GDN_HEADS)], axis=1)

    zeros_head = jnp.zeros((CHUNK, GDN_HEAD_DIM), BF16)
    q_all = [l2norm_heads(qkv[r, 0:GROUP_WIDTH], GDN_HEAD_DIM ** -0.5) for r in rows]
    k_all = [l2norm_heads(qkv[r, GROUP_WIDTH:2 * GROUP_WIDTH], 1.0) for r in rows]
    v_all = [qkv[r, 2 * GROUP_WIDTH:3 * GROUP_WIDTH] for r in rows]
    beta_x = [_expand_heads(beta_all[r], SMALL_B, GDN_HEADS, GDN_HEAD_DIM) for r in rows]
    e_g_x = [_expand_heads(e, SMALL_A, GDN_HEADS, GDN_HEAD_DIM) for e in e_acs]
    kb = [x * b for x, b in zip(k_all, beta_x)]
    k_rows = []
    for x in k_all:
        xb = x.astype(BF16)
        k_rows.append(jnp.concatenate(
            [jnp.concatenate([xb[:, head_cols(0, h)] if h == g else zeros_head for h in range(GDN_HEADS)], axis=1)
             for g in range(GDN_HEADS)], axis=0))
    decay = [decay_blocks(rate[r], SMALL_A, GDN_HEADS)[0] for r in rows]
    m_low = [jnp.where(strict, _dot_nt(x, y) * dc, 0.0) for x, y, dc in zip(kb, k_rows, decay)]
    qk = [_dot_nt(x, y) * dc for x, y, dc in zip(q_all, k_rows, decay)]
    t_inv = _unit_lower_inverse(m_low, row, col, lane_block, GDN_HEADS)
    zeros_rhs = jnp.zeros((CHUNK, 2 * GDN_HEAD_DIM), F32)
    sol = []
    for c in chunks:
        vb, kg = v_all[c] * beta_x[c], kb[c] * e_g_x[c]
        rhs = jnp.concatenate(
            [jnp.concatenate([jnp.concatenate([vb[:, head_cols(0, h)], kg[:, head_cols(0, h)]], axis=1)
                              if h == g else zeros_rhs for h in range(GDN_HEADS)], axis=1)
             for g in range(GDN_HEADS)], axis=0)
        sol.append(_dot3(_split_bf16(t_inv[c]), _split_bf16(rhs)))
    q_dec = [x * e for x, e in zip(q_all, e_g_x)]
    k_end = [x * _expand_heads(e, SMALL_A, GDN_HEADS, GDN_HEAD_DIM) for x, e in zip(k_all, e_end)]

    x_in, y_diag, s_inc, c_mat, e_acs_x, e_blk_x = [], [], [], [], [], []
    for c in chunks:
        r = rows[c]
        x_in.append(xbc[r, :GROUP_WIDTH])
        x_dt = x_in[c] * _expand_heads(sp[r], SMALL_DT, SSM_HEADS, SSM_HEAD_DIM)
        x_end = x_dt * _expand_heads(e_end[c], SMALL_DT, SSM_HEADS, SSM_HEAD_DIM)
        e_acs_x.append(_expand_heads(e_acs[c], SMALL_DT, SSM_HEADS, SSM_HEAD_DIM))
        e_blk_x.append(_expand_heads(e_blk[c], SMALL_DT, SSM_HEADS, SSM_HEAD_DIM))
        l_m = decay_blocks(rate[r], SMALL_DT, SSM_HEADS)
        for g in range(SSM_GROUPS):
            b_g = xbc[r, GROUP_WIDTH + g * SSM_STATE:GROUP_WIDTH + (g + 1) * SSM_STATE]
            c_g = xbc[r, GROUP_WIDTH + (SSM_GROUPS + g) * SSM_STATE:GROUP_WIDTH + (SSM_GROUPS + g + 1) * SSM_STATE]
            b_bf = b_g.astype(BF16)
            cb = _dot_nt(c_g, jnp.concatenate([b_bf] * hg, axis=0))
            x_g = x_dt[:, g * gw:(g + 1) * gw]
            y_diag.append(_dot(cb * l_m[g], _block_diag(x_g.astype(BF16), lane_block, hg)))
            s_inc.append(_dot_tn(b_bf, x_end[:, g * gw:(g + 1) * gw]))
            c_mat.append(c_g)

    s_state = [s_ref[h] for h in range(GDN_HEADS)]
    h_state = [h_ref[g] for g in range(SSM_GROUPS)]
    zeros_v = jnp.zeros((CHUNK, GDN_HEAD_DIM), BF16)
    for c in chunks:
        r = rows[c]
        hd = 2 * GDN_HEAD_DIM
        v_new = [sol[c][:, h * hd:h * hd + GDN_HEAD_DIM]
                 - _dot(sol[c][:, h * hd + GDN_HEAD_DIM:(h + 1) * hd], s_state[h]) for h in range(GDN_HEADS)]
        v_rows = jnp.concatenate(
            [jnp.concatenate([v_new[h].astype(BF16) if h == g else zeros_v for h in range(GDN_HEADS)], axis=1)
             for g in range(GDN_HEADS)], axis=0)
        o_intra = _dot(qk[c], v_rows)
        o = [_dot(q_dec[c][:, head_cols(0, h)], s_state[h]) + o_intra[:, head_cols(0, h)]
             for h in range(GDN_HEADS)]
        s_state = [s_state[h] * e_blk[c][:, SMALL_A + h:SMALL_A + h + 1]
                   + _dot_tn(k_end[c][:, head_cols(0, h)], v_new[h]) for h in range(GDN_HEADS)]
        for h in range(GDN_HEADS):
            z = cur(OFF_CZ + h * GDN_HEAD_DIM, GDN_HEAD_DIM, r)
            y = _rmsnorm(o[h], gdn_norm_ref[...]) * _silu(z)
            out_ref[r, head_cols(2 * GROUP_WIDTH, h)] = y.astype(out_ref.dtype)

        ys = []
        for g in range(SSM_GROUPS):
            n = c * SSM_GROUPS + g
            ys.append(y_diag[n] + _dot(c_mat[n], h_state[g]) * e_acs_x[c][:, g * gw:(g + 1) * gw])
            h_state[g] = h_state[g] * e_blk_x[c][:, g * gw:(g + 1) * gw] + s_inc[n]
        y = jnp.concatenate(ys, axis=1) + ssm_d_ref[...] * x_in[c]
        y = y * _silu(cur(OFF_DZ, GROUP_WIDTH, r))
        for g in range(SSM_GROUPS):
            yn = _rmsnorm(y[:, g * gw:(g + 1) * gw], ssm_norm_ref[:, g * gw:(g + 1) * gw])
            out_ref[r, 3 * GROUP_WIDTH + g * gw:3 * GROUP_WIDTH + (g + 1) * gw] = yn.astype(out_ref.dtype)
    for h in range(GDN_HEADS):
        s_ref[h] = s_state[h]
    for g in range(SSM_GROUPS):
        h_ref[g] = h_state[g]


def _mixers(x, g_pre, w_head, w_tail, batch, seq, conv_a_w, pool_w, pool_scale, gdn_conv_w, gdn_norm_g, ssm_conv_w,
            ssm_conv_b, ssm_d_x, ssm_norm_g, alog_row, dtbias_row, tile, col_chunk):
    n_tiles = seq // tile
    d = x.shape[1]

    def const(shape):
        return pl.BlockSpec(shape, lambda b, i: (0,) * len(shape))

    return pl.pallas_call(
        functools.partial(_mixer_kernel, tile=tile, col_chunk=col_chunk),
        grid=(batch, n_tiles),
        in_specs=[
            pl.BlockSpec((tile, d), lambda b, i: (b * n_tiles + i, 0)),
            const((1, d)),
            pl.BlockSpec((d, HEAD_COLS), lambda b, i: (0, 0), pipeline_mode=pl.Buffered(1)),
            pl.BlockSpec((d, TAIL_COLS), lambda b, i: (0, 0), pipeline_mode=pl.Buffered(1)),
            const(conv_a_w.shape), const(pool_w.shape), const(pool_scale.shape), const(gdn_conv_w.shape),
            const(gdn_norm_g.shape), const(ssm_conv_w.shape), const(ssm_conv_b.shape), const(ssm_d_x.shape),
            const(ssm_norm_g.shape), const(alog_row.shape), const(dtbias_row.shape),
        ],
        out_specs=pl.BlockSpec((tile, D_MODEL), lambda b, i: (b * n_tiles + i, 0)),
        out_shape=jax.ShapeDtypeStruct((batch * seq, D_MODEL), BF16),
        scratch_shapes=[
            pltpu.VMEM((HALO + tile, U_COLS), F32),
            pltpu.VMEM((GDN_HEADS, GDN_HEAD_DIM, GDN_HEAD_DIM), F32),
            pltpu.VMEM((SSM_GROUPS, SSM_STATE, (SSM_HEADS // SSM_GROUPS) * SSM_HEAD_DIM), F32),
        ],
        compiler_params=pltpu.CompilerParams(
            dimension_semantics=("arbitrary", "arbitrary"), vmem_limit_bytes=VMEM_LIMIT_LARGE),
        name="mixers",
    )(x, g_pre.reshape(1, d), w_head, w_tail, conv_a_w, pool_w, pool_scale, gdn_conv_w, gdn_norm_g, ssm_conv_w,
      ssm_conv_b, ssm_d_x, ssm_norm_g, alog_row, dtbias_row)


def _xattn_kernel(x_ref, g_ref, wq_ref, kv_ref, o_ref, wb_ref):
    @pl.when((pl.program_id(0) == 0) & (pl.program_id(1) == 0))
    def _():
        wb_ref[...] = wq_ref[...].astype(BF16)

    xn = _rmsnorm(x_ref[...], g_ref[...]).astype(BF16)
    q = jnp.dot(xn, wb_ref[...], preferred_element_type=F32)
    for h in range(XA_HEADS):
        lo = h * XA_HEAD_DIM
        k_h = kv_ref[:, lo:lo + XA_HEAD_DIM]
        v_h = kv_ref[:, D_MODEL + lo:D_MODEL + lo + XA_HEAD_DIM]
        s = _dot_nt(q[:, lo:lo + XA_HEAD_DIM], k_h) * XA_HEAD_DIM ** -0.5
        e = jnp.exp(s - jnp.max(s, axis=-1, keepdims=True))
        p = e / jnp.sum(e, axis=-1, keepdims=True)
        o_ref[:, lo:lo + XA_HEAD_DIM] = _dot(p, v_h).astype(o_ref.dtype)


def _xattn(x, g, wq, layer, kv, batch, seq, tq):
    n, d = x.shape
    n_tiles = seq // tq
    return pl.pallas_call(
        _xattn_kernel,
        grid=(batch, n_tiles),
        in_specs=[
            pl.BlockSpec((tq, d), lambda b, i: (b * n_tiles + i, 0)),
            pl.BlockSpec((1, d), lambda b, i: (0, 0)),
            pl.BlockSpec((None, d, d), lambda b, i: (layer, 0, 0), pipeline_mode=pl.Buffered(1)),
            pl.BlockSpec((MEM_LEN, 2 * d), lambda b, i: (b, 0)),
        ],
        out_specs=pl.BlockSpec((tq, d), lambda b, i: (b * n_tiles + i, 0)),
        out_shape=jax.ShapeDtypeStruct((n, d), BF16),
        scratch_shapes=[pltpu.VMEM((d, d), BF16)],
        compiler_params=pltpu.CompilerParams(
            dimension_semantics=("arbitrary", "arbitrary"), vmem_limit_bytes=VMEM_LIMIT),
        name="xattn",
    )(x, g.reshape(1, d), wq, kv)


def _ffn_kernel(x_ref, gpre_ref, wg_ref, wu_ref, wd_ref, gpost_ref, o_ref, xn_ref):
    j = pl.program_id(1)

    @pl.when(j == 0)
    def _():
        xn_ref[...] = _rmsnorm(x_ref[...], gpre_ref[...]).astype(BF16)
        o_ref[...] = jnp.zeros_like(o_ref)

    xn = xn_ref[...]
    gate = jnp.dot(xn, wg_ref[...].astype(BF16), preferred_element_type=F32)
    up = jnp.dot(xn, wu_ref[...].astype(BF16), preferred_element_type=F32)
    o_ref[...] += jnp.dot((_silu(gate) * up).astype(BF16), wd_ref[...].astype(BF16), preferred_element_type=F32)

    @pl.when(j == pl.num_programs(1) - 1)
    def _():
        o_ref[...] = x_ref[...] + _rmsnorm(o_ref[...], gpost_ref[...])


def _ffn(x, gpre, w_gu, w_down, layer, gpost, tm, tf):
    n, d = x.shape
    n_ff = D_FF // tf
    return pl.pallas_call(
        _ffn_kernel,
        grid=(n // tm, n_ff),
        in_specs=[
            pl.BlockSpec((tm, d), lambda i, j: (i, 0), pipeline_mode=pl.Buffered(1)),
            pl.BlockSpec((1, d), lambda i, j: (0, 0)),
            pl.BlockSpec((None, d, tf), lambda i, j: (layer, 0, j)),
            pl.BlockSpec((None, d, tf), lambda i, j: (layer, 0, j + n_ff)),
            pl.BlockSpec((None, tf, d), lambda i, j: (layer, j, 0)),
            pl.BlockSpec((1, d), lambda i, j: (0, 0)),
        ],
        out_specs=pl.BlockSpec((tm, d), lambda i, j: (i, 0), pipeline_mode=pl.Buffered(1)),
        out_shape=jax.ShapeDtypeStruct((n, d), F32),
        scratch_shapes=[pltpu.VMEM((tm, d), BF16)],
        compiler_params=pltpu.CompilerParams(
            dimension_semantics=("arbitrary", "arbitrary"), vmem_limit_bytes=VMEM_LIMIT_FFN),
        name="ffn",
    )(x, gpre.reshape(1, d), w_gu, w_gu, w_down, gpost.reshape(1, d))


def _small_row(gdn_vals, ssm_vals):
    row = jnp.zeros((1, 128), F32)
    row = row.at[0, SMALL_A:SMALL_A + GDN_HEADS].set(gdn_vals.astype(F32))
    return row.at[0, SMALL_DT:SMALL_DT + SSM_HEADS].set(ssm_vals.astype(F32))


def kernel(x, mem, norm_g, w_in, conv_a_w, pool_w, pool_scale, gdn_conv_w, gdn_A_log, gdn_dt_bias, gdn_norm_g,
           ssm_conv_w, ssm_conv_b, ssm_A_log, ssm_dt_bias, ssm_D, ssm_norm_g, w_out, xa_wq, xa_wkv, xa_wo,
           ffn_w_gu, ffn_w_down):
    batch, seq, d = x.shape
    depth = w_in.shape[0]
    xf = x.reshape(batch * seq, d)
    memf = mem.reshape(batch * MEM_LEN, d)
    w_in_t = jnp.swapaxes(w_in, 1, 2)
    for l in range(depth):
        g = norm_g[l]
        mixed = _mixers(
            xf, g[0], *_prep_w_in(w_in_t, l), batch, seq, conv_a_w[l], pool_w[l], pool_scale[l].reshape(1, -1),
            gdn_conv_w[l], gdn_norm_g[l].reshape(1, -1), ssm_conv_w[l], ssm_conv_b[l].reshape(1, -1),
            jnp.repeat(ssm_D[l], SSM_HEAD_DIM).reshape(1, -1), ssm_norm_g[l].reshape(1, -1),
            _small_row(gdn_A_log[l], ssm_A_log[l]), _small_row(gdn_dt_bias[l], ssm_dt_bias[l]),
            tile=4 * CHUNK, col_chunk=HEAD_COLS if l == 0 else 1024)
        xf = _proj_post(mixed, w_out, l, g[1], xf, tm=512)
        kv = _norm_matmul(memf, g[4], xa_wkv, l, 2 * d, BF16, tm=batch * MEM_LEN, tn=512)
        att = _xattn(xf, g[2], xa_wq, l, kv, batch, seq, tq=512)
        xf = _proj_post(att, xa_wo, l, g[3], xf, tm=512)
        xf = _ffn(xf, g[5], ffn_w_gu, ffn_w_down, l, g[6], tm=1024, tf=512)
    return xf.reshape(batch, seq, d)
```

```python
import functools

import jax
import jax.numpy as jnp
from jax import lax
from jax.experimental import pallas as pl
from jax.experimental.pallas import tpu as pltpu

F32 = jnp.float32
BF16 = jnp.bfloat16
EPS = 1e-6

D_MODEL = 2048
GROUP_WIDTH = 512
CHUNK = 64
POOL_WINDOWS = (2, 4, 8, 16)
POOL_GROUP = 128
GDN_HEADS = 4
GDN_HEAD_DIM = 128
SSM_HEADS = 8
SSM_HEAD_DIM = 64
SSM_GROUPS = 2
SSM_STATE = 128
SSM_XBC = 1024
MEM_LEN = 256
XA_HEADS = 4
XA_HEAD_DIM = 512
D_FF = 5632

IN_COLS = 5648
HEAD_COLS = 4096
TAIL_START = 4096
TAIL_VALID = IN_COLS - TAIL_START
TAIL_COLS = 1664
TAIL_BLOCK = 2048
TAIL_SKEW = 8
U_COLS = HEAD_COLS + TAIL_COLS
OFF_A = 0
OFF_B = 1536
OFF_QKV = 2048
OFF_CZ = 3584
OFF_DZ = 4096
OFF_XBC = 4608
OFF_SMALL = 5632
SMALL_DT = 0
SMALL_A = 120
SMALL_B = 124
HALO = 16

VMEM_LIMIT = 48 * 1024 * 1024
VMEM_LIMIT_LARGE = 56 * 1024 * 1024
VMEM_LIMIT_FFN = 60000 * 1024


def _rmsnorm(x, g):
    return x * lax.rsqrt(jnp.mean(x * x, axis=-1, keepdims=True) + EPS) * g


def _silu(x):
    return x * jax.nn.sigmoid(x)


def _softplus(x):
    return jnp.maximum(x, 0.0) + jnp.log1p(jnp.exp(-jnp.abs(x)))


def _dot(a, b):
    return jnp.dot(a.astype(BF16), b.astype(BF16), preferred_element_type=F32)


def _dot_nt(a, b):
    return lax.dot_general(a.astype(BF16), b.astype(BF16), (((1,), (1,)), ((), ())),
                           preferred_element_type=F32)


def _dot_tn(a, b):
    return lax.dot_general(a.astype(BF16), b.astype(BF16), (((0,), (0,)), ((), ())),
                           preferred_element_type=F32)


def _split_bf16(a):
    hi = a.astype(BF16)
    return hi, (a - hi.astype(F32)).astype(BF16)


def _dot3(a, b):
    return (jnp.dot(a[0], b[0], preferred_element_type=F32)
            + jnp.dot(a[0], b[1], preferred_element_type=F32)
            + jnp.dot(a[1], b[0], preferred_element_type=F32))


def _norm_matmul_kernel(x_ref, g_ref, w_ref, o_ref, xn_ref):
    @pl.when(pl.program_id(1) == 0)
    def _():
        xn_ref[...] = _rmsnorm(x_ref[...], g_ref[...]).astype(BF16)

    o_ref[...] = jnp.dot(xn_ref[...], w_ref[...].astype(BF16), preferred_element_type=F32).astype(o_ref.dtype)


def _norm_matmul(x, g, w, layer, n_cols, out_dtype, tm, tn):
    n, k = x.shape
    return pl.pallas_call(
        _norm_matmul_kernel,
        grid=(n // tm, n_cols // tn),
        in_specs=[
            pl.BlockSpec((tm, k), lambda i, j: (i, 0)),
            pl.BlockSpec((1, k), lambda i, j: (0, 0)),
            pl.BlockSpec((None, k, tn), lambda i, j: (layer, 0, j)),
        ],
        out_specs=pl.BlockSpec((tm, tn), lambda i, j: (i, j)),
        out_shape=jax.ShapeDtypeStruct((n, n_cols), out_dtype),
        scratch_shapes=[pltpu.VMEM((tm, k), BF16)],
        compiler_params=pltpu.CompilerParams(
            dimension_semantics=("arbitrary", "arbitrary"), vmem_limit_bytes=VMEM_LIMIT_LARGE),
        name="norm_matmul",
    )(x, g.reshape(1, k), w)


PREP_ROWS = 512


def _prep_head_kernel(wt_ref, o_ref):
    o_ref[...] = wt_ref[...].T.astype(BF16)


def _prep_tail_kernel(wt_ref, o_ref):
    dt0 = TAIL_VALID - TAIL_SKEW
    body = TAIL_COLS - 128
    for r in range(0, body, PREP_ROWS):
        o_ref[:, r:r + PREP_ROWS] = wt_ref[TAIL_SKEW + r:TAIL_SKEW + r + PREP_ROWS, :].T.astype(BF16)
    last = jnp.concatenate([wt_ref[dt0:TAIL_VALID, :],
                            jnp.zeros((128 - 2 * TAIL_SKEW, wt_ref.shape[1]), F32),
                            wt_ref[0:TAIL_SKEW, :]], axis=0)
    o_ref[:, body:] = last.T.astype(BF16)


def _prep_w_in(wt, layer):
    k = wt.shape[2]
    head = pl.pallas_call(
        _prep_head_kernel,
        grid=(HEAD_COLS // PREP_ROWS,),
        in_specs=[pl.BlockSpec((None, PREP_ROWS, k), lambda j: (layer, j, 0))],
        out_specs=pl.BlockSpec((k, PREP_ROWS), lambda j: (0, j)),
        out_shape=jax.ShapeDtypeStruct((k, HEAD_COLS), BF16),
        compiler_params=pltpu.CompilerParams(dimension_semantics=("arbitrary",), vmem_limit_bytes=VMEM_LIMIT),
        name="prep_w_in_head",
    )(wt)
    tail = pl.pallas_call(
        _prep_tail_kernel,
        grid=(1,),
        in_specs=[pl.BlockSpec((None, TAIL_BLOCK, k), lambda j: (layer, TAIL_START // TAIL_BLOCK, 0),
                               pipeline_mode=pl.Buffered(1))],
        out_specs=pl.BlockSpec((k, TAIL_COLS), lambda j: (0, 0)),
        out_shape=jax.ShapeDtypeStruct((k, TAIL_COLS), BF16),
        compiler_params=pltpu.CompilerParams(dimension_semantics=("arbitrary",), vmem_limit_bytes=VMEM_LIMIT_LARGE),
        name="prep_w_in_tail",
    )(wt)
    return head, tail


def _proj_post_kernel(a_ref, w_ref, g_ref, x_ref, o_ref, wb_ref, *, row_chunks):
    @pl.when(pl.program_id(0) == 0)
    def _():
        wb_ref[...] = w_ref[...].astype(BF16)

    rows = o_ref.shape[0] // row_chunks
    for r in range(0, o_ref.shape[0], rows):
        y = jnp.dot(a_ref[r:r + rows, :], wb_ref[...], preferred_element_type=F32)
        o_ref[r:r + rows, :] = x_ref[r:r + rows, :] + _rmsnorm(y, g_ref[...])


def _proj_post(a, w, layer, g, x, tm, row_chunks):
    n, k = a.shape
    d = w.shape[2]
    return pl.pallas_call(
        functools.partial(_proj_post_kernel, row_chunks=row_chunks),
        grid=(n // tm,),
        in_specs=[
            pl.BlockSpec((tm, k), lambda i: (i, 0)),
            pl.BlockSpec((None, k, d), lambda i: (layer, 0, 0), pipeline_mode=pl.Buffered(1)),
            pl.BlockSpec((1, d), lambda i: (0, 0)),
            pl.BlockSpec((tm, d), lambda i: (i, 0)),
        ],
        out_specs=pl.BlockSpec((tm, d), lambda i: (i, 0)),
        out_shape=jax.ShapeDtypeStruct((n, d), F32),
        scratch_shapes=[pltpu.VMEM((k, d), BF16)],
        compiler_params=pltpu.CompilerParams(
            dimension_semantics=("arbitrary",), vmem_limit_bytes=VMEM_LIMIT_LARGE),
        name="proj_post",
    )(a, w, g.reshape(1, d), x)


def _shift_rows(x, k):
    return pltpu.roll(x, k, axis=0) if k else x


def _causal_conv(x_ext, w_ref):
    width = w_ref.shape[0]
    if width == 4:
        x1 = _shift_rows(x_ext, 1)
        older = _shift_rows(x_ext * w_ref[1:2, :] + x1 * w_ref[0:1, :], 2)
        return x_ext * w_ref[3:4, :] + x1 * w_ref[2:3, :] + older
    acc = x_ext * w_ref[width - 1:width, :]
    for k in range(width - 1):
        acc = acc + _shift_rows(x_ext, width - 1 - k) * w_ref[k:k + 1, :]
    return acc


def _expand_heads(cols, first, n_heads, width):
    rows = cols.shape[0]
    per_vreg = 128 // width
    lane = lax.broadcasted_iota(jnp.int32, (rows, 128), 1)
    pieces = []
    for m in range(n_heads // per_vreg):
        piece = jnp.broadcast_to(cols[:, first + m * per_vreg:first + m * per_vreg + 1], (rows, 128))
        for j in range(1, per_vreg):
            c = first + m * per_vreg + j
            piece = jnp.where(lane >= j * width, jnp.broadcast_to(cols[:, c:c + 1], (rows, 128)), piece)
        pieces.append(piece)
    return jnp.concatenate(pieces, axis=1)


def _block_diag(y, lane_block, n_blocks):
    return jnp.concatenate([jnp.where(lane_block == b, y, jnp.zeros_like(y)) for b in range(n_blocks)], axis=0)


def _split_block_diag(y, lane_block, n_blocks):
    hi, lo = _split_bf16(y)
    return _block_diag(hi, lane_block, n_blocks), _block_diag(lo, lane_block, n_blocks)


def _unit_lower_inverse(ms, row, col, lane_block, n_blocks):
    same16 = (row >> 4) == (col >> 4)
    same32 = (row >> 5) == (col >> 5)
    off16 = same32 & jnp.logical_not(same16)
    eye = (row == col).astype(F32)
    split = lambda xs: [_split_bf16(x) for x in xs]
    split_bd = lambda xs: [_split_block_diag(x, lane_block, n_blocks) for x in xs]
    d = [jnp.where(same16, m, 0.0) for m in ms]
    d2 = [_dot3(x, y) for x, y in zip(split(d), split_bd(d))]
    p = [eye - x for x in d]
    d2_bd = split_bd(d2)
    p = [x + _dot3(xs, y) for x, xs, y in zip(p, split(p), d2_bd)]
    d4 = [_dot3(x, y) for x, y in zip(split(d2), d2_bd)]
    d4_bd = split_bd(d4)
    p = [x + _dot3(xs, y) for x, xs, y in zip(p, split(p), d4_bd)]
    d8_bd = split_bd([_dot3(x, y) for x, y in zip(split(d4), d4_bd)])
    p = [x + _dot3(xs, y) for x, xs, y in zip(p, split(p), d8_bd)]
    ps, p_bd = split(p), split_bd(p)
    t = split([_dot3(xs, _split_block_diag(jnp.where(off16, m, 0.0), lane_block, n_blocks)) for xs, m in zip(ps, ms)])
    q = [x - _dot3(y, xb) for x, xb, y in zip(p, p_bd, t)]
    qs, q_bd = split(q), split_bd(q)
    t = split([_dot3(xs, _split_block_diag(jnp.where(same32, 0.0, m), lane_block, n_blocks)) for xs, m in zip(qs, ms)])
    return [x - _dot3(y, xb) for x, xb, y in zip(q, q_bd, t)]


def _dot_ltri(ltri, x):
    x1 = x.astype(BF16)
    r1 = x - x1.astype(F32)
    x2 = r1.astype(BF16)
    x3 = (r1 - x2.astype(F32)).astype(BF16)
    return (jnp.dot(ltri, x1, preferred_element_type=F32) + jnp.dot(ltri, x2, preferred_element_type=F32)
            + jnp.dot(ltri, x3, preferred_element_type=F32))


def _mixer_kernel(x_ref, gpre_ref, w_head_ref, w_tail_ref, conv_a_ref, pool_w_ref, pool_scale_ref, gdn_conv_ref,
                  gdn_norm_ref, ssm_conv_ref, ssm_bias_ref, ssm_d_ref, ssm_norm_ref,
                  alog_ref, dtbias_ref, out_ref, u_ref, s_ref, h_ref, *, tile, col_chunk):
    i = pl.program_id(1)
    n_chunks = tile // CHUNK
    hg = SSM_HEADS // SSM_GROUPS
    gw = hg * SSM_HEAD_DIM

    @pl.when(i == 0)
    def _():
        s_ref[...] = jnp.zeros_like(s_ref)
        h_ref[...] = jnp.zeros_like(h_ref)
        u_ref[0:HALO, :] = jnp.zeros((HALO, U_COLS), F32)

    @pl.when(i > 0)
    def _():
        u_ref[0:HALO, :] = u_ref[tile:tile + HALO, :]

    xn = _rmsnorm(x_ref[...], gpre_ref[...]).astype(BF16)
    for c0 in range(0, HEAD_COLS, col_chunk):
        u_ref[HALO:, c0:c0 + col_chunk] = jnp.dot(xn, w_head_ref[:, c0:c0 + col_chunk],
                                                    preferred_element_type=F32)
    u_ref[HALO:, HEAD_COLS:] = jnp.dot(xn, w_tail_ref[...], preferred_element_type=F32)

    def cur(off, width, rows=slice(0, tile)):
        return u_ref[HALO + rows.start:HALO + rows.stop, off:off + width]

    def ext(off, width):
        return u_ref[:, off:off + width]

    p = ext(OFF_A + GROUP_WIDTH, GROUP_WIDTH) * ext(OFF_A + 2 * GROUP_WIDTH, GROUP_WIDTH)
    ya = cur(OFF_A, GROUP_WIDTH) * _causal_conv(p, conv_a_ref)[HALO:]
    out_ref[:, 0:GROUP_WIDTH] = ya.astype(out_ref.dtype)

    pos = (i * tile + 1 + lax.broadcasted_iota(jnp.int32, (tile, 1), 0)).astype(F32)
    for gi, window in enumerate(POOL_WINDOWS):
        off = OFF_B + gi * POOL_GROUP
        s = ext(off, POOL_GROUP)
        k = 1
        while k < window:
            s = s + _shift_rows(s, k)
            k *= 2
        pooled = s[HALO:] / jnp.minimum(pos, float(window)) - cur(off, POOL_GROUP)
        yb = _dot(pooled, pool_w_ref[gi]) * pool_scale_ref[:, gi * POOL_GROUP:(gi + 1) * POOL_GROUP]
        out_ref[:, GROUP_WIDTH + gi * POOL_GROUP:GROUP_WIDTH + (gi + 1) * POOL_GROUP] = yb.astype(out_ref.dtype)

    small = cur(OFF_SMALL, 128)
    sp = _softplus(small + dtbias_ref[...])
    rate = -jnp.exp(alog_ref[...]) * sp
    beta_all = jax.nn.sigmoid(small)

    qkv = _silu(_causal_conv(ext(OFF_QKV, 3 * GROUP_WIDTH), gdn_conv_ref)[HALO:])
    xbc = _silu(_causal_conv(ext(OFF_XBC, SSM_XBC), ssm_conv_ref)[HALO:] + ssm_bias_ref[...])

    n4 = GDN_HEADS * CHUNK
    row = lax.broadcasted_iota(jnp.int32, (CHUNK, n4), 0)
    lane = lax.broadcasted_iota(jnp.int32, (CHUNK, n4), 1)
    col = lane & (CHUNK - 1)
    lane_block = lane >> 6
    incl = row >= col
    strict = row > col
    ltri = (lax.broadcasted_iota(jnp.int32, (CHUNK, CHUNK), 0)
            >= lax.broadcasted_iota(jnp.int32, (CHUNK, CHUNK), 1)).astype(BF16)

    def decay_blocks(rate_c, first, n_heads):
        out = []
        for m in range(n_heads // GDN_HEADS):
            r = _expand_heads(rate_c, first + m * GDN_HEADS, GDN_HEADS, CHUNK)
            diff = _dot_ltri(ltri, jnp.where(strict, r, 0.0))
            out.append(jnp.where(incl, jnp.exp(jnp.where(incl, diff, 0.0)), 0.0))
        return out

    chunks = range(n_chunks)
    rows = [slice(c * CHUNK, (c + 1) * CHUNK) for c in chunks]
    acs = [_dot_ltri(ltri, rate[r]) for r in rows]
    e_acs = [jnp.exp(a) for a in acs]
    e_end = [jnp.exp(a[CHUNK - 1:CHUNK, :] - a) for a in acs]
    e_blk = [jnp.exp(a[CHUNK - 1:CHUNK, :]) for a in acs]

    def head_cols(base, h):
        return slice(base + h * GDN_HEAD_DIM, base + (h + 1) * GDN_HEAD_DIM)

    def l2norm_heads(x, scale):
        return jnp.concatenate(
            [x[:, head_cols(0, h)] * (lax.rsqrt(jnp.sum(x[:, head_cols(0, h)] * x[:, head_cols(0, h)],
                                                        axis=-1, keepdims=True) + EPS) * scale)
             for h in range(GDN_HEADS)], axis=1)

    zeros_head = jnp.zeros((CHUNK, GDN_HEAD_DIM), BF16)
    q_all = [l2norm_heads(qkv[r, 0:GROUP_WIDTH], GDN_HEAD_DIM ** -0.5) for r in rows]
    k_all = [l2norm_heads(qkv[r, GROUP_WIDTH:2 * GROUP_WIDTH], 1.0) for r in rows]
    v_all = [qkv[r, 2 * GROUP_WIDTH:3 * GROUP_WIDTH] for r in rows]
    beta_x = [_expand_heads(beta_all[r], SMALL_B, GDN_HEADS, GDN_HEAD_DIM) for r in rows]
    e_g_x = [_expand_heads(e, SMALL_A, GDN_HEADS, GDN_HEAD_DIM) for e in e_acs]
    kb = [x * b for x, b in zip(k_all, beta_x)]
    k_rows = []
    for x in k_all:
        xb = x.astype(BF16)
        k_rows.append(jnp.concatenate(
            [jnp.concatenate([xb[:, head_cols(0, h)] if h == g else zeros_head for h in range(GDN_HEADS)], axis=1)
             for g in range(GDN_HEADS)], axis=0))
    decay = [decay_blocks(rate[r], SMALL_A, GDN_HEADS)[0] for r in rows]
    m_low = [jnp.where(strict, _dot_nt(x, y) * dc, 0.0) for x, y, dc in zip(kb, k_rows, decay)]
    qk = [_dot_nt(x, y) * dc for x, y, dc in zip(q_all, k_rows, decay)]
    t_inv = _unit_lower_inverse(m_low, row, col, lane_block, GDN_HEADS)
    zeros_rhs = jnp.zeros((CHUNK, 2 * GDN_HEAD_DIM), F32)
    sol = []
    for c in chunks:
        vb, kg = v_all[c] * beta_x[c], kb[c] * e_g_x[c]
        rhs = jnp.concatenate(
            [jnp.concatenate([jnp.concatenate([vb[:, head_cols(0, h)], kg[:, head_cols(0, h)]], axis=1)
                              if h == g else zeros_rhs for h in range(GDN_HEADS)], axis=1)
             for g in range(GDN_HEADS)], axis=0)
        sol.append(_dot3(_split_bf16(t_inv[c]), _split_bf16(rhs)))
    q_dec = [x * e for x, e in zip(q_all, e_g_x)]
    k_end = [x * _expand_heads(e, SMALL_A, GDN_HEADS, GDN_HEAD_DIM) for x, e in zip(k_all, e_end)]

    x_in, y_diag, s_inc, c_mat, e_acs_x, e_blk_x = [], [], [], [], [], []
    for c in chunks:
        r = rows[c]
        x_in.append(xbc[r, :GROUP_WIDTH])
        x_dt = x_in[c] * _expand_heads(sp[r], SMALL_DT, SSM_HEADS, SSM_HEAD_DIM)
        x_end = x_dt * _expand_heads(e_end[c], SMALL_DT, SSM_HEADS, SSM_HEAD_DIM)
        e_acs_x.append(_expand_heads(e_acs[c], SMALL_DT, SSM_HEADS, SSM_HEAD_DIM))
        e_blk_x.append(_expand_heads(e_blk[c], SMALL_DT, SSM_HEADS, SSM_HEAD_DIM))
        l_m = decay_blocks(rate[r], SMALL_DT, SSM_HEADS)
        for g in range(SSM_GROUPS):
            b_g = xbc[r, GROUP_WIDTH + g * SSM_STATE:GROUP_WIDTH + (g + 1) * SSM_STATE]
            c_g = xbc[r, GROUP_WIDTH + (SSM_GROUPS + g) * SSM_STATE:GROUP_WIDTH + (SSM_GROUPS + g + 1) * SSM_STATE]
            b_bf = b_g.astype(BF16)
            cb = _dot_nt(c_g, jnp.concatenate([b_bf] * hg, axis=0))
            x_g = x_dt[:, g * gw:(g + 1) * gw]
            y_diag.append(_dot(cb * l_m[g], _block_diag(x_g.astype(BF16), lane_block, hg)))
            s_inc.append(_dot_tn(b_bf, x_end[:, g * gw:(g + 1) * gw]))
            c_mat.append(c_g)

    s_state = [s_ref[h] for h in range(GDN_HEADS)]
    h_state = [h_ref[g] for g in range(SSM_GROUPS)]
    zeros_v = jnp.zeros((CHUNK, GDN_HEAD_DIM), BF16)
    for c in chunks:
        r = rows[c]
        hd = 2 * GDN_HEAD_DIM
        v_new = [sol[c][:, h * hd:h * hd + GDN_HEAD_DIM]
                 - _dot(sol[c][:, h * hd + GDN_HEAD_DIM:(h + 1) * hd], s_state[h]) for h in range(GDN_HEADS)]
        v_rows = jnp.concatenate(
            [jnp.concatenate([v_new[h].astype(BF16) if h == g else zeros_v for h in range(GDN_HEADS)], axis=1)
             for g in range(GDN_HEADS)], axis=0)
        o_intra = _dot(qk[c], v_rows)
        o = [_dot(q_dec[c][:, head_cols(0, h)], s_state[h]) + o_intra[:, head_cols(0, h)]
             for h in range(GDN_HEADS)]
        s_state = [s_state[h] * e_blk[c][:, SMALL_A + h:SMALL_A + h + 1]
                   + _dot_tn(k_end[c][:, head_cols(0, h)], v_new[h]) for h in range(GDN_HEADS)]
        for h in range(GDN_HEADS):
            z = cur(OFF_CZ + h * GDN_HEAD_DIM, GDN_HEAD_DIM, r)
            y = _rmsnorm(o[h], gdn_norm_ref[...]) * _silu(z)
            out_ref[r, head_cols(2 * GROUP_WIDTH, h)] = y.astype(out_ref.dtype)

        ys = []
        for g in range(SSM_GROUPS):
            n = c * SSM_GROUPS + g
            ys.append(y_diag[n] + _dot(c_mat[n], h_state[g]) * e_acs_x[c][:, g * gw:(g + 1) * gw])
            h_state[g] = h_state[g] * e_blk_x[c][:, g * gw:(g + 1) * gw] + s_inc[n]
        y = jnp.concatenate(ys, axis=1) + ssm_d_ref[...] * x_in[c]
        y = y * _silu(cur(OFF_DZ, GROUP_WIDTH, r))
        for g in range(SSM_GROUPS):
            yn = _rmsnorm(y[:, g * gw:(g + 1) * gw], ssm_norm_ref[:, g * gw:(g + 1) * gw])
            out_ref[r, 3 * GROUP_WIDTH + g * gw:3 * GROUP_WIDTH + (g + 1) * gw] = yn.astype(out_ref.dtype)
    for h in range(GDN_HEADS):
        s_ref[h] = s_state[h]
    for g in range(SSM_GROUPS):
        h_ref[g] = h_state[g]


def _mixers(x, g_pre, w_head, w_tail, batch, seq, conv_a_w, pool_w, pool_scale, gdn_conv_w, gdn_norm_g, ssm_conv_w,
            ssm_conv_b, ssm_d_x, ssm_norm_g, alog_row, dtbias_row, tile, col_chunk):
    n_tiles = seq // tile
    d = x.shape[1]

    def const(shape):
        return pl.BlockSpec(shape, lambda b, i: (0,) * len(shape))

    return pl.pallas_call(
        functools.partial(_mixer_kernel, tile=tile, col_chunk=col_chunk),
        grid=(batch, n_tiles),
        in_specs=[
            pl.BlockSpec((tile, d), lambda b, i: (b * n_tiles + i, 0)),
            const((1, d)),
            pl.BlockSpec((d, HEAD_COLS), lambda b, i: (0, 0), pipeline_mode=pl.Buffered(1)),
            pl.BlockSpec((d, TAIL_COLS), lambda b, i: (0, 0), pipeline_mode=pl.Buffered(1)),
            const(conv_a_w.shape), const(pool_w.shape), const(pool_scale.shape), const(gdn_conv_w.shape),
            const(gdn_norm_g.shape), const(ssm_conv_w.shape), const(ssm_conv_b.shape), const(ssm_d_x.shape),
            const(ssm_norm_g.shape), const(alog_row.shape), const(dtbias_row.shape),
        ],
        out_specs=pl.BlockSpec((tile, D_MODEL), lambda b, i: (b * n_tiles + i, 0)),
        out_shape=jax.ShapeDtypeStruct((batch * seq, D_MODEL), BF16),
        scratch_shapes=[
            pltpu.VMEM((HALO + tile, U_COLS), F32),
            pltpu.VMEM((GDN_HEADS, GDN_HEAD_DIM, GDN_HEAD_DIM), F32),
            pltpu.VMEM((SSM_GROUPS, SSM_STATE, (SSM_HEADS // SSM_GROUPS) * SSM_HEAD_DIM), F32),
        ],
        compiler_params=pltpu.CompilerParams(
            dimension_semantics=("arbitrary", "arbitrary"), vmem_limit_bytes=VMEM_LIMIT_LARGE),
        name="mixers",
    )(x, g_pre.reshape(1, d), w_head, w_tail, conv_a_w, pool_w, pool_scale, gdn_conv_w, gdn_norm_g, ssm_conv_w,
      ssm_conv_b, ssm_d_x, ssm_norm_g, alog_row, dtbias_row)


def _xattn_kernel(x_ref, g_ref, wq_ref, kv_ref, o_ref, wb_ref):
    @pl.when((pl.program_id(0) == 0) & (pl.program_id(1) == 0))
    def _():
        wb_ref[...] = wq_ref[...].astype(BF16)

    xn = _rmsnorm(x_ref[...], g_ref[...]).astype(BF16)
    q = jnp.dot(xn, wb_ref[...], preferred_element_type=F32)
    for h in range(XA_HEADS):
        lo = h * XA_HEAD_DIM
        k_h = kv_ref[:, lo:lo + XA_HEAD_DIM]
        v_h = kv_ref[:, D_MODEL + lo:D_MODEL + lo + XA_HEAD_DIM]
        s = _dot_nt(q[:, lo:lo + XA_HEAD_DIM], k_h) * XA_HEAD_DIM ** -0.5
        e = jnp.exp(s - jnp.max(s, axis=-1, keepdims=True))
        p = e / jnp.sum(e, axis=-1, keepdims=True)
        o_ref[:, lo:lo + XA_HEAD_DIM] = _dot(p, v_h).astype(o_ref.dtype)


def _xattn(x, g, wq, layer, kv, batch, seq, tq):
    n, d = x.shape
    n_tiles = seq // tq
    return pl.pallas_call(
        _xattn_kernel,
        grid=(batch, n_tiles),
        in_specs=[
            pl.BlockSpec((tq, d), lambda b, i: (b * n_tiles + i, 0)),
            pl.BlockSpec((1, d), lambda b, i: (0, 0)),
            pl.BlockSpec((None, d, d), lambda b, i: (layer, 0, 0), pipeline_mode=pl.Buffered(1)),
            pl.BlockSpec((MEM_LEN, 2 * d), lambda b, i: (b, 0)),
        ],
        out_specs=pl.BlockSpec((tq, d), lambda b, i: (b * n_tiles + i, 0)),
        out_shape=jax.ShapeDtypeStruct((n, d), BF16),
        scratch_shapes=[pltpu.VMEM((d, d), BF16)],
        compiler_params=pltpu.CompilerParams(
            dimension_semantics=("arbitrary", "arbitrary"), vmem_limit_bytes=VMEM_LIMIT),
        name="xattn",
    )(x, g.reshape(1, d), wq, kv)


def _ffn_kernel(x_ref, gpre_ref, wg_ref, wu_ref, wd_ref, gpost_ref, o_ref, xn_ref, *, edge_chunks):
    j = pl.program_id(1)
    last = pl.num_programs(1) - 1
    wg = wg_ref[...].astype(BF16)
    wu = wu_ref[...].astype(BF16)
    wd = wd_ref[...].astype(BF16)

    def partial_out(rows):
        xn = xn_ref[rows, :]
        gate = jnp.dot(xn, wg, preferred_element_type=F32)
        up = jnp.dot(xn, wu, preferred_element_type=F32)
        return jnp.dot((_silu(gate) * up).astype(BF16), wd, preferred_element_type=F32)

    tm = o_ref.shape[0]
    chunks = [slice(r, r + tm // edge_chunks) for r in range(0, tm, tm // edge_chunks)]

    @pl.when(j == 0)
    def _():
        for rows in chunks:
            xn_ref[rows, :] = _rmsnorm(x_ref[rows, :], gpre_ref[...]).astype(BF16)
            o_ref[rows, :] = partial_out(rows)

    @pl.when((j > 0) & (j < last))
    def _():
        o_ref[...] += partial_out(slice(0, tm))

    @pl.when(j == last)
    def _():
        for rows in chunks:
            y = o_ref[rows, :] + partial_out(rows)
            o_ref[rows, :] = x_ref[rows, :] + _rmsnorm(y, gpost_ref[...])


def _ffn(x, gpre, w_gu, w_down, layer, gpost, tm, tf, edge_chunks):
    n, d = x.shape
    n_ff = D_FF // tf
    return pl.pallas_call(
        functools.partial(_ffn_kernel, edge_chunks=edge_chunks),
        grid=(n // tm, n_ff),
        in_specs=[
            pl.BlockSpec((tm, d), lambda i, j: (i, 0), pipeline_mode=pl.Buffered(1)),
            pl.BlockSpec((1, d), lambda i, j: (0, 0)),
            pl.BlockSpec((None, d, tf), lambda i, j: (layer, 0, j)),
            pl.BlockSpec((None, d, tf), lambda i, j: (layer, 0, j + n_ff)),
            pl.BlockSpec((None, tf, d), lambda i, j: (layer, j, 0)),
            pl.BlockSpec((1, d), lambda i, j: (0, 0)),
        ],
        out_specs=pl.BlockSpec((tm, d), lambda i, j: (i, 0), pipeline_mode=pl.Buffered(1)),
        out_shape=jax.ShapeDtypeStruct((n, d), F32),
        scratch_shapes=[pltpu.VMEM((tm, d), BF16)],
        compiler_params=pltpu.CompilerParams(
            dimension_semantics=("arbitrary", "arbitrary"), vmem_limit_bytes=VMEM_LIMIT_FFN),
        name="ffn",
    )(x, gpre.reshape(1, d), w_gu, w_gu, w_down, gpost.reshape(1, d))


def _small_row(gdn_vals, ssm_vals):
    row = jnp.zeros((1, 128), F32)
    row = row.at[0, SMALL_A:SMALL_A + GDN_HEADS].set(gdn_vals.astype(F32))
    return row.at[0, SMALL_DT:SMALL_DT + SSM_HEADS].set(ssm_vals.astype(F32))


def kernel(x, mem, norm_g, w_in, conv_a_w, pool_w, pool_scale, gdn_conv_w, gdn_A_log, gdn_dt_bias, gdn_norm_g,
           ssm_conv_w, ssm_conv_b, ssm_A_log, ssm_dt_bias, ssm_D, ssm_norm_g, w_out, xa_wq, xa_wkv, xa_wo,
           ffn_w_gu, ffn_w_down):
    batch, seq, d = x.shape
    depth = w_in.shape[0]
    xf = x.reshape(batch * seq, d)
    memf = mem.reshape(batch * MEM_LEN, d)
    w_in_t = jnp.swapaxes(w_in, 1, 2)
    for l in range(depth):
        g = norm_g[l]
        mixed = _mixers(
            xf, g[0], *_prep_w_in(w_in_t, l), batch, seq, conv_a_w[l], pool_w[l], pool_scale[l].reshape(1, -1),
            gdn_conv_w[l], gdn_norm_g[l].reshape(1, -1), ssm_conv_w[l], ssm_conv_b[l].reshape(1, -1),
            jnp.repeat(ssm_D[l], SSM_HEAD_DIM).reshape(1, -1), ssm_norm_g[l].reshape(1, -1),
            _small_row(gdn_A_log[l], ssm_A_log[l]), _small_row(gdn_dt_bias[l], ssm_dt_bias[l]),
            tile=4 * CHUNK, col_chunk=1024)
        xf = _proj_post(mixed, w_out, l, g[1], xf, tm=512, row_chunks=2 if l == 0 else 1)
        kv = _norm_matmul(memf, g[4], xa_wkv, l, 2 * d, BF16, tm=batch * MEM_LEN, tn=512)
        att = _xattn(xf, g[2], xa_wq, l, kv, batch, seq, tq=512)
        xf = _proj_post(att, xa_wo, l, g[3], xf, tm=512, row_chunks=2 if l == 0 else 1)
        xf = _ffn(xf, g[5], ffn_w_gu, ffn_w_down, l, g[6], tm=1024, tf=512, edge_chunks=4 if l == 0 else 1)
    return xf.reshape(batch, seq, d)
```

```python
import functools

import jax
import jax.numpy as jnp
from jax import lax
from jax.experimental import pallas as pl
from jax.experimental.pallas import tpu as pltpu

F32 = jnp.float32
BF16 = jnp.bfloat16
EPS = 1e-6

D_MODEL = 2048
GROUP_WIDTH = 512
CHUNK = 64
POOL_WINDOWS = (2, 4, 8, 16)
POOL_GROUP = 128
GDN_HEADS = 4
GDN_HEAD_DIM = 128
SSM_HEADS = 8
SSM_HEAD_DIM = 64
SSM_GROUPS = 2
SSM_STATE = 128
SSM_XBC = 1024
MEM_LEN = 256
XA_HEADS = 4
XA_HEAD_DIM = 512
D_FF = 5632

IN_COLS = 5648
HEAD_COLS = 4096
TAIL_START = 4096
TAIL_VALID = IN_COLS - TAIL_START
TAIL_COLS = 1664
TAIL_BLOCK = 2048
TAIL_SKEW = 8
U_COLS = HEAD_COLS + TAIL_COLS
OFF_A = 0
OFF_B = 1536
OFF_QKV = 2048
OFF_CZ = 3584
OFF_DZ = 4096
OFF_XBC = 4608
OFF_SMALL = 5632
SMALL_DT = 0
SMALL_A = 120
SMALL_B = 124
HALO = 16

MIX_TILE = 4 * CHUNK
MIX_COL_CHUNK = 1024
ROW_TILE = 512
KV_COL_TILE = 512
FFN_ROW_TILE = 1024
FFN_COL_TILE = 512

VMEM_LIMIT = 48 * 1024 * 1024
VMEM_LIMIT_LARGE = 56 * 1024 * 1024
VMEM_LIMIT_FFN = 60000 * 1024


def _rmsnorm(x, g):
    return x * lax.rsqrt(jnp.mean(x * x, axis=-1, keepdims=True) + EPS) * g


def _silu(x):
    return x * jax.nn.sigmoid(x)


def _softplus(x):
    return jnp.maximum(x, 0.0) + jnp.log1p(jnp.exp(-jnp.abs(x)))


def _dot(a, b):
    return jnp.dot(a.astype(BF16), b.astype(BF16), preferred_element_type=F32)


def _dot_nt(a, b):
    return lax.dot_general(a.astype(BF16), b.astype(BF16), (((1,), (1,)), ((), ())),
                           preferred_element_type=F32)


def _dot_tn(a, b):
    return lax.dot_general(a.astype(BF16), b.astype(BF16), (((0,), (0,)), ((), ())),
                           preferred_element_type=F32)


def _split_bf16(a):
    hi = a.astype(BF16)
    return hi, (a - hi.astype(F32)).astype(BF16)


def _dot3(a, b):
    return (jnp.dot(a[0], b[0], preferred_element_type=F32)
            + jnp.dot(a[0], b[1], preferred_element_type=F32)
            + jnp.dot(a[1], b[0], preferred_element_type=F32))


def _norm_matmul_kernel(x_ref, g_ref, w_ref, o_ref, xn_ref):
    @pl.when(pl.program_id(1) == 0)
    def _():
        xn_ref[...] = _rmsnorm(x_ref[...], g_ref[...]).astype(BF16)

    o_ref[...] = jnp.dot(xn_ref[...], w_ref[...].astype(BF16), preferred_element_type=F32).astype(o_ref.dtype)


def _norm_matmul(x, g, w, layer, n_cols, out_dtype, tm, tn):
    n, k = x.shape
    return pl.pallas_call(
        _norm_matmul_kernel,
        grid=(n // tm, n_cols // tn),
        in_specs=[
            pl.BlockSpec((tm, k), lambda i, j: (i, 0)),
            pl.BlockSpec((1, k), lambda i, j: (0, 0)),
            pl.BlockSpec((None, k, tn), lambda i, j: (layer, 0, j)),
        ],
        out_specs=pl.BlockSpec((tm, tn), lambda i, j: (i, j)),
        out_shape=jax.ShapeDtypeStruct((n, n_cols), out_dtype),
        scratch_shapes=[pltpu.VMEM((tm, k), BF16)],
        compiler_params=pltpu.CompilerParams(
            dimension_semantics=("arbitrary", "arbitrary"), vmem_limit_bytes=VMEM_LIMIT_LARGE),
        name="norm_matmul",
    )(x, g.reshape(1, k), w)


PREP_ROWS = 512


def _prep_head_kernel(wt_ref, o_ref):
    o_ref[...] = wt_ref[...].T.astype(BF16)


def _prep_tail_kernel(wt_ref, o_ref):
    dt0 = TAIL_VALID - TAIL_SKEW
    body = TAIL_COLS - 128
    for r in range(0, body, PREP_ROWS):
        o_ref[:, r:r + PREP_ROWS] = wt_ref[TAIL_SKEW + r:TAIL_SKEW + r + PREP_ROWS, :].T.astype(BF16)
    last = jnp.concatenate([wt_ref[dt0:TAIL_VALID, :],
                            jnp.zeros((128 - 2 * TAIL_SKEW, wt_ref.shape[1]), F32),
                            wt_ref[0:TAIL_SKEW, :]], axis=0)
    o_ref[:, body:] = last.T.astype(BF16)


def _prep_w_in(wt, layer):
    k = wt.shape[2]
    head = pl.pallas_call(
        _prep_head_kernel,
        grid=(HEAD_COLS // PREP_ROWS,),
        in_specs=[pl.BlockSpec((None, PREP_ROWS, k), lambda j: (layer, j, 0))],
        out_specs=pl.BlockSpec((k, PREP_ROWS), lambda j: (0, j)),
        out_shape=jax.ShapeDtypeStruct((k, HEAD_COLS), BF16),
        compiler_params=pltpu.CompilerParams(dimension_semantics=("arbitrary",), vmem_limit_bytes=VMEM_LIMIT),
        name="prep_w_in_head",
    )(wt)
    tail = pl.pallas_call(
        _prep_tail_kernel,
        grid=(1,),
        in_specs=[pl.BlockSpec((None, TAIL_BLOCK, k), lambda j: (layer, TAIL_START // TAIL_BLOCK, 0),
                               pipeline_mode=pl.Buffered(1))],
        out_specs=pl.BlockSpec((k, TAIL_COLS), lambda j: (0, 0)),
        out_shape=jax.ShapeDtypeStruct((k, TAIL_COLS), BF16),
        compiler_params=pltpu.CompilerParams(dimension_semantics=("arbitrary",), vmem_limit_bytes=VMEM_LIMIT_LARGE),
        name="prep_w_in_tail",
    )(wt)
    return head, tail


def _proj_post_kernel(a_ref, w_ref, g_ref, x_ref, o_ref, wb_ref):
    @pl.when(pl.program_id(0) == 0)
    def _():
        wb_ref[...] = w_ref[...].astype(BF16)

    y = jnp.dot(a_ref[...], wb_ref[...], preferred_element_type=F32)
    o_ref[...] = x_ref[...] + _rmsnorm(y, g_ref[...])


def _proj_post(a, w, layer, g, x, tm):
    n, k = a.shape
    d = w.shape[2]
    return pl.pallas_call(
        _proj_post_kernel,
        grid=(n // tm,),
        in_specs=[
            pl.BlockSpec((tm, k), lambda i: (i, 0)),
            pl.BlockSpec((None, k, d), lambda i: (layer, 0, 0), pipeline_mode=pl.Buffered(1)),
            pl.BlockSpec((1, d), lambda i: (0, 0)),
            pl.BlockSpec((tm, d), lambda i: (i, 0)),
        ],
        out_specs=pl.BlockSpec((tm, d), lambda i: (i, 0)),
        out_shape=jax.ShapeDtypeStruct((n, d), F32),
        scratch_shapes=[pltpu.VMEM((k, d), BF16)],
        compiler_params=pltpu.CompilerParams(
            dimension_semantics=("arbitrary",), vmem_limit_bytes=VMEM_LIMIT_LARGE),
        name="proj_post",
    )(a, w, g.reshape(1, d), x)


def _shift_rows(x, k):
    return pltpu.roll(x, k, axis=0) if k else x


def _causal_conv(x_ext, w_ref):
    width = w_ref.shape[0]
    if width == 4:
        x1 = _shift_rows(x_ext, 1)
        older = _shift_rows(x_ext * w_ref[1:2, :] + x1 * w_ref[0:1, :], 2)
        return x_ext * w_ref[3:4, :] + x1 * w_ref[2:3, :] + older
    acc = x_ext * w_ref[width - 1:width, :]
    for k in range(width - 1):
        acc = acc + _shift_rows(x_ext, width - 1 - k) * w_ref[k:k + 1, :]
    return acc


def _expand_heads(cols, first, n_heads, width):
    rows = cols.shape[0]
    per_vreg = 128 // width
    lane = lax.broadcasted_iota(jnp.int32, (rows, 128), 1)
    pieces = []
    for m in range(n_heads // per_vreg):
        piece = jnp.broadcast_to(cols[:, first + m * per_vreg:first + m * per_vreg + 1], (rows, 128))
        for j in range(1, per_vreg):
            c = first + m * per_vreg + j
            piece = jnp.where(lane >= j * width, jnp.broadcast_to(cols[:, c:c + 1], (rows, 128)), piece)
        pieces.append(piece)
    return jnp.concatenate(pieces, axis=1)


def _block_diag(y, lane_block, n_blocks):
    return jnp.concatenate([jnp.where(lane_block == b, y, jnp.zeros_like(y)) for b in range(n_blocks)], axis=0)


def _split_block_diag(y, lane_block, n_blocks):
    hi, lo = _split_bf16(y)
    return _block_diag(hi, lane_block, n_blocks), _block_diag(lo, lane_block, n_blocks)


def _unit_lower_inverse(ms, row, col, lane_block, n_blocks):
    same16 = (row >> 4) == (col >> 4)
    same32 = (row >> 5) == (col >> 5)
    off16 = same32 & jnp.logical_not(same16)
    eye = (row == col).astype(F32)
    split = lambda xs: [_split_bf16(x) for x in xs]
    split_bd = lambda xs: [_split_block_diag(x, lane_block, n_blocks) for x in xs]
    d = [jnp.where(same16, m, 0.0) for m in ms]
    d2 = [_dot3(x, y) for x, y in zip(split(d), split_bd(d))]
    p = [eye - x for x in d]
    d2_bd = split_bd(d2)
    p = [x + _dot3(xs, y) for x, xs, y in zip(p, split(p), d2_bd)]
    d4 = [_dot3(x, y) for x, y in zip(split(d2), d2_bd)]
    d4_bd = split_bd(d4)
    p = [x + _dot3(xs, y) for x, xs, y in zip(p, split(p), d4_bd)]
    d8_bd = split_bd([_dot3(x, y) for x, y in zip(split(d4), d4_bd)])
    p = [x + _dot3(xs, y) for x, xs, y in zip(p, split(p), d8_bd)]
    ps, p_bd = split(p), split_bd(p)
    t = split([_dot3(xs, _split_block_diag(jnp.where(off16, m, 0.0), lane_block, n_blocks)) for xs, m in zip(ps, ms)])
    q = [x - _dot3(y, xb) for x, xb, y in zip(p, p_bd, t)]
    qs, q_bd = split(q), split_bd(q)
    t = split([_dot3(xs, _split_block_diag(jnp.where(same32, 0.0, m), lane_block, n_blocks)) for xs, m in zip(qs, ms)])
    return [x - _dot3(y, xb) for x, xb, y in zip(q, q_bd, t)]


def _dot_ltri(ltri, x):
    x1 = x.astype(BF16)
    r1 = x - x1.astype(F32)
    x2 = r1.astype(BF16)
    x3 = (r1 - x2.astype(F32)).astype(BF16)
    return (jnp.dot(ltri, x1, preferred_element_type=F32) + jnp.dot(ltri, x2, preferred_element_type=F32)
            + jnp.dot(ltri, x3, preferred_element_type=F32))


def _mixer_kernel(x_ref, gpre_ref, w_head_ref, w_tail_ref, conv_a_ref, pool_w_ref, pool_scale_ref, gdn_conv_ref,
                  gdn_norm_ref, ssm_conv_ref, ssm_bias_ref, ssm_d_ref, ssm_norm_ref,
                  alog_ref, dtbias_ref, out_ref, u_ref, s_ref, h_ref, *, tile):
    i = pl.program_id(1)
    n_chunks = tile // CHUNK
    hg = SSM_HEADS // SSM_GROUPS
    gw = hg * SSM_HEAD_DIM

    @pl.when(i == 0)
    def _():
        s_ref[...] = jnp.zeros_like(s_ref)
        h_ref[...] = jnp.zeros_like(h_ref)
        u_ref[0:HALO, :] = jnp.zeros((HALO, U_COLS), F32)

    @pl.when(i > 0)
    def _():
        u_ref[0:HALO, :] = u_ref[tile:tile + HALO, :]

    xn = _rmsnorm(x_ref[...], gpre_ref[...]).astype(BF16)
    for c0 in range(0, HEAD_COLS, MIX_COL_CHUNK):
        u_ref[HALO:, c0:c0 + MIX_COL_CHUNK] = jnp.dot(xn, w_head_ref[:, c0:c0 + MIX_COL_CHUNK],
                                                        preferred_element_type=F32)
    u_ref[HALO:, HEAD_COLS:] = jnp.dot(xn, w_tail_ref[...], preferred_element_type=F32)

    def cur(off, width, rows=slice(0, tile)):
        return u_ref[HALO + rows.start:HALO + rows.stop, off:off + width]

    def ext(off, width):
        return u_ref[:, off:off + width]

    p = ext(OFF_A + GROUP_WIDTH, GROUP_WIDTH) * ext(OFF_A + 2 * GROUP_WIDTH, GROUP_WIDTH)
    ya = cur(OFF_A, GROUP_WIDTH) * _causal_conv(p, conv_a_ref)[HALO:]
    out_ref[:, 0:GROUP_WIDTH] = ya.astype(out_ref.dtype)

    pos = (i * tile + 1 + lax.broadcasted_iota(jnp.int32, (tile, 1), 0)).astype(F32)
    for gi, window in enumerate(POOL_WINDOWS):
        off = OFF_B + gi * POOL_GROUP
        s = ext(off, POOL_GROUP)
        k = 1
        while k < window:
            s = s + _shift_rows(s, k)
            k *= 2
        pooled = s[HALO:] / jnp.minimum(pos, float(window)) - cur(off, POOL_GROUP)
        yb = _dot(pooled, pool_w_ref[gi]) * pool_scale_ref[:, gi * POOL_GROUP:(gi + 1) * POOL_GROUP]
        out_ref[:, GROUP_WIDTH + gi * POOL_GROUP:GROUP_WIDTH + (gi + 1) * POOL_GROUP] = yb.astype(out_ref.dtype)

    small = cur(OFF_SMALL, 128)
    sp = _softplus(small + dtbias_ref[...])
    rate = -jnp.exp(alog_ref[...]) * sp
    beta_all = jax.nn.sigmoid(small)

    qkv = _silu(_causal_conv(ext(OFF_QKV, 3 * GROUP_WIDTH), gdn_conv_ref)[HALO:])
    xbc = _silu(_causal_conv(ext(OFF_XBC, SSM_XBC), ssm_conv_ref)[HALO:] + ssm_bias_ref[...])

    n4 = GDN_HEADS * CHUNK
    row = lax.broadcasted_iota(jnp.int32, (CHUNK, n4), 0)
    lane = lax.broadcasted_iota(jnp.int32, (CHUNK, n4), 1)
    col = lane & (CHUNK - 1)
    lane_block = lane >> 6
    incl = row >= col
    strict = row > col
    ltri = (lax.broadcasted_iota(jnp.int32, (CHUNK, CHUNK), 0)
            >= lax.broadcasted_iota(jnp.int32, (CHUNK, CHUNK), 1)).astype(BF16)

    def decay_blocks(rate_c, first, n_heads):
        out = []
        for m in range(n_heads // GDN_HEADS):
            r = _expand_heads(rate_c, first + m * GDN_HEADS, GDN_HEADS, CHUNK)
            diff = _dot_ltri(ltri, jnp.where(strict, r, 0.0))
            out.append(jnp.where(incl, jnp.exp(jnp.where(incl, diff, 0.0)), 0.0))
        return out

    chunks = range(n_chunks)
    rows = [slice(c * CHUNK, (c + 1) * CHUNK) for c in chunks]
    acs = [_dot_ltri(ltri, rate[r]) for r in rows]
    e_acs = [jnp.exp(a) for a in acs]
    e_end = [jnp.exp(a[CHUNK - 1:CHUNK, :] - a) for a in acs]
    e_blk = [jnp.exp(a[CHUNK - 1:CHUNK, :]) for a in acs]

    def head_cols(base, h):
        return slice(base + h * GDN_HEAD_DIM, base + (h + 1) * GDN_HEAD_DIM)

    def l2norm_heads(x, scale):
        return jnp.concatenate(
            [x[:, head_cols(0, h)] * (lax.rsqrt(jnp.sum(x[:, head_cols(0, h)] * x[:, head_cols(0, h)],
                                                        axis=-1, keepdims=True) + EPS) * scale)
             for h in range(GDN_HEADS)], axis=1)

    zeros_head = jnp.zeros((CHUNK, GDN_HEAD_DIM), BF16)
    q_all = [l2norm_heads(qkv[r, 0:GROUP_WIDTH], GDN_HEAD_DIM ** -0.5) for r in rows]
    k_all = [l2norm_heads(qkv[r, GROUP_WIDTH:2 * GROUP_WIDTH], 1.0) for r in rows]
    v_all = [qkv[r, 2 * GROUP_WIDTH:3 * GROUP_WIDTH] for r in rows]
    beta_x = [_expand_heads(beta_all[r], SMALL_B, GDN_HEADS, GDN_HEAD_DIM) for r in rows]
    e_g_x = [_expand_heads(e, SMALL_A, GDN_HEADS, GDN_HEAD_DIM) for e in e_acs]
    kb = [x * b for x, b in zip(k_all, beta_x)]
    k_rows = []
    for x in k_all:
        xb = x.astype(BF16)
        k_rows.append(jnp.concatenate(
            [jnp.concatenate([xb[:, head_cols(0, h)] if h == g else zeros_head for h in range(GDN_HEADS)], axis=1)
             for g in range(GDN_HEADS)], axis=0))
    decay = [decay_blocks(rate[r], SMALL_A, GDN_HEADS)[0] for r in rows]
    m_low = [jnp.where(strict, _dot_nt(x, y) * dc, 0.0) for x, y, dc in zip(kb, k_rows, decay)]
    qk = [_dot_nt(x, y) * dc for x, y, dc in zip(q_all, k_rows, decay)]
    t_inv = _unit_lower_inverse(m_low, row, col, lane_block, GDN_HEADS)
    zeros_rhs = jnp.zeros((CHUNK, 2 * GDN_HEAD_DIM), F32)
    sol = []
    for c in chunks:
        vb, kg = v_all[c] * beta_x[c], kb[c] * e_g_x[c]
        rhs = jnp.concatenate(
            [jnp.concatenate([jnp.concatenate([vb[:, head_cols(0, h)], kg[:, head_cols(0, h)]], axis=1)
                              if h == g else zeros_rhs for h in range(GDN_HEADS)], axis=1)
             for g in range(GDN_HEADS)], axis=0)
        sol.append(_dot3(_split_bf16(t_inv[c]), _split_bf16(rhs)))
    q_dec = [x * e for x, e in zip(q_all, e_g_x)]
    k_end = [x * _expand_heads(e, SMALL_A, GDN_HEADS, GDN_HEAD_DIM) for x, e in zip(k_all, e_end)]

    x_in, y_diag, s_inc, c_mat, e_acs_x, e_blk_x = [], [], [], [], [], []
    for c in chunks:
        r = rows[c]
        x_in.append(xbc[r, :GROUP_WIDTH])
        x_dt = x_in[c] * _expand_heads(sp[r], SMALL_DT, SSM_HEADS, SSM_HEAD_DIM)
        x_end = x_dt * _expand_heads(e_end[c], SMALL_DT, SSM_HEADS, SSM_HEAD_DIM)
        e_acs_x.append(_expand_heads(e_acs[c], SMALL_DT, SSM_HEADS, SSM_HEAD_DIM))
        e_blk_x.append(_expand_heads(e_blk[c], SMALL_DT, SSM_HEADS, SSM_HEAD_DIM))
        l_m = decay_blocks(rate[r], SMALL_DT, SSM_HEADS)
        for g in range(SSM_GROUPS):
            b_g = xbc[r, GROUP_WIDTH + g * SSM_STATE:GROUP_WIDTH + (g + 1) * SSM_STATE]
            c_g = xbc[r, GROUP_WIDTH + (SSM_GROUPS + g) * SSM_STATE:GROUP_WIDTH + (SSM_GROUPS + g + 1) * SSM_STATE]
            b_bf = b_g.astype(BF16)
            cb = _dot_nt(c_g, jnp.concatenate([b_bf] * hg, axis=0))
            x_g = x_dt[:, g * gw:(g + 1) * gw]
            y_diag.append(_dot(cb * l_m[g], _block_diag(x_g.astype(BF16), lane_block, hg)))
            s_inc.append(_dot_tn(b_bf, x_end[:, g * gw:(g + 1) * gw]))
            c_mat.append(c_g)

    s_state = [s_ref[h] for h in range(GDN_HEADS)]
    h_state = [h_ref[g] for g in range(SSM_GROUPS)]
    zeros_v = jnp.zeros((CHUNK, GDN_HEAD_DIM), BF16)
    for c in chunks:
        r = rows[c]
        hd = 2 * GDN_HEAD_DIM
        v_new = [sol[c][:, h * hd:h * hd + GDN_HEAD_DIM]
                 - _dot(sol[c][:, h * hd + GDN_HEAD_DIM:(h + 1) * hd], s_state[h]) for h in range(GDN_HEADS)]
        v_rows = jnp.concatenate(
            [jnp.concatenate([v_new[h].astype(BF16) if h == g else zeros_v for h in range(GDN_HEADS)], axis=1)
             for g in range(GDN_HEADS)], axis=0)
        o_intra = _dot(qk[c], v_rows)
        o = [_dot(q_dec[c][:, head_cols(0, h)], s_state[h]) + o_intra[:, head_cols(0, h)]
             for h in range(GDN_HEADS)]
        s_state = [s_state[h] * e_blk[c][:, SMALL_A + h:SMALL_A + h + 1]
                   + _dot_tn(k_end[c][:, head_cols(0, h)], v_new[h]) for h in range(GDN_HEADS)]
        for h in range(GDN_HEADS):
            z = cur(OFF_CZ + h * GDN_HEAD_DIM, GDN_HEAD_DIM, r)
            y = _rmsnorm(o[h], gdn_norm_ref[...]) * _silu(z)
            out_ref[r, head_cols(2 * GROUP_WIDTH, h)] = y.astype(out_ref.dtype)

        ys = []
        for g in range(SSM_GROUPS):
            n = c * SSM_GROUPS + g
            ys.append(y_diag[n] + _dot(c_mat[n], h_state[g]) * e_acs_x[c][:, g * gw:(g + 1) * gw])
            h_state[g] = h_state[g] * e_blk_x[c][:, g * gw:(g + 1) * gw] + s_inc[n]
        y = jnp.concatenate(ys, axis=1) + ssm_d_ref[...] * x_in[c]
        y = y * _silu(cur(OFF_DZ, GROUP_WIDTH, r))
        for g in range(SSM_GROUPS):
            yn = _rmsnorm(y[:, g * gw:(g + 1) * gw], ssm_norm_ref[:, g * gw:(g + 1) * gw])
            out_ref[r, 3 * GROUP_WIDTH + g * gw:3 * GROUP_WIDTH + (g + 1) * gw] = yn.astype(out_ref.dtype)
    for h in range(GDN_HEADS):
        s_ref[h] = s_state[h]
    for g in range(SSM_GROUPS):
        h_ref[g] = h_state[g]


def _mixers(x, g_pre, w_head, w_tail, batch, seq, conv_a_w, pool_w, pool_scale, gdn_conv_w, gdn_norm_g, ssm_conv_w,
            ssm_conv_b, ssm_d_x, ssm_norm_g, alog_row, dtbias_row, tile):
    n_tiles = seq // tile
    d = x.shape[1]

    def const(shape):
        return pl.BlockSpec(shape, lambda b, i: (0,) * len(shape))

    return pl.pallas_call(
        functools.partial(_mixer_kernel, tile=tile),
        grid=(batch, n_tiles),
        in_specs=[
            pl.BlockSpec((tile, d), lambda b, i: (b * n_tiles + i, 0)),
            const((1, d)),
            pl.BlockSpec((d, HEAD_COLS), lambda b, i: (0, 0), pipeline_mode=pl.Buffered(1)),
            pl.BlockSpec((d, TAIL_COLS), lambda b, i: (0, 0), pipeline_mode=pl.Buffered(1)),
            const(conv_a_w.shape), const(pool_w.shape), const(pool_scale.shape), const(gdn_conv_w.shape),
            const(gdn_norm_g.shape), const(ssm_conv_w.shape), const(ssm_conv_b.shape), const(ssm_d_x.shape),
            const(ssm_norm_g.shape), const(alog_row.shape), const(dtbias_row.shape),
        ],
        out_specs=pl.BlockSpec((tile, D_MODEL), lambda b, i: (b * n_tiles + i, 0)),
        out_shape=jax.ShapeDtypeStruct((batch * seq, D_MODEL), BF16),
        scratch_shapes=[
            pltpu.VMEM((HALO + tile, U_COLS), F32),
            pltpu.VMEM((GDN_HEADS, GDN_HEAD_DIM, GDN_HEAD_DIM), F32),
            pltpu.VMEM((SSM_GROUPS, SSM_STATE, (SSM_HEADS // SSM_GROUPS) * SSM_HEAD_DIM), F32),
        ],
        compiler_params=pltpu.CompilerParams(
            dimension_semantics=("arbitrary", "arbitrary"), vmem_limit_bytes=VMEM_LIMIT_LARGE),
        name="mixers",
    )(x, g_pre.reshape(1, d), w_head, w_tail, conv_a_w, pool_w, pool_scale, gdn_conv_w, gdn_norm_g, ssm_conv_w,
      ssm_conv_b, ssm_d_x, ssm_norm_g, alog_row, dtbias_row)


def _xattn_kernel(x_ref, g_ref, wq_ref, kv_ref, o_ref, wb_ref):
    @pl.when((pl.program_id(0) == 0) & (pl.program_id(1) == 0))
    def _():
        wb_ref[...] = wq_ref[...].astype(BF16)

    xn = _rmsnorm(x_ref[...], g_ref[...]).astype(BF16)
    q = jnp.dot(xn, wb_ref[...], preferred_element_type=F32)
    for h in range(XA_HEADS):
        lo = h * XA_HEAD_DIM
        k_h = kv_ref[:, lo:lo + XA_HEAD_DIM]
        v_h = kv_ref[:, D_MODEL + lo:D_MODEL + lo + XA_HEAD_DIM]
        s = _dot_nt(q[:, lo:lo + XA_HEAD_DIM], k_h) * XA_HEAD_DIM ** -0.5
        e = jnp.exp(s - jnp.max(s, axis=-1, keepdims=True))
        p = e / jnp.sum(e, axis=-1, keepdims=True)
        o_ref[:, lo:lo + XA_HEAD_DIM] = _dot(p, v_h).astype(o_ref.dtype)


def _xattn(x, g, wq, layer, kv, batch, seq, tq):
    n, d = x.shape
    n_tiles = seq // tq
    return pl.pallas_call(
        _xattn_kernel,
        grid=(batch, n_tiles),
        in_specs=[
            pl.BlockSpec((tq, d), lambda b, i: (b * n_tiles + i, 0)),
            pl.BlockSpec((1, d), lambda b, i: (0, 0)),
            pl.BlockSpec((None, d, d), lambda b, i: (layer, 0, 0), pipeline_mode=pl.Buffered(1)),
            pl.BlockSpec((MEM_LEN, 2 * d), lambda b, i: (b, 0)),
        ],
        out_specs=pl.BlockSpec((tq, d), lambda b, i: (b * n_tiles + i, 0)),
        out_shape=jax.ShapeDtypeStruct((n, d), BF16),
        scratch_shapes=[pltpu.VMEM((d, d), BF16)],
        compiler_params=pltpu.CompilerParams(
            dimension_semantics=("arbitrary", "arbitrary"), vmem_limit_bytes=VMEM_LIMIT),
        name="xattn",
    )(x, g.reshape(1, d), wq, kv)


def _ffn_kernel(x_ref, gpre_ref, wg_ref, wu_ref, wd_ref, gpost_ref, o_ref, xn_ref):
    j = pl.program_id(1)

    @pl.when(j == 0)
    def _():
        xn_ref[...] = _rmsnorm(x_ref[...], gpre_ref[...]).astype(BF16)
        o_ref[...] = jnp.zeros_like(o_ref)

    xn = xn_ref[...]
    gate = jnp.dot(xn, wg_ref[...].astype(BF16), preferred_element_type=F32)
    up = jnp.dot(xn, wu_ref[...].astype(BF16), preferred_element_type=F32)
    o_ref[...] += jnp.dot((_silu(gate) * up).astype(BF16), wd_ref[...].astype(BF16), preferred_element_type=F32)

    @pl.when(j == pl.num_programs(1) - 1)
    def _():
        o_ref[...] = x_ref[...] + _rmsnorm(o_ref[...], gpost_ref[...])


def _ffn(x, gpre, w_gu, w_down, layer, gpost, tm, tf):
    n, d = x.shape
    n_ff = D_FF // tf
    return pl.pallas_call(
        _ffn_kernel,
        grid=(n // tm, n_ff),
        in_specs=[
            pl.BlockSpec((tm, d), lambda i, j: (i, 0), pipeline_mode=pl.Buffered(1)),
            pl.BlockSpec((1, d), lambda i, j: (0, 0)),
            pl.BlockSpec((None, d, tf), lambda i, j: (layer, 0, j)),
            pl.BlockSpec((None, d, tf), lambda i, j: (layer, 0, j + n_ff)),
            pl.BlockSpec((None, tf, d), lambda i, j: (layer, j, 0)),
            pl.BlockSpec((1, d), lambda i, j: (0, 0)),
        ],
        out_specs=pl.BlockSpec((tm, d), lambda i, j: (i, 0), pipeline_mode=pl.Buffered(1)),
        out_shape=jax.ShapeDtypeStruct((n, d), F32),
        scratch_shapes=[pltpu.VMEM((tm, d), BF16)],
        compiler_params=pltpu.CompilerParams(
            dimension_semantics=("arbitrary", "arbitrary"), vmem_limit_bytes=VMEM_LIMIT_FFN),
        name="ffn",
    )(x, gpre.reshape(1, d), w_gu, w_gu, w_down, gpost.reshape(1, d))


def _small_row(gdn_vals, ssm_vals):
    row = jnp.zeros((1, 128), F32)
    row = row.at[0, SMALL_A:SMALL_A + GDN_HEADS].set(gdn_vals.astype(F32))
    return row.at[0, SMALL_DT:SMALL_DT + SSM_HEADS].set(ssm_vals.astype(F32))


def kernel(x, mem, norm_g, w_in, conv_a_w, pool_w, pool_scale, gdn_conv_w, gdn_A_log, gdn_dt_bias, gdn_norm_g,
           ssm_conv_w, ssm_conv_b, ssm_A_log, ssm_dt_bias, ssm_D, ssm_norm_g, w_out, xa_wq, xa_wkv, xa_wo,
           ffn_w_gu, ffn_w_down):
    batch, seq, d = x.shape
    depth = w_in.shape[0]
    xf = x.reshape(batch * seq, d)
    memf = mem.reshape(batch * MEM_LEN, d)
    w_in_t = jnp.swapaxes(w_in, 1, 2)
    for l in range(depth):
        g = norm_g[l]
        mixed = _mixers(
            xf, g[0], *_prep_w_in(w_in_t, l), batch, seq, conv_a_w[l], pool_w[l], pool_scale[l].reshape(1, -1),
            gdn_conv_w[l], gdn_norm_g[l].reshape(1, -1), ssm_conv_w[l], ssm_conv_b[l].reshape(1, -1),
            jnp.repeat(ssm_D[l], SSM_HEAD_DIM).reshape(1, -1), ssm_norm_g[l].reshape(1, -1),
            _small_row(gdn_A_log[l], ssm_A_log[l]), _small_row(gdn_dt_bias[l], ssm_dt_bias[l]),
            tile=MIX_TILE)
        xf = _proj_post(mixed, w_out, l, g[1], xf, tm=ROW_TILE)
        kv = _norm_matmul(memf, g[4], xa_wkv, l, 2 * d, BF16, tm=batch * MEM_LEN, tn=KV_COL_TILE)
        att = _xattn(xf, g[2], xa_wq, l, kv, batch, seq, tq=ROW_TILE)
        xf = _proj_post(att, xa_wo, l, g[3], xf, tm=ROW_TILE)
        xf = _ffn(xf, g[5], ffn_w_gu, ffn_w_down, l, g[6], tm=FFN_ROW_TILE, tf=FFN_COL_TILE)
    return xf.reshape(batch, seq, d)
```

```python
import functools

import jax
import jax.numpy as jnp
from jax import lax
from jax.experimental import pallas as pl
from jax.experimental.pallas import tpu as pltpu

F32 = jnp.float32
BF16 = jnp.bfloat16
EPS = 1e-6

D_MODEL = 2048
GROUP_WIDTH = 512
CHUNK = 64
POOL_WINDOWS = (2, 4, 8, 16)
POOL_GROUP = 128
GDN_HEADS = 4
GDN_HEAD_DIM = 128
SSM_HEADS = 8
SSM_HEAD_DIM = 64
SSM_GROUPS = 2
SSM_STATE = 128
SSM_XBC = 1024
MEM_LEN = 256
XA_HEADS = 4
XA_HEAD_DIM = 512
D_FF = 5632

IN_COLS = 5648
HEAD_COLS = 4096
TAIL_START = 4096
TAIL_VALID = IN_COLS - TAIL_START
TAIL_COLS = 1664
TAIL_BLOCK = 2048
TAIL_SKEW = 8
U_COLS = HEAD_COLS + TAIL_COLS
OFF_A = 0
OFF_B = 1536
OFF_QKV = 2048
OFF_CZ = 3584
OFF_DZ = 4096
OFF_XBC = 4608
OFF_SMALL = 5632
SMALL_DT = 0
SMALL_A = 120
SMALL_B = 124
HALO = 16

MIX_TILE = 4 * CHUNK
MIX_COL_CHUNK = 1024
ROW_TILE = 512
KV_COL_TILE = 512
FFN_ROW_TILE = 1024
FFN_COL_TILE = 512

VMEM_LIMIT = 48 * 1024 * 1024
VMEM_LIMIT_LARGE = 56 * 1024 * 1024
VMEM_LIMIT_FFN = 60000 * 1024


def _rmsnorm(x, g):
    return x * lax.rsqrt(jnp.mean(x * x, axis=-1, keepdims=True) + EPS) * g


def _silu(x):
    return x * jax.nn.sigmoid(x)


def _softplus(x):
    return jnp.maximum(x, 0.0) + jnp.log1p(jnp.exp(-jnp.abs(x)))


def _dot(a, b):
    return jnp.dot(a.astype(BF16), b.astype(BF16), preferred_element_type=F32)


def _dot_nt(a, b):
    return lax.dot_general(a.astype(BF16), b.astype(BF16), (((1,), (1,)), ((), ())),
                           preferred_element_type=F32)


def _dot_tn(a, b):
    return lax.dot_general(a.astype(BF16), b.astype(BF16), (((0,), (0,)), ((), ())),
                           preferred_element_type=F32)


def _split_bf16(a):
    hi = a.astype(BF16)
    return hi, (a - hi.astype(F32)).astype(BF16)


def _dot3(a, b):
    return (jnp.dot(a[0], b[0], preferred_element_type=F32)
            + jnp.dot(a[0], b[1], preferred_element_type=F32)
            + jnp.dot(a[1], b[0], preferred_element_type=F32))


def _norm_matmul_kernel(x_ref, g_ref, w_ref, o_ref, xn_ref):
    @pl.when(pl.program_id(1) == 0)
    def _():
        xn_ref[...] = _rmsnorm(x_ref[...], g_ref[...]).astype(BF16)

    o_ref[...] = jnp.dot(xn_ref[...], w_ref[...].astype(BF16), preferred_element_type=F32).astype(o_ref.dtype)


def _norm_matmul(x, g, w, layer, n_cols, out_dtype, tm, tn):
    n, k = x.shape
    return pl.pallas_call(
        _norm_matmul_kernel,
        grid=(n // tm, n_cols // tn),
        in_specs=[
            pl.BlockSpec((tm, k), lambda i, j: (i, 0)),
            pl.BlockSpec((1, k), lambda i, j: (0, 0)),
            pl.BlockSpec((None, k, tn), lambda i, j: (layer, 0, j)),
        ],
        out_specs=pl.BlockSpec((tm, tn), lambda i, j: (i, j)),
        out_shape=jax.ShapeDtypeStruct((n, n_cols), out_dtype),
        scratch_shapes=[pltpu.VMEM((tm, k), BF16)],
        compiler_params=pltpu.CompilerParams(
            dimension_semantics=("arbitrary", "arbitrary"), vmem_limit_bytes=VMEM_LIMIT_LARGE),
        name="norm_matmul",
    )(x, g.reshape(1, k), w)


PREP_ROWS = 512


def _prep_head_kernel(wt_ref, o_ref):
    o_ref[...] = wt_ref[...].T.astype(BF16)


def _prep_tail_kernel(wt_ref, o_ref):
    dt0 = TAIL_VALID - TAIL_SKEW
    body = TAIL_COLS - 128
    for r in range(0, body, PREP_ROWS):
        o_ref[:, r:r + PREP_ROWS] = wt_ref[TAIL_SKEW + r:TAIL_SKEW + r + PREP_ROWS, :].T.astype(BF16)
    last = jnp.concatenate([wt_ref[dt0:TAIL_VALID, :],
                            jnp.zeros((128 - 2 * TAIL_SKEW, wt_ref.shape[1]), F32),
                            wt_ref[0:TAIL_SKEW, :]], axis=0)
    o_ref[:, body:] = last.T.astype(BF16)


def _prep_w_in(wt, layer):
    k = wt.shape[2]
    head = pl.pallas_call(
        _prep_head_kernel,
        grid=(HEAD_COLS // PREP_ROWS,),
        in_specs=[pl.BlockSpec((None, PREP_ROWS, k), lambda j: (layer, j, 0))],
        out_specs=pl.BlockSpec((k, PREP_ROWS), lambda j: (0, j)),
        out_shape=jax.ShapeDtypeStruct((k, HEAD_COLS), BF16),
        compiler_params=pltpu.CompilerParams(dimension_semantics=("arbitrary",), vmem_limit_bytes=VMEM_LIMIT),
        name="prep_w_in_head",
    )(wt)
    tail = pl.pallas_call(
        _prep_tail_kernel,
        grid=(1,),
        in_specs=[pl.BlockSpec((None, TAIL_BLOCK, k), lambda j: (layer, TAIL_START // TAIL_BLOCK, 0),
                               pipeline_mode=pl.Buffered(1))],
        out_specs=pl.BlockSpec((k, TAIL_COLS), lambda j: (0, 0)),
        out_shape=jax.ShapeDtypeStruct((k, TAIL_COLS), BF16),
        compiler_params=pltpu.CompilerParams(dimension_semantics=("arbitrary",), vmem_limit_bytes=VMEM_LIMIT_LARGE),
        name="prep_w_in_tail",
    )(wt)
    return head, tail


def _proj_post_kernel(a_ref, w_ref, g_ref, x_ref, o_ref, wb_ref):
    @pl.when(pl.program_id(0) == 0)
    def _():
        wb_ref[...] = w_ref[...].astype(BF16)

    y = jnp.dot(a_ref[...], wb_ref[...], preferred_element_type=F32)
    o_ref[...] = x_ref[...] + _rmsnorm(y, g_ref[...])


def _proj_post(a, w, layer, g, x, tm):
    n, k = a.shape
    d = w.shape[2]
    return pl.pallas_call(
        _proj_post_kernel,
        grid=(n // tm,),
        in_specs=[
            pl.BlockSpec((tm, k), lambda i: (i, 0)),
            pl.BlockSpec((None, k, d), lambda i: (layer, 0, 0), pipeline_mode=pl.Buffered(1)),
            pl.BlockSpec((1, d), lambda i: (0, 0)),
            pl.BlockSpec((tm, d), lambda i: (i, 0)),
        ],
        out_specs=pl.BlockSpec((tm, d), lambda i: (i, 0)),
        out_shape=jax.ShapeDtypeStruct((n, d), F32),
        scratch_shapes=[pltpu.VMEM((k, d), BF16)],
        compiler_params=pltpu.CompilerParams(
            dimension_semantics=("arbitrary",), vmem_limit_bytes=VMEM_LIMIT_LARGE),
        name="proj_post",
    )(a, w, g.reshape(1, d), x)


def _shift_rows(x, k):
    return pltpu.roll(x, k, axis=0) if k else x


def _causal_conv(x_ext, w_ref):
    width = w_ref.shape[0]
    if width == 4:
        x1 = _shift_rows(x_ext, 1)
        older = _shift_rows(x_ext * w_ref[1:2, :] + x1 * w_ref[0:1, :], 2)
        return x_ext * w_ref[3:4, :] + x1 * w_ref[2:3, :] + older
    acc = x_ext * w_ref[width - 1:width, :]
    for k in range(width - 1):
        acc = acc + _shift_rows(x_ext, width - 1 - k) * w_ref[k:k + 1, :]
    return acc


def _expand_heads(cols, first, n_heads, width):
    rows = cols.shape[0]
    per_vreg = 128 // width
    lane = lax.broadcasted_iota(jnp.int32, (rows, 128), 1)
    pieces = []
    for m in range(n_heads // per_vreg):
        piece = jnp.broadcast_to(cols[:, first + m * per_vreg:first + m * per_vreg + 1], (rows, 128))
        for j in range(1, per_vreg):
            c = first + m * per_vreg + j
            piece = jnp.where(lane >= j * width, jnp.broadcast_to(cols[:, c:c + 1], (rows, 128)), piece)
        pieces.append(piece)
    return jnp.concatenate(pieces, axis=1)


def _block_diag(y, lane_block, n_blocks):
    return jnp.concatenate([jnp.where(lane_block == b, y, jnp.zeros_like(y)) for b in range(n_blocks)], axis=0)


def _split_block_diag(y, lane_block, n_blocks):
    hi, lo = _split_bf16(y)
    return _block_diag(hi, lane_block, n_blocks), _block_diag(lo, lane_block, n_blocks)


def _unit_lower_inverse(ms, row, col, lane_block, n_blocks):
    same16 = (row >> 4) == (col >> 4)
    same32 = (row >> 5) == (col >> 5)
    off16 = same32 & jnp.logical_not(same16)
    eye = (row == col).astype(F32)
    split = lambda xs: [_split_bf16(x) for x in xs]
    split_bd = lambda xs: [_split_block_diag(x, lane_block, n_blocks) for x in xs]
    d = [jnp.where(same16, m, 0.0) for m in ms]
    d2 = [_dot3(x, y) for x, y in zip(split(d), split_bd(d))]
    p = [eye - x for x in d]
    d2_bd = split_bd(d2)
    p = [x + _dot3(xs, y) for x, xs, y in zip(p, split(p), d2_bd)]
    d4 = [_dot3(x, y) for x, y in zip(split(d2), d2_bd)]
    d4_bd = split_bd(d4)
    p = [x + _dot3(xs, y) for x, xs, y in zip(p, split(p), d4_bd)]
    d8_bd = split_bd([_dot3(x, y) for x, y in zip(split(d4), d4_bd)])
    p = [x + _dot3(xs, y) for x, xs, y in zip(p, split(p), d8_bd)]
    ps, p_bd = split(p), split_bd(p)
    t = split([_dot3(xs, _split_block_diag(jnp.where(off16, m, 0.0), lane_block, n_blocks)) for xs, m in zip(ps, ms)])
    q = [x - _dot3(y, xb) for x, xb, y in zip(p, p_bd, t)]
    qs, q_bd = split(q), split_bd(q)
    t = split([_dot3(xs, _split_block_diag(jnp.where(same32, 0.0, m), lane_block, n_blocks)) for xs, m in zip(qs, ms)])
    return [x - _dot3(y, xb) for x, xb, y in zip(q, q_bd, t)]


def _dot_ltri(ltri, x):
    x1 = x.astype(BF16)
    r1 = x - x1.astype(F32)
    x2 = r1.astype(BF16)
    x3 = (r1 - x2.astype(F32)).astype(BF16)
    return (jnp.dot(ltri, x1, preferred_element_type=F32) + jnp.dot(ltri, x2, preferred_element_type=F32)
            + jnp.dot(ltri, x3, preferred_element_type=F32))


def _mixer_kernel(x_ref, gpre_ref, w_head_ref, w_tail_ref, conv_a_ref, pool_w_ref, pool_scale_ref, gdn_conv_ref,
                  gdn_norm_ref, ssm_conv_ref, ssm_bias_ref, ssm_d_ref, ssm_norm_ref,
                  alog_ref, dtbias_ref, out_ref, u_ref, s_ref, h_ref, *, tile):
    i = pl.program_id(1)
    n_chunks = tile // CHUNK
    hg = SSM_HEADS // SSM_GROUPS
    gw = hg * SSM_HEAD_DIM

    @pl.when(i == 0)
    def _():
        s_ref[...] = jnp.zeros_like(s_ref)
        h_ref[...] = jnp.zeros_like(h_ref)
        u_ref[0:HALO, :] = jnp.zeros((HALO, U_COLS), F32)

    @pl.when(i > 0)
    def _():
        u_ref[0:HALO, :] = u_ref[tile:tile + HALO, :]

    xn = _rmsnorm(x_ref[...], gpre_ref[...]).astype(BF16)
    for c0 in range(0, HEAD_COLS, MIX_COL_CHUNK):
        u_ref[HALO:, c0:c0 + MIX_COL_CHUNK] = jnp.dot(xn, w_head_ref[:, c0:c0 + MIX_COL_CHUNK],
                                                        preferred_element_type=F32)
    u_ref[HALO:, HEAD_COLS:] = jnp.dot(xn, w_tail_ref[...], preferred_element_type=F32)

    def cur(off, width, rows=slice(0, tile)):
        return u_ref[HALO + rows.start:HALO + rows.stop, off:off + width]

    def ext(off, width):
        return u_ref[:, off:off + width]

    p = ext(OFF_A + GROUP_WIDTH, GROUP_WIDTH) * ext(OFF_A + 2 * GROUP_WIDTH, GROUP_WIDTH)
    ya = cur(OFF_A, GROUP_WIDTH) * _causal_conv(p, conv_a_ref)[HALO:]
    out_ref[:, 0:GROUP_WIDTH] = ya.astype(out_ref.dtype)

    pos = (i * tile + 1 + lax.broadcasted_iota(jnp.int32, (tile, 1), 0)).astype(F32)
    for gi, window in enumerate(POOL_WINDOWS):
        off = OFF_B + gi * POOL_GROUP
        s = ext(off, POOL_GROUP)
        k = 1
        while k < window:
            s = s + _shift_rows(s, k)
            k *= 2
        pooled = s[HALO:] / jnp.minimum(pos, float(window)) - cur(off, POOL_GROUP)
        yb = _dot(pooled, pool_w_ref[gi]) * pool_scale_ref[:, gi * POOL_GROUP:(gi + 1) * POOL_GROUP]
        out_ref[:, GROUP_WIDTH + gi * POOL_GROUP:GROUP_WIDTH + (gi + 1) * POOL_GROUP] = yb.astype(out_ref.dtype)

    small = cur(OFF_SMALL, 128)
    sp = _softplus(small + dtbias_ref[...])
    rate = -jnp.exp(alog_ref[...]) * sp
    beta_all = jax.nn.sigmoid(small)

    qkv = _silu(_causal_conv(ext(OFF_QKV, 3 * GROUP_WIDTH), gdn_conv_ref)[HALO:])
    xbc = _silu(_causal_conv(ext(OFF_XBC, SSM_XBC), ssm_conv_ref)[HALO:] + ssm_bias_ref[...])

    n4 = GDN_HEADS * CHUNK
    row = lax.broadcasted_iota(jnp.int32, (CHUNK, n4), 0)
    lane = lax.broadcasted_iota(jnp.int32, (CHUNK, n4), 1)
    col = lane & (CHUNK - 1)
    lane_block = lane >> 6
    incl = row >= col
    strict = row > col
    ltri = (lax.broadcasted_iota(jnp.int32, (CHUNK, CHUNK), 0)
            >= lax.broadcasted_iota(jnp.int32, (CHUNK, CHUNK), 1)).astype(BF16)

    def decay_blocks(rate_c, first, n_heads):
        out = []
        for m in range(n_heads // GDN_HEADS):
            r = _expand_heads(rate_c, first + m * GDN_HEADS, GDN_HEADS, CHUNK)
            diff = _dot_ltri(ltri, jnp.where(strict, r, 0.0))
            out.append(jnp.where(incl, jnp.exp(jnp.where(incl, diff, 0.0)), 0.0))
        return out

    chunks = range(n_chunks)
    rows = [slice(c * CHUNK, (c + 1) * CHUNK) for c in chunks]
    acs = [_dot_ltri(ltri, rate[r]) for r in rows]
    e_acs = [jnp.exp(a) for a in acs]
    e_end = [jnp.exp(a[CHUNK - 1:CHUNK, :] - a) for a in acs]
    e_blk = [jnp.exp(a[CHUNK - 1:CHUNK, :]) for a in acs]

    def head_cols(base, h):
        return slice(base + h * GDN_HEAD_DIM, base + (h + 1) * GDN_HEAD_DIM)

    def l2norm_heads(x, scale):
        return jnp.concatenate(
            [x[:, head_cols(0, h)] * (lax.rsqrt(jnp.sum(x[:, head_cols(0, h)] * x[:, head_cols(0, h)],
                                                        axis=-1, keepdims=True) + EPS) * scale)
             for h in range(GDN_HEADS)], axis=1)

    zeros_head = jnp.zeros((CHUNK, GDN_HEAD_DIM), BF16)
    q_all = [l2norm_heads(qkv[r, 0:GROUP_WIDTH], GDN_HEAD_DIM ** -0.5) for r in rows]
    k_all = [l2norm_heads(qkv[r, GROUP_WIDTH:2 * GROUP_WIDTH], 1.0) for r in rows]
    v_all = [qkv[r, 2 * GROUP_WIDTH:3 * GROUP_WIDTH] for r in rows]
    beta_x = [_expand_heads(beta_all[r], SMALL_B, GDN_HEADS, GDN_HEAD_DIM) for r in rows]
    e_g_x = [_expand_heads(e, SMALL_A, GDN_HEADS, GDN_HEAD_DIM) for e in e_acs]
    kb = [x * b for x, b in zip(k_all, beta_x)]
    k_rows = []
    for x in k_all:
        xb = x.astype(BF16)
        k_rows.append(jnp.concatenate(
            [jnp.concatenate([xb[:, head_cols(0, h)] if h == g else zeros_head for h in range(GDN_HEADS)], axis=1)
             for g in range(GDN_HEADS)], axis=0))
    decay = [decay_blocks(rate[r], SMALL_A, GDN_HEADS)[0] for r in rows]
    m_low = [jnp.where(strict, _dot_nt(x, y) * dc, 0.0) for x, y, dc in zip(kb, k_rows, decay)]
    qk = [_dot_nt(x, y) * dc for x, y, dc in zip(q_all, k_rows, decay)]
    t_inv = _unit_lower_inverse(m_low, row, col, lane_block, GDN_HEADS)
    zeros_rhs = jnp.zeros((CHUNK, 2 * GDN_HEAD_DIM), F32)
    sol = []
    for c in chunks:
        vb, kg = v_all[c] * beta_x[c], kb[c] * e_g_x[c]
        rhs = jnp.concatenate(
            [jnp.concatenate([jnp.concatenate([vb[:, head_cols(0, h)], kg[:, head_cols(0, h)]], axis=1)
                              if h == g else zeros_rhs for h in range(GDN_HEADS)], axis=1)
             for g in range(GDN_HEADS)], axis=0)
        sol.append(_dot3(_split_bf16(t_inv[c]), _split_bf16(rhs)))
    q_dec = [x * e for x, e in zip(q_all, e_g_x)]
    k_end = [x * _expand_heads(e, SMALL_A, GDN_HEADS, GDN_HEAD_DIM) for x, e in zip(k_all, e_end)]

    x_in, y_diag, s_inc, c_mat, e_acs_x, e_blk_x = [], [], [], [], [], []
    for c in chunks:
        r = rows[c]
        x_in.append(xbc[r, :GROUP_WIDTH])
        x_dt = x_in[c] * _expand_heads(sp[r], SMALL_DT, SSM_HEADS, SSM_HEAD_DIM)
        x_end = x_dt * _expand_heads(e_end[c], SMALL_DT, SSM_HEADS, SSM_HEAD_DIM)
        e_acs_x.append(_expand_heads(e_acs[c], SMALL_DT, SSM_HEADS, SSM_HEAD_DIM))
        e_blk_x.append(_expand_heads(e_blk[c], SMALL_DT, SSM_HEADS, SSM_HEAD_DIM))
        l_m = decay_blocks(rate[r], SMALL_DT, SSM_HEADS)
        for g in range(SSM_GROUPS):
            b_g = xbc[r, GROUP_WIDTH + g * SSM_STATE:GROUP_WIDTH + (g + 1) * SSM_STATE]
            c_g = xbc[r, GROUP_WIDTH + (SSM_GROUPS + g) * SSM_STATE:GROUP_WIDTH + (SSM_GROUPS + g + 1) * SSM_STATE]
            b_bf = b_g.astype(BF16)
            cb = _dot_nt(c_g, jnp.concatenate([b_bf] * hg, axis=0))
            x_g = x_dt[:, g * gw:(g + 1) * gw]
            y_diag.append(_dot(cb * l_m[g], _block_diag(x_g.astype(BF16), lane_block, hg)))
            s_inc.append(_dot_tn(b_bf, x_end[:, g * gw:(g + 1) * gw]))
            c_mat.append(c_g)

    s_state = [s_ref[h] for h in range(GDN_HEADS)]
    h_state = [h_ref[g] for g in range(SSM_GROUPS)]
    zeros_v = jnp.zeros((CHUNK, GDN_HEAD_DIM), BF16)
    for c in chunks:
        r = rows[c]
        hd = 2 * GDN_HEAD_DIM
        v_new = [sol[c][:, h * hd:h * hd + GDN_HEAD_DIM]
                 - _dot(sol[c][:, h * hd + GDN_HEAD_DIM:(h + 1) * hd], s_state[h]) for h in range(GDN_HEADS)]
        v_rows = jnp.concatenate(
            [jnp.concatenate([v_new[h].astype(BF16) if h == g else zeros_v for h in range(GDN_HEADS)], axis=1)
             for g in range(GDN_HEADS)], axis=0)
        o_intra = _dot(qk[c], v_rows)
        o = [_dot(q_dec[c][:, head_cols(0, h)], s_state[h]) + o_intra[:, head_cols(0, h)]
             for h in range(GDN_HEADS)]
        s_state = [s_state[h] * e_blk[c][:, SMALL_A + h:SMALL_A + h + 1]
                   + _dot_tn(k_end[c][:, head_cols(0, h)], v_new[h]) for h in range(GDN_HEADS)]
        for h in range(GDN_HEADS):
            z = cur(OFF_CZ + h * GDN_HEAD_DIM, GDN_HEAD_DIM, r)
            y = _rmsnorm(o[h], gdn_norm_ref[...]) * _silu(z)
            out_ref[r, head_cols(2 * GROUP_WIDTH, h)] = y.astype(out_ref.dtype)

        ys = []
        for g in range(SSM_GROUPS):
            n = c * SSM_GROUPS + g
            ys.append(y_diag[n] + _dot(c_mat[n], h_state[g]) * e_acs_x[c][:, g * gw:(g + 1) * gw])
            h_state[g] = h_state[g] * e_blk_x[c][:, g * gw:(g + 1) * gw] + s_inc[n]
        y = jnp.concatenate(ys, axis=1) + ssm_d_ref[...] * x_in[c]
        y = y * _silu(cur(OFF_DZ, GROUP_WIDTH, r))
        for g in range(SSM_GROUPS):
            yn = _rmsnorm(y[:, g * gw:(g + 1) * gw], ssm_norm_ref[:, g * gw:(g + 1) * gw])
            out_ref[r, 3 * GROUP_WIDTH + g * gw:3 * GROUP_WIDTH + (g + 1) * gw] = yn.astype(out_ref.dtype)
    for h in range(GDN_HEADS):
        s_ref[h] = s_state[h]
    for g in range(SSM_GROUPS):
        h_ref[g] = h_state[g]


def _mixers(x, g_pre, w_head, w_tail, batch, seq, conv_a_w, pool_w, pool_scale, gdn_conv_w, gdn_norm_g, ssm_conv_w,
            ssm_conv_b, ssm_d_x, ssm_norm_g, alog_row, dtbias_row, tile):
    n_tiles = seq // tile
    d = x.shape[1]

    def const(shape):
        return pl.BlockSpec(shape, lambda b, i: (0,) * len(shape))

    return pl.pallas_call(
        functools.partial(_mixer_kernel, tile=tile),
        grid=(batch, n_tiles),
        in_specs=[
            pl.BlockSpec((tile, d), lambda b, i: (b * n_tiles + i, 0)),
            const((1, d)),
            pl.BlockSpec((d, HEAD_COLS), lambda b, i: (0, 0), pipeline_mode=pl.Buffered(1)),
            pl.BlockSpec((d, TAIL_COLS), lambda b, i: (0, 0), pipeline_mode=pl.Buffered(1)),
            const(conv_a_w.shape), const(pool_w.shape), const(pool_scale.shape), const(gdn_conv_w.shape),
            const(gdn_norm_g.shape), const(ssm_conv_w.shape), const(ssm_conv_b.shape), const(ssm_d_x.shape),
            const(ssm_norm_g.shape), const(alog_row.shape), const(dtbias_row.shape),
        ],
        out_specs=pl.BlockSpec((tile, D_MODEL), lambda b, i: (b * n_tiles + i, 0)),
        out_shape=jax.ShapeDtypeStruct((batch * seq, D_MODEL), BF16),
        scratch_shapes=[
            pltpu.VMEM((HALO + tile, U_COLS), F32),
            pltpu.VMEM((GDN_HEADS, GDN_HEAD_DIM, GDN_HEAD_DIM), F32),
            pltpu.VMEM((SSM_GROUPS, SSM_STATE, (SSM_HEADS // SSM_GROUPS) * SSM_HEAD_DIM), F32),
        ],
        compiler_params=pltpu.CompilerParams(
            dimension_semantics=("arbitrary", "arbitrary"), vmem_limit_bytes=VMEM_LIMIT_LARGE),
        name="mixers",
    )(x, g_pre.reshape(1, d), w_head, w_tail, conv_a_w, pool_w, pool_scale, gdn_conv_w, gdn_norm_g, ssm_conv_w,
      ssm_conv_b, ssm_d_x, ssm_norm_g, alog_row, dtbias_row)


def _xattn_kernel(x_ref, g_ref, wq_ref, kv_ref, o_ref, wb_ref):
    @pl.when((pl.program_id(0) == 0) & (pl.program_id(1) == 0))
    def _():
        wb_ref[...] = wq_ref[...].astype(BF16)

    xn = _rmsnorm(x_ref[...], g_ref[...]).astype(BF16)
    q = jnp.dot(xn, wb_ref[...], preferred_element_type=F32)
    for h in range(XA_HEADS):
        lo = h * XA_HEAD_DIM
        k_h = kv_ref[:, lo:lo + XA_HEAD_DIM]
        v_h = kv_ref[:, D_MODEL + lo:D_MODEL + lo + XA_HEAD_DIM]
        s = _dot_nt(q[:, lo:lo + XA_HEAD_DIM], k_h) * XA_HEAD_DIM ** -0.5
        e = jnp.exp(s - jnp.max(s, axis=-1, keepdims=True))
        p = e / jnp.sum(e, axis=-1, keepdims=True)
        o_ref[:, lo:lo + XA_HEAD_DIM] = _dot(p, v_h).astype(o_ref.dtype)


def _xattn(x, g, wq, layer, kv, batch, seq, tq):
    n, d = x.shape
    n_tiles = seq // tq
    return pl.pallas_call(
        _xattn_kernel,
        grid=(batch, n_tiles),
        in_specs=[
            pl.BlockSpec((tq, d), lambda b, i: (b * n_tiles + i, 0)),
            pl.BlockSpec((1, d), lambda b, i: (0, 0)),
            pl.BlockSpec((None, d, d), lambda b, i: (layer, 0, 0), pipeline_mode=pl.Buffered(1)),
            pl.BlockSpec((MEM_LEN, 2 * d), lambda b, i: (b, 0)),
        ],
        out_specs=pl.BlockSpec((tq, d), lambda b, i: (b * n_tiles + i, 0)),
        out_shape=jax.ShapeDtypeStruct((n, d), BF16),
        scratch_shapes=[pltpu.VMEM((d, d), BF16)],
        compiler_params=pltpu.CompilerParams(
            dimension_semantics=("arbitrary", "arbitrary"), vmem_limit_bytes=VMEM_LIMIT),
        name="xattn",
    )(x, g.reshape(1, d), wq, kv)


def _ffn_kernel(x_ref, gpre_ref, wg_ref, wu_ref, wd_ref, gpost_ref, o_ref, xn_ref):
    j = pl.program_id(1)

    @pl.when(j == 0)
    def _():
        xn_ref[...] = _rmsnorm(x_ref[...], gpre_ref[...]).astype(BF16)
        o_ref[...] = jnp.zeros_like(o_ref)

    xn = xn_ref[...]
    gate = jnp.dot(xn, wg_ref[...].astype(BF16), preferred_element_type=F32)
    up = jnp.dot(xn, wu_ref[...].astype(BF16), preferred_element_type=F32)
    o_ref[...] += jnp.dot((_silu(gate) * up).astype(BF16), wd_ref[...].astype(BF16), preferred_element_type=F32)

    @pl.when(j == pl.num_programs(1) - 1)
    def _():
        o_ref[...] = x_ref[...] + _rmsnorm(o_ref[...], gpost_ref[...])


def _ffn(x, gpre, w_gu, w_down, layer, gpost, tm, tf, io_buffers):
    n, d = x.shape
    n_ff = D_FF // tf
    return pl.pallas_call(
        _ffn_kernel,
        grid=(n // tm, n_ff),
        in_specs=[
            pl.BlockSpec((tm, d), lambda i, j: (i, 0), pipeline_mode=pl.Buffered(io_buffers)),
            pl.BlockSpec((1, d), lambda i, j: (0, 0)),
            pl.BlockSpec((None, d, tf), lambda i, j: (layer, 0, j)),
            pl.BlockSpec((None, d, tf), lambda i, j: (layer, 0, j + n_ff)),
            pl.BlockSpec((None, tf, d), lambda i, j: (layer, j, 0)),
            pl.BlockSpec((1, d), lambda i, j: (0, 0)),
        ],
        out_specs=pl.BlockSpec((tm, d), lambda i, j: (i, 0), pipeline_mode=pl.Buffered(io_buffers)),
        out_shape=jax.ShapeDtypeStruct((n, d), F32),
        scratch_shapes=[pltpu.VMEM((tm, d), BF16)],
        compiler_params=pltpu.CompilerParams(
            dimension_semantics=("arbitrary", "arbitrary"), vmem_limit_bytes=VMEM_LIMIT_FFN),
        name="ffn",
    )(x, gpre.reshape(1, d), w_gu, w_gu, w_down, gpost.reshape(1, d))


def _small_row(gdn_vals, ssm_vals):
    row = jnp.zeros((1, 128), F32)
    row = row.at[0, SMALL_A:SMALL_A + GDN_HEADS].set(gdn_vals.astype(F32))
    return row.at[0, SMALL_DT:SMALL_DT + SSM_HEADS].set(ssm_vals.astype(F32))


def kernel(x, mem, norm_g, w_in, conv_a_w, pool_w, pool_scale, gdn_conv_w, gdn_A_log, gdn_dt_bias, gdn_norm_g,
           ssm_conv_w, ssm_conv_b, ssm_A_log, ssm_dt_bias, ssm_D, ssm_norm_g, w_out, xa_wq, xa_wkv, xa_wo,
           ffn_w_gu, ffn_w_down):
    batch, seq, d = x.shape
    depth = w_in.shape[0]
    xf = x.reshape(batch * seq, d)
    memf = mem.reshape(batch * MEM_LEN, d)
    w_in_t = jnp.swapaxes(w_in, 1, 2)
    for l in range(depth):
        g = norm_g[l]
        mixed = _mixers(
            xf, g[0], *_prep_w_in(w_in_t, l), batch, seq, conv_a_w[l], pool_w[l], pool_scale[l].reshape(1, -1),
            gdn_conv_w[l], gdn_norm_g[l].reshape(1, -1), ssm_conv_w[l], ssm_conv_b[l].reshape(1, -1),
            jnp.repeat(ssm_D[l], SSM_HEAD_DIM).reshape(1, -1), ssm_norm_g[l].reshape(1, -1),
            _small_row(gdn_A_log[l], ssm_A_log[l]), _small_row(gdn_dt_bias[l], ssm_dt_bias[l]),
            tile=MIX_TILE)
        xf = _proj_post(mixed, w_out, l, g[1], xf, tm=ROW_TILE)
        kv = _norm_matmul(memf, g[4], xa_wkv, l, 2 * d, BF16, tm=batch * MEM_LEN, tn=KV_COL_TILE)
        att = _xattn(xf, g[2], xa_wq, l, kv, batch, seq, tq=ROW_TILE)
        xf = _proj_post(att, xa_wo, l, g[3], xf, tm=ROW_TILE)
        if l == 0:
            xf = _ffn(xf, g[5], ffn_w_gu, ffn_w_down, l, g[6], tm=FFN_ROW_TILE, tf=256, io_buffers=2)
        else:
            xf = _ffn(xf, g[5], ffn_w_gu, ffn_w_down, l, g[6], tm=FFN_ROW_TILE, tf=FFN_COL_TILE, io_buffers=1)
    return xf.reshape(batch, seq, d)
```

```python
import functools

import jax
import jax.numpy as jnp
from jax import lax
from jax.experimental import pallas as pl
from jax.experimental.pallas import tpu as pltpu

F32 = jnp.float32
BF16 = jnp.bfloat16
EPS = 1e-6

D_MODEL = 2048
GROUP_WIDTH = 512
CHUNK = 64
POOL_WINDOWS = (2, 4, 8, 16)
POOL_GROUP = 128
GDN_HEADS = 4
GDN_HEAD_DIM = 128
SSM_HEADS = 8
SSM_HEAD_DIM = 64
SSM_GROUPS = 2
SSM_STATE = 128
SSM_XBC = 1024
MEM_LEN = 256
XA_HEADS = 4
XA_HEAD_DIM = 512
D_FF = 5632

IN_COLS = 5648
HEAD_COLS = 4096
TAIL_START = 4096
TAIL_VALID = IN_COLS - TAIL_START
TAIL_COLS = 1664
TAIL_BLOCK = 2048
TAIL_SKEW = 8
U_COLS = HEAD_COLS + TAIL_COLS
OFF_A = 0
OFF_B = 1536
OFF_QKV = 2048
OFF_CZ = 3584
OFF_DZ = 4096
OFF_XBC = 4608
OFF_SMALL = 5632
SMALL_DT = 0
SMALL_A = 120
SMALL_B = 124
HALO = 16

MIX_TILE = 4 * CHUNK
MIX_COL_CHUNK = 1024
ROW_TILE = 512
KV_COL_TILE = 512
FFN_ROW_TILE = 1024
FFN_COL_TILE = 256

VMEM_LIMIT = 48 * 1024 * 1024
VMEM_LIMIT_LARGE = 56 * 1024 * 1024
VMEM_LIMIT_FFN = 60000 * 1024


def _rmsnorm(x, g):
    return x * lax.rsqrt(jnp.mean(x * x, axis=-1, keepdims=True) + EPS) * g


def _silu(x):
    return x * jax.nn.sigmoid(x)


def _softplus(x):
    return jnp.maximum(x, 0.0) + jnp.log1p(jnp.exp(-jnp.abs(x)))


def _dot(a, b):
    return jnp.dot(a.astype(BF16), b.astype(BF16), preferred_element_type=F32)


def _dot_nt(a, b):
    return lax.dot_general(a.astype(BF16), b.astype(BF16), (((1,), (1,)), ((), ())),
                           preferred_element_type=F32)


def _dot_tn(a, b):
    return lax.dot_general(a.astype(BF16), b.astype(BF16), (((0,), (0,)), ((), ())),
                           preferred_element_type=F32)


def _split_bf16(a):
    hi = a.astype(BF16)
    return hi, (a - hi.astype(F32)).astype(BF16)


def _dot3(a, b):
    return (jnp.dot(a[0], b[0], preferred_element_type=F32)
            + jnp.dot(a[0], b[1], preferred_element_type=F32)
            + jnp.dot(a[1], b[0], preferred_element_type=F32))


def _norm_matmul_kernel(x_ref, g_ref, w_ref, o_ref, xn_ref):
    @pl.when(pl.program_id(1) == 0)
    def _():
        xn_ref[...] = _rmsnorm(x_ref[...], g_ref[...]).astype(BF16)

    o_ref[...] = jnp.dot(xn_ref[...], w_ref[...].astype(BF16), preferred_element_type=F32).astype(o_ref.dtype)


def _norm_matmul(x, g, w, layer, n_cols, out_dtype, tm, tn):
    n, k = x.shape
    return pl.pallas_call(
        _norm_matmul_kernel,
        grid=(n // tm, n_cols // tn),
        in_specs=[
            pl.BlockSpec((tm, k), lambda i, j: (i, 0)),
            pl.BlockSpec((1, k), lambda i, j: (0, 0)),
            pl.BlockSpec((None, k, tn), lambda i, j: (layer, 0, j)),
        ],
        out_specs=pl.BlockSpec((tm, tn), lambda i, j: (i, j)),
        out_shape=jax.ShapeDtypeStruct((n, n_cols), out_dtype),
        scratch_shapes=[pltpu.VMEM((tm, k), BF16)],
        compiler_params=pltpu.CompilerParams(
            dimension_semantics=("arbitrary", "arbitrary"), vmem_limit_bytes=VMEM_LIMIT_LARGE),
        name="norm_matmul",
    )(x, g.reshape(1, k), w)


PREP_ROWS = 512


def _prep_head_kernel(wt_ref, o_ref):
    o_ref[...] = wt_ref[...].T.astype(BF16)


def _prep_tail_kernel(wt_ref, o_ref):
    dt0 = TAIL_VALID - TAIL_SKEW
    body = TAIL_COLS - 128
    for r in range(0, body, PREP_ROWS):
        o_ref[:, r:r + PREP_ROWS] = wt_ref[TAIL_SKEW + r:TAIL_SKEW + r + PREP_ROWS, :].T.astype(BF16)
    last = jnp.concatenate([wt_ref[dt0:TAIL_VALID, :],
                            jnp.zeros((128 - 2 * TAIL_SKEW, wt_ref.shape[1]), F32),
                            wt_ref[0:TAIL_SKEW, :]], axis=0)
    o_ref[:, body:] = last.T.astype(BF16)


def _prep_w_in(wt, layer):
    k = wt.shape[2]
    head = pl.pallas_call(
        _prep_head_kernel,
        grid=(HEAD_COLS // PREP_ROWS,),
        in_specs=[pl.BlockSpec((None, PREP_ROWS, k), lambda j: (layer, j, 0))],
        out_specs=pl.BlockSpec((k, PREP_ROWS), lambda j: (0, j)),
        out_shape=jax.ShapeDtypeStruct((k, HEAD_COLS), BF16),
        compiler_params=pltpu.CompilerParams(dimension_semantics=("arbitrary",), vmem_limit_bytes=VMEM_LIMIT),
        name="prep_w_in_head",
    )(wt)
    tail = pl.pallas_call(
        _prep_tail_kernel,
        grid=(1,),
        in_specs=[pl.BlockSpec((None, TAIL_BLOCK, k), lambda j: (layer, TAIL_START // TAIL_BLOCK, 0),
                               pipeline_mode=pl.Buffered(1))],
        out_specs=pl.BlockSpec((k, TAIL_COLS), lambda j: (0, 0)),
        out_shape=jax.ShapeDtypeStruct((k, TAIL_COLS), BF16),
        compiler_params=pltpu.CompilerParams(dimension_semantics=("arbitrary",), vmem_limit_bytes=VMEM_LIMIT_LARGE),
        name="prep_w_in_tail",
    )(wt)
    return head, tail


def _proj_post_kernel(a_ref, w_ref, g_ref, x_ref, o_ref, wb_ref):
    @pl.when(pl.program_id(0) == 0)
    def _():
        wb_ref[...] = w_ref[...].astype(BF16)

    y = jnp.dot(a_ref[...], wb_ref[...], preferred_element_type=F32)
    o_ref[...] = x_ref[...] + _rmsnorm(y, g_ref[...])


def _proj_post(a, w, layer, g, x, tm):
    n, k = a.shape
    d = w.shape[2]
    return pl.pallas_call(
        _proj_post_kernel,
        grid=(n // tm,),
        in_specs=[
            pl.BlockSpec((tm, k), lambda i: (i, 0)),
            pl.BlockSpec((None, k, d), lambda i: (layer, 0, 0), pipeline_mode=pl.Buffered(1)),
            pl.BlockSpec((1, d), lambda i: (0, 0)),
            pl.BlockSpec((tm, d), lambda i: (i, 0)),
        ],
        out_specs=pl.BlockSpec((tm, d), lambda i: (i, 0)),
        out_shape=jax.ShapeDtypeStruct((n, d), F32),
        scratch_shapes=[pltpu.VMEM((k, d), BF16)],
        compiler_params=pltpu.CompilerParams(
            dimension_semantics=("arbitrary",), vmem_limit_bytes=VMEM_LIMIT_LARGE),
        name="proj_post",
    )(a, w, g.reshape(1, d), x)


def _shift_rows(x, k):
    return pltpu.roll(x, k, axis=0) if k else x


def _causal_conv(x_ext, w_ref):
    width = w_ref.shape[0]
    if width == 4:
        x1 = _shift_rows(x_ext, 1)
        older = _shift_rows(x_ext * w_ref[1:2, :] + x1 * w_ref[0:1, :], 2)
        return x_ext * w_ref[3:4, :] + x1 * w_ref[2:3, :] + older
    acc = x_ext * w_ref[width - 1:width, :]
    for k in range(width - 1):
        acc = acc + _shift_rows(x_ext, width - 1 - k) * w_ref[k:k + 1, :]
    return acc


def _expand_heads(cols, first, n_heads, width):
    rows = cols.shape[0]
    per_vreg = 128 // width
    lane = lax.broadcasted_iota(jnp.int32, (rows, 128), 1)
    pieces = []
    for m in range(n_heads // per_vreg):
        piece = jnp.broadcast_to(cols[:, first + m * per_vreg:first + m * per_vreg + 1], (rows, 128))
        for j in range(1, per_vreg):
            c = first + m * per_vreg + j
            piece = jnp.where(lane >= j * width, jnp.broadcast_to(cols[:, c:c + 1], (rows, 128)), piece)
        pieces.append(piece)
    return jnp.concatenate(pieces, axis=1)


def _block_diag(y, lane_block, n_blocks):
    return jnp.concatenate([jnp.where(lane_block == b, y, jnp.zeros_like(y)) for b in range(n_blocks)], axis=0)


def _split_block_diag(y, lane_block, n_blocks):
    hi, lo = _split_bf16(y)
    return _block_diag(hi, lane_block, n_blocks), _block_diag(lo, lane_block, n_blocks)


def _unit_lower_inverse(ms, row, col, lane_block, n_blocks):
    same16 = (row >> 4) == (col >> 4)
    same32 = (row >> 5) == (col >> 5)
    off16 = same32 & jnp.logical_not(same16)
    eye = (row == col).astype(F32)
    split = lambda xs: [_split_bf16(x) for x in xs]
    split_bd = lambda xs: [_split_block_diag(x, lane_block, n_blocks) for x in xs]
    d = [jnp.where(same16, m, 0.0) for m in ms]
    d2 = [_dot3(x, y) for x, y in zip(split(d), split_bd(d))]
    p = [eye - x for x in d]
    d2_bd = split_bd(d2)
    p = [x + _dot3(xs, y) for x, xs, y in zip(p, split(p), d2_bd)]
    d4 = [_dot3(x, y) for x, y in zip(split(d2), d2_bd)]
    d4_bd = split_bd(d4)
    p = [x + _dot3(xs, y) for x, xs, y in zip(p, split(p), d4_bd)]
    d8_bd = split_bd([_dot3(x, y) for x, y in zip(split(d4), d4_bd)])
    p = [x + _dot3(xs, y) for x, xs, y in zip(p, split(p), d8_bd)]
    ps, p_bd = split(p), split_bd(p)
    t = split([_dot3(xs, _split_block_diag(jnp.where(off16, m, 0.0), lane_block, n_blocks)) for xs, m in zip(ps, ms)])
    q = [x - _dot3(y, xb) for x, xb, y in zip(p, p_bd, t)]
    qs, q_bd = split(q), split_bd(q)
    t = split([_dot3(xs, _split_block_diag(jnp.where(same32, 0.0, m), lane_block, n_blocks)) for xs, m in zip(qs, ms)])
    return [x - _dot3(y, xb) for x, xb, y in zip(q, q_bd, t)]


def _dot_ltri(ltri, x):
    x1 = x.astype(BF16)
    r1 = x - x1.astype(F32)
    x2 = r1.astype(BF16)
    x3 = (r1 - x2.astype(F32)).astype(BF16)
    return (jnp.dot(ltri, x1, preferred_element_type=F32) + jnp.dot(ltri, x2, preferred_element_type=F32)
            + jnp.dot(ltri, x3, preferred_element_type=F32))


def _mixer_kernel(x_ref, gpre_ref, w_head_ref, w_tail_ref, conv_a_ref, pool_w_ref, pool_scale_ref, gdn_conv_ref,
                  gdn_norm_ref, ssm_conv_ref, ssm_bias_ref, ssm_d_ref, ssm_norm_ref,
                  alog_ref, dtbias_ref, out_ref, u_ref, s_ref, h_ref, *, tile):
    i = pl.program_id(1)
    n_chunks = tile // CHUNK
    hg = SSM_HEADS // SSM_GROUPS
    gw = hg * SSM_HEAD_DIM

    @pl.when(i == 0)
    def _():
        s_ref[...] = jnp.zeros_like(s_ref)
        h_ref[...] = jnp.zeros_like(h_ref)
        u_ref[0:HALO, :] = jnp.zeros((HALO, U_COLS), F32)

    @pl.when(i > 0)
    def _():
        u_ref[0:HALO, :] = u_ref[tile:tile + HALO, :]

    xn = _rmsnorm(x_ref[...], gpre_ref[...]).astype(BF16)
    for c0 in range(0, HEAD_COLS, MIX_COL_CHUNK):
        u_ref[HALO:, c0:c0 + MIX_COL_CHUNK] = jnp.dot(xn, w_head_ref[:, c0:c0 + MIX_COL_CHUNK],
                                                        preferred_element_type=F32)
    u_ref[HALO:, HEAD_COLS:] = jnp.dot(xn, w_tail_ref[...], preferred_element_type=F32)

    def cur(off, width, rows=slice(0, tile)):
        return u_ref[HALO + rows.start:HALO + rows.stop, off:off + width]

    def ext(off, width):
        return u_ref[:, off:off + width]

    p = ext(OFF_A + GROUP_WIDTH, GROUP_WIDTH) * ext(OFF_A + 2 * GROUP_WIDTH, GROUP_WIDTH)
    ya = cur(OFF_A, GROUP_WIDTH) * _causal_conv(p, conv_a_ref)[HALO:]
    out_ref[:, 0:GROUP_WIDTH] = ya.astype(out_ref.dtype)

    pos = (i * tile + 1 + lax.broadcasted_iota(jnp.int32, (tile, 1), 0)).astype(F32)
    for gi, window in enumerate(POOL_WINDOWS):
        off = OFF_B + gi * POOL_GROUP
        s = ext(off, POOL_GROUP)
        k = 1
        while k < window:
            s = s + _shift_rows(s, k)
            k *= 2
        pooled = s[HALO:] / jnp.minimum(pos, float(window)) - cur(off, POOL_GROUP)
        yb = _dot(pooled, pool_w_ref[gi]) * pool_scale_ref[:, gi * POOL_GROUP:(gi + 1) * POOL_GROUP]
        out_ref[:, GROUP_WIDTH + gi * POOL_GROUP:GROUP_WIDTH + (gi + 1) * POOL_GROUP] = yb.astype(out_ref.dtype)

    small = cur(OFF_SMALL, 128)
    sp = _softplus(small + dtbias_ref[...])
    rate = -jnp.exp(alog_ref[...]) * sp
    beta_all = jax.nn.sigmoid(small)

    qkv = _silu(_causal_conv(ext(OFF_QKV, 3 * GROUP_WIDTH), gdn_conv_ref)[HALO:])
    xbc = _silu(_causal_conv(ext(OFF_XBC, SSM_XBC), ssm_conv_ref)[HALO:] + ssm_bias_ref[...])

    n4 = GDN_HEADS * CHUNK
    row = lax.broadcasted_iota(jnp.int32, (CHUNK, n4), 0)
    lane = lax.broadcasted_iota(jnp.int32, (CHUNK, n4), 1)
    col = lane & (CHUNK - 1)
    lane_block = lane >> 6
    incl = row >= col
    strict = row > col
    ltri = (lax.broadcasted_iota(jnp.int32, (CHUNK, CHUNK), 0)
            >= lax.broadcasted_iota(jnp.int32, (CHUNK, CHUNK), 1)).astype(BF16)

    def decay_blocks(rate_c, first, n_heads):
        out = []
        for m in range(n_heads // GDN_HEADS):
            r = _expand_heads(rate_c, first + m * GDN_HEADS, GDN_HEADS, CHUNK)
            diff = _dot_ltri(ltri, jnp.where(strict, r, 0.0))
            out.append(jnp.where(incl, jnp.exp(jnp.where(incl, diff, 0.0)), 0.0))
        return out

    chunks = range(n_chunks)
    rows = [slice(c * CHUNK, (c + 1) * CHUNK) for c in chunks]
    acs = [_dot_ltri(ltri, rate[r]) for r in rows]
    e_acs = [jnp.exp(a) for a in acs]
    e_end = [jnp.exp(a[CHUNK - 1:CHUNK, :] - a) for a in acs]
    e_blk = [jnp.exp(a[CHUNK - 1:CHUNK, :]) for a in acs]

    def head_cols(base, h):
        return slice(base + h * GDN_HEAD_DIM, base + (h + 1) * GDN_HEAD_DIM)

    def l2norm_heads(x, scale):
        return jnp.concatenate(
            [x[:, head_cols(0, h)] * (lax.rsqrt(jnp.sum(x[:, head_cols(0, h)] * x[:, head_cols(0, h)],
                                                        axis=-1, keepdims=True) + EPS) * scale)
             for h in range(GDN_HEADS)], axis=1)

    zeros_head = jnp.zeros((CHUNK, GDN_HEAD_DIM), BF16)
    q_all = [l2norm_heads(qkv[r, 0:GROUP_WIDTH], GDN_HEAD_DIM ** -0.5) for r in rows]
    k_all = [l2norm_heads(qkv[r, GROUP_WIDTH:2 * GROUP_WIDTH], 1.0) for r in rows]
    v_all = [qkv[r, 2 * GROUP_WIDTH:3 * GROUP_WIDTH] for r in rows]
    beta_x = [_expand_heads(beta_all[r], SMALL_B, GDN_HEADS, GDN_HEAD_DIM) for r in rows]
    e_g_x = [_expand_heads(e, SMALL_A, GDN_HEADS, GDN_HEAD_DIM) for e in e_acs]
    kb = [x * b for x, b in zip(k_all, beta_x)]
    k_rows = []
    for x in k_all:
        xb = x.astype(BF16)
        k_rows.append(jnp.concatenate(
            [jnp.concatenate([xb[:, head_cols(0, h)] if h == g else zeros_head for h in range(GDN_HEADS)], axis=1)
             for g in range(GDN_HEADS)], axis=0))
    decay = [decay_blocks(rate[r], SMALL_A, GDN_HEADS)[0] for r in rows]
    m_low = [jnp.where(strict, _dot_nt(x, y) * dc, 0.0) for x, y, dc in zip(kb, k_rows, decay)]
    qk = [_dot_nt(x, y) * dc for x, y, dc in zip(q_all, k_rows, decay)]
    t_inv = _unit_lower_inverse(m_low, row, col, lane_block, GDN_HEADS)
    zeros_rhs = jnp.zeros((CHUNK, 2 * GDN_HEAD_DIM), F32)
    sol = []
    for c in chunks:
        vb, kg = v_all[c] * beta_x[c], kb[c] * e_g_x[c]
        rhs = jnp.concatenate(
            [jnp.concatenate([jnp.concatenate([vb[:, head_cols(0, h)], kg[:, head_cols(0, h)]], axis=1)
                              if h == g else zeros_rhs for h in range(GDN_HEADS)], axis=1)
             for g in range(GDN_HEADS)], axis=0)
        sol.append(_dot3(_split_bf16(t_inv[c]), _split_bf16(rhs)))
    q_dec = [x * e for x, e in zip(q_all, e_g_x)]
    k_end = [x * _expand_heads(e, SMALL_A, GDN_HEADS, GDN_HEAD_DIM) for x, e in zip(k_all, e_end)]

    x_in, y_diag, s_inc, c_mat, e_acs_x, e_blk_x = [], [], [], [], [], []
    for c in chunks:
        r = rows[c]
        x_in.append(xbc[r, :GROUP_WIDTH])
        x_dt = x_in[c] * _expand_heads(sp[r], SMALL_DT, SSM_HEADS, SSM_HEAD_DIM)
        x_end = x_dt * _expand_heads(e_end[c], SMALL_DT, SSM_HEADS, SSM_HEAD_DIM)
        e_acs_x.append(_expand_heads(e_acs[c], SMALL_DT, SSM_HEADS, SSM_HEAD_DIM))
        e_blk_x.append(_expand_heads(e_blk[c], SMALL_DT, SSM_HEADS, SSM_HEAD_DIM))
        l_m = decay_blocks(rate[r], SMALL_DT, SSM_HEADS)
        for g in range(SSM_GROUPS):
            b_g = xbc[r, GROUP_WIDTH + g * SSM_STATE:GROUP_WIDTH + (g + 1) * SSM_STATE]
            c_g = xbc[r, GROUP_WIDTH + (SSM_GROUPS + g) * SSM_STATE:GROUP_WIDTH + (SSM_GROUPS + g + 1) * SSM_STATE]
            b_bf = b_g.astype(BF16)
            cb = _dot_nt(c_g, jnp.concatenate([b_bf] * hg, axis=0))
            x_g = x_dt[:, g * gw:(g + 1) * gw]
            y_diag.append(_dot(cb * l_m[g], _block_diag(x_g.astype(BF16), lane_block, hg)))
            s_inc.append(_dot_tn(b_bf, x_end[:, g * gw:(g + 1) * gw]))
            c_mat.append(c_g)

    s_state = [s_ref[h] for h in range(GDN_HEADS)]
    h_state = [h_ref[g] for g in range(SSM_GROUPS)]
    zeros_v = jnp.zeros((CHUNK, GDN_HEAD_DIM), BF16)
    for c in chunks:
        r = rows[c]
        hd = 2 * GDN_HEAD_DIM
        v_new = [sol[c][:, h * hd:h * hd + GDN_HEAD_DIM]
                 - _dot(sol[c][:, h * hd + GDN_HEAD_DIM:(h + 1) * hd], s_state[h]) for h in range(GDN_HEADS)]
        v_rows = jnp.concatenate(
            [jnp.concatenate([v_new[h].astype(BF16) if h == g else zeros_v for h in range(GDN_HEADS)], axis=1)
             for g in range(GDN_HEADS)], axis=0)
        o_intra = _dot(qk[c], v_rows)
        o = [_dot(q_dec[c][:, head_cols(0, h)], s_state[h]) + o_intra[:, head_cols(0, h)]
             for h in range(GDN_HEADS)]
        s_state = [s_state[h] * e_blk[c][:, SMALL_A + h:SMALL_A + h + 1]
                   + _dot_tn(k_end[c][:, head_cols(0, h)], v_new[h]) for h in range(GDN_HEADS)]
        for h in range(GDN_HEADS):
            z = cur(OFF_CZ + h * GDN_HEAD_DIM, GDN_HEAD_DIM, r)
            y = _rmsnorm(o[h], gdn_norm_ref[...]) * _silu(z)
            out_ref[r, head_cols(2 * GROUP_WIDTH, h)] = y.astype(out_ref.dtype)

        ys = []
        for g in range(SSM_GROUPS):
            n = c * SSM_GROUPS + g
            ys.append(y_diag[n] + _dot(c_mat[n], h_state[g]) * e_acs_x[c][:, g * gw:(g + 1) * gw])
            h_state[g] = h_state[g] * e_blk_x[c][:, g * gw:(g + 1) * gw] + s_inc[n]
        y = jnp.concatenate(ys, axis=1) + ssm_d_ref[...] * x_in[c]
        y = y * _silu(cur(OFF_DZ, GROUP_WIDTH, r))
        for g in range(SSM_GROUPS):
            yn = _rmsnorm(y[:, g * gw:(g + 1) * gw], ssm_norm_ref[:, g * gw:(g + 1) * gw])
            out_ref[r, 3 * GROUP_WIDTH + g * gw:3 * GROUP_WIDTH + (g + 1) * gw] = yn.astype(out_ref.dtype)
    for h in range(GDN_HEADS):
        s_ref[h] = s_state[h]
    for g in range(SSM_GROUPS):
        h_ref[g] = h_state[g]


def _mixers(x, g_pre, w_head, w_tail, batch, seq, conv_a_w, pool_w, pool_scale, gdn_conv_w, gdn_norm_g, ssm_conv_w,
            ssm_conv_b, ssm_d_x, ssm_norm_g, alog_row, dtbias_row, tile):
    n_tiles = seq // tile
    d = x.shape[1]

    def const(shape):
        return pl.BlockSpec(shape, lambda b, i: (0,) * len(shape))

    return pl.pallas_call(
        functools.partial(_mixer_kernel, tile=tile),
        grid=(batch, n_tiles),
        in_specs=[
            pl.BlockSpec((tile, d), lambda b, i: (b * n_tiles + i, 0)),
            const((1, d)),
            pl.BlockSpec((d, HEAD_COLS), lambda b, i: (0, 0), pipeline_mode=pl.Buffered(1)),
            pl.BlockSpec((d, TAIL_COLS), lambda b, i: (0, 0), pipeline_mode=pl.Buffered(1)),
            const(conv_a_w.shape), const(pool_w.shape), const(pool_scale.shape), const(gdn_conv_w.shape),
            const(gdn_norm_g.shape), const(ssm_conv_w.shape), const(ssm_conv_b.shape), const(ssm_d_x.shape),
            const(ssm_norm_g.shape), const(alog_row.shape), const(dtbias_row.shape),
        ],
        out_specs=pl.BlockSpec((tile, D_MODEL), lambda b, i: (b * n_tiles + i, 0)),
        out_shape=jax.ShapeDtypeStruct((batch * seq, D_MODEL), BF16),
        scratch_shapes=[
            pltpu.VMEM((HALO + tile, U_COLS), F32),
            pltpu.VMEM((GDN_HEADS, GDN_HEAD_DIM, GDN_HEAD_DIM), F32),
            pltpu.VMEM((SSM_GROUPS, SSM_STATE, (SSM_HEADS // SSM_GROUPS) * SSM_HEAD_DIM), F32),
        ],
        compiler_params=pltpu.CompilerParams(
            dimension_semantics=("arbitrary", "arbitrary"), vmem_limit_bytes=VMEM_LIMIT_LARGE),
        name="mixers",
    )(x, g_pre.reshape(1, d), w_head, w_tail, conv_a_w, pool_w, pool_scale, gdn_conv_w, gdn_norm_g, ssm_conv_w,
      ssm_conv_b, ssm_d_x, ssm_norm_g, alog_row, dtbias_row)


def _xattn_kernel(x_ref, g_ref, wq_ref, kv_ref, o_ref, wb_ref):
    @pl.when((pl.program_id(0) == 0) & (pl.program_id(1) == 0))
    def _():
        wb_ref[...] = wq_ref[...].astype(BF16)

    xn = _rmsnorm(x_ref[...], g_ref[...]).astype(BF16)
    q = jnp.dot(xn, wb_ref[...], preferred_element_type=F32)
    for h in range(XA_HEADS):
        lo = h * XA_HEAD_DIM
        k_h = kv_ref[:, lo:lo + XA_HEAD_DIM]
        v_h = kv_ref[:, D_MODEL + lo:D_MODEL + lo + XA_HEAD_DIM]
        s = _dot_nt(q[:, lo:lo + XA_HEAD_DIM], k_h) * XA_HEAD_DIM ** -0.5
        e = jnp.exp(s - jnp.max(s, axis=-1, keepdims=True))
        p = e / jnp.sum(e, axis=-1, keepdims=True)
        o_ref[:, lo:lo + XA_HEAD_DIM] = _dot(p, v_h).astype(o_ref.dtype)


def _xattn(x, g, wq, layer, kv, batch, seq, tq):
    n, d = x.shape
    n_tiles = seq // tq
    return pl.pallas_call(
        _xattn_kernel,
        grid=(batch, n_tiles),
        in_specs=[
            pl.BlockSpec((tq, d), lambda b, i: (b * n_tiles + i, 0)),
            pl.BlockSpec((1, d), lambda b, i: (0, 0)),
            pl.BlockSpec((None, d, d), lambda b, i: (layer, 0, 0), pipeline_mode=pl.Buffered(1)),
            pl.BlockSpec((MEM_LEN, 2 * d), lambda b, i: (b, 0)),
        ],
        out_specs=pl.BlockSpec((tq, d), lambda b, i: (b * n_tiles + i, 0)),
        out_shape=jax.ShapeDtypeStruct((n, d), BF16),
        scratch_shapes=[pltpu.VMEM((d, d), BF16)],
        compiler_params=pltpu.CompilerParams(
            dimension_semantics=("arbitrary", "arbitrary"), vmem_limit_bytes=VMEM_LIMIT),
        name="xattn",
    )(x, g.reshape(1, d), wq, kv)


def _ffn_kernel(x_ref, gpre_ref, wg_ref, wu_ref, wd_ref, gpost_ref, o_ref, xn_ref):
    j = pl.program_id(1)

    @pl.when(j == 0)
    def _():
        xn_ref[...] = _rmsnorm(x_ref[...], gpre_ref[...]).astype(BF16)
        o_ref[...] = jnp.zeros_like(o_ref)

    xn = xn_ref[...]
    gate = jnp.dot(xn, wg_ref[...].astype(BF16), preferred_element_type=F32)
    up = jnp.dot(xn, wu_ref[...].astype(BF16), preferred_element_type=F32)
    o_ref[...] += jnp.dot((_silu(gate) * up).astype(BF16), wd_ref[...].astype(BF16), preferred_element_type=F32)

    @pl.when(j == pl.num_programs(1) - 1)
    def _():
        o_ref[...] = x_ref[...] + _rmsnorm(o_ref[...], gpost_ref[...])


def _ffn(x, gpre, w_gu, w_down, layer, gpost, tm, tf):
    n, d = x.shape
    n_ff = D_FF // tf
    return pl.pallas_call(
        _ffn_kernel,
        grid=(n // tm, n_ff),
        in_specs=[
            pl.BlockSpec((tm, d), lambda i, j: (i, 0)),
            pl.BlockSpec((1, d), lambda i, j: (0, 0)),
            pl.BlockSpec((None, d, tf), lambda i, j: (layer, 0, j)),
            pl.BlockSpec((None, d, tf), lambda i, j: (layer, 0, j + n_ff)),
            pl.BlockSpec((None, tf, d), lambda i, j: (layer, j, 0)),
            pl.BlockSpec((1, d), lambda i, j: (0, 0)),
        ],
        out_specs=pl.BlockSpec((tm, d), lambda i, j: (i, 0)),
        out_shape=jax.ShapeDtypeStruct((n, d), F32),
        scratch_shapes=[pltpu.VMEM((tm, d), BF16)],
        compiler_params=pltpu.CompilerParams(
            dimension_semantics=("arbitrary", "arbitrary"), vmem_limit_bytes=VMEM_LIMIT_FFN),
        name="ffn",
    )(x, gpre.reshape(1, d), w_gu, w_gu, w_down, gpost.reshape(1, d))


def _small_row(gdn_vals, ssm_vals):
    row = jnp.zeros((1, 128), F32)
    row = row.at[0, SMALL_A:SMALL_A + GDN_HEADS].set(gdn_vals.astype(F32))
    return row.at[0, SMALL_DT:SMALL_DT + SSM_HEADS].set(ssm_vals.astype(F32))


def kernel(x, mem, norm_g, w_in, conv_a_w, pool_w, pool_scale, gdn_conv_w, gdn_A_log, gdn_dt_bias, gdn_norm_g,
           ssm_conv_w, ssm_conv_b, ssm_A_log, ssm_dt_bias, ssm_D, ssm_norm_g, w_out, xa_wq, xa_wkv, xa_wo,
           ffn_w_gu, ffn_w_down):
    batch, seq, d = x.shape
    depth = w_in.shape[0]
    xf = x.reshape(batch * seq, d)
    memf = mem.reshape(batch * MEM_LEN, d)
    w_in_t = jnp.swapaxes(w_in, 1, 2)
    for l in range(depth):
        g = norm_g[l]
        mixed = _mixers(
            xf, g[0], *_prep_w_in(w_in_t, l), batch, seq, conv_a_w[l], pool_w[l], pool_scale[l].reshape(1, -1),
            gdn_conv_w[l], gdn_norm_g[l].reshape(1, -1), ssm_conv_w[l], ssm_conv_b[l].reshape(1, -1),
            jnp.repeat(ssm_D[l], SSM_HEAD_DIM).reshape(1, -1), ssm_norm_g[l].reshape(1, -1),
            _small_row(gdn_A_log[l], ssm_A_log[l]), _small_row(gdn_dt_bias[l], ssm_dt_bias[l]),
            tile=MIX_TILE)
        xf = _proj_post(mixed, w_out, l, g[1], xf, tm=ROW_TILE)
        kv = _norm_matmul(memf, g[4], xa_wkv, l, 2 * d, BF16, tm=batch * MEM_LEN,
                          tn=2 * KV_COL_TILE if l == 0 else KV_COL_TILE)
        att = _xattn(xf, g[2], xa_wq, l, kv, batch, seq, tq=ROW_TILE)
        xf = _proj_post(att, xa_wo, l, g[3], xf, tm=ROW_TILE)
        xf = _ffn(xf, g[5], ffn_w_gu, ffn_w_down, l, g[6], tm=FFN_ROW_TILE, tf=FFN_COL_TILE)
    return xf.reshape(batch, seq, d)
```

```python
import functools

import jax
import jax.numpy as jnp
from jax import lax
from jax.experimental import pallas as pl
from jax.experimental.pallas import tpu as pltpu

F32 = jnp.float32
BF16 = jnp.bfloat16
EPS = 1e-6

D_MODEL = 2048
GROUP_WIDTH = 512
CHUNK = 64
POOL_WINDOWS = (2, 4, 8, 16)
POOL_GROUP = 128
GDN_HEADS = 4
GDN_HEAD_DIM = 128
SSM_HEADS = 8
SSM_HEAD_DIM = 64
SSM_GROUPS = 2
SSM_STATE = 128
SSM_XBC = 1024
MEM_LEN = 256
XA_HEADS = 4
XA_HEAD_DIM = 512
D_FF = 5632

IN_COLS = 5648
HEAD_COLS = 4096
TAIL_START = 4096
TAIL_VALID = IN_COLS - TAIL_START
TAIL_COLS = 1664
TAIL_BLOCK = 2048
TAIL_SKEW = 8
U_COLS = HEAD_COLS + TAIL_COLS
OFF_A = 0
OFF_B = 1536
OFF_QKV = 2048
OFF_CZ = 3584
OFF_DZ = 4096
OFF_XBC = 4608
OFF_SMALL = 5632
SMALL_DT = 0
SMALL_A = 120
SMALL_B = 124
HALO = 16

MIX_TILE = 4 * CHUNK
MIX_COL_CHUNK = 1024
ROW_TILE = 512
KV_COL_TILE = 512
FFN_ROW_TILE = 1024
FFN_COL_TILE = 256

VMEM_LIMIT = 48 * 1024 * 1024
VMEM_LIMIT_LARGE = 56 * 1024 * 1024
VMEM_LIMIT_FFN = 60000 * 1024


def _rmsnorm(x, g):
    return x * lax.rsqrt(jnp.mean(x * x, axis=-1, keepdims=True) + EPS) * g


def _silu(x):
    return x * jax.nn.sigmoid(x)


def _softplus(x):
    return jnp.maximum(x, 0.0) + jnp.log1p(jnp.exp(-jnp.abs(x)))


def _dot(a, b):
    return jnp.dot(a.astype(BF16), b.astype(BF16), preferred_element_type=F32)


def _dot_nt(a, b):
    return lax.dot_general(a.astype(BF16), b.astype(BF16), (((1,), (1,)), ((), ())),
                           preferred_element_type=F32)


def _dot_tn(a, b):
    return lax.dot_general(a.astype(BF16), b.astype(BF16), (((0,), (0,)), ((), ())),
                           preferred_element_type=F32)


def _split_bf16(a):
    hi = a.astype(BF16)
    return hi, (a - hi.astype(F32)).astype(BF16)


def _dot3(a, b):
    return (jnp.dot(a[0], b[0], preferred_element_type=F32)
            + jnp.dot(a[0], b[1], preferred_element_type=F32)
            + jnp.dot(a[1], b[0], preferred_element_type=F32))


def _norm_matmul_kernel(x_ref, g_ref, w_ref, o_ref, xn_ref):
    @pl.when(pl.program_id(1) == 0)
    def _():
        xn_ref[...] = _rmsnorm(x_ref[...], g_ref[...]).astype(BF16)

    o_ref[...] = jnp.dot(xn_ref[...], w_ref[...].astype(BF16), preferred_element_type=F32).astype(o_ref.dtype)


def _norm_matmul(x, g, w, layer, n_cols, out_dtype, tm, tn):
    n, k = x.shape
    return pl.pallas_call(
        _norm_matmul_kernel,
        grid=(n // tm, n_cols // tn),
        in_specs=[
            pl.BlockSpec((tm, k), lambda i, j: (i, 0)),
            pl.BlockSpec((1, k), lambda i, j: (0, 0)),
            pl.BlockSpec((None, k, tn), lambda i, j: (layer, 0, j)),
        ],
        out_specs=pl.BlockSpec((tm, tn), lambda i, j: (i, j)),
        out_shape=jax.ShapeDtypeStruct((n, n_cols), out_dtype),
        scratch_shapes=[pltpu.VMEM((tm, k), BF16)],
        compiler_params=pltpu.CompilerParams(
            dimension_semantics=("arbitrary", "arbitrary"), vmem_limit_bytes=VMEM_LIMIT_LARGE),
        name="norm_matmul",
    )(x, g.reshape(1, k), w)


PREP_ROWS = 512


def _prep_head_kernel(wt_ref, o_ref):
    o_ref[...] = wt_ref[...].T.astype(BF16)


def _prep_tail_kernel(wt_ref, o_ref):
    dt0 = TAIL_VALID - TAIL_SKEW
    body = TAIL_COLS - 128
    for r in range(0, body, PREP_ROWS):
        o_ref[:, r:r + PREP_ROWS] = wt_ref[TAIL_SKEW + r:TAIL_SKEW + r + PREP_ROWS, :].T.astype(BF16)
    last = jnp.concatenate([wt_ref[dt0:TAIL_VALID, :],
                            jnp.zeros((128 - 2 * TAIL_SKEW, wt_ref.shape[1]), F32),
                            wt_ref[0:TAIL_SKEW, :]], axis=0)
    o_ref[:, body:] = last.T.astype(BF16)


def _prep_w_in(wt, layer):
    k = wt.shape[2]
    head = pl.pallas_call(
        _prep_head_kernel,
        grid=(HEAD_COLS // PREP_ROWS,),
        in_specs=[pl.BlockSpec((None, PREP_ROWS, k), lambda j: (layer, j, 0))],
        out_specs=pl.BlockSpec((k, PREP_ROWS), lambda j: (0, j)),
        out_shape=jax.ShapeDtypeStruct((k, HEAD_COLS), BF16),
        compiler_params=pltpu.CompilerParams(dimension_semantics=("arbitrary",), vmem_limit_bytes=VMEM_LIMIT),
        name="prep_w_in_head",
    )(wt)
    tail = pl.pallas_call(
        _prep_tail_kernel,
        grid=(1,),
        in_specs=[pl.BlockSpec((None, TAIL_BLOCK, k), lambda j: (layer, TAIL_START // TAIL_BLOCK, 0),
                               pipeline_mode=pl.Buffered(1))],
        out_specs=pl.BlockSpec((k, TAIL_COLS), lambda j: (0, 0)),
        out_shape=jax.ShapeDtypeStruct((k, TAIL_COLS), BF16),
        compiler_params=pltpu.CompilerParams(dimension_semantics=("arbitrary",), vmem_limit_bytes=VMEM_LIMIT_LARGE),
        name="prep_w_in_tail",
    )(wt)
    return head, tail


def _proj_post_kernel(a_ref, w_ref, g_ref, x_ref, o_ref, wb_ref):
    @pl.when(pl.program_id(0) == 0)
    def _():
        wb_ref[...] = w_ref[...].astype(BF16)

    y = jnp.dot(a_ref[...], wb_ref[...], preferred_element_type=F32)
    o_ref[...] = x_ref[...] + _rmsnorm(y, g_ref[...])


def _proj_post(a, w, layer, g, x, tm):
    n, k = a.shape
    d = w.shape[2]
    return pl.pallas_call(
        _proj_post_kernel,
        grid=(n // tm,),
        in_specs=[
            pl.BlockSpec((tm, k), lambda i: (i, 0)),
            pl.BlockSpec((None, k, d), lambda i: (layer, 0, 0), pipeline_mode=pl.Buffered(1)),
            pl.BlockSpec((1, d), lambda i: (0, 0)),
            pl.BlockSpec((tm, d), lambda i: (i, 0)),
        ],
        out_specs=pl.BlockSpec((tm, d), lambda i: (i, 0)),
        out_shape=jax.ShapeDtypeStruct((n, d), F32),
        scratch_shapes=[pltpu.VMEM((k, d), BF16)],
        compiler_params=pltpu.CompilerParams(
            dimension_semantics=("arbitrary",), vmem_limit_bytes=VMEM_LIMIT_LARGE),
        name="proj_post",
    )(a, w, g.reshape(1, d), x)


def _shift_rows(x, k):
    return pltpu.roll(x, k, axis=0) if k else x


def _causal_conv(x_ext, w_ref):
    width = w_ref.shape[0]
    if width == 4:
        x1 = _shift_rows(x_ext, 1)
        older = _shift_rows(x_ext * w_ref[1:2, :] + x1 * w_ref[0:1, :], 2)
        return x_ext * w_ref[3:4, :] + x1 * w_ref[2:3, :] + older
    acc = x_ext * w_ref[width - 1:width, :]
    for k in range(width - 1):
        acc = acc + _shift_rows(x_ext, width - 1 - k) * w_ref[k:k + 1, :]
    return acc


def _expand_heads(cols, first, n_heads, width):
    rows = cols.shape[0]
    per_vreg = 128 // width
    lane = lax.broadcasted_iota(jnp.int32, (rows, 128), 1)
    pieces = []
    for m in range(n_heads // per_vreg):
        piece = jnp.broadcast_to(cols[:, first + m * per_vreg:first + m * per_vreg + 1], (rows, 128))
        for j in range(1, per_vreg):
            c = first + m * per_vreg + j
            piece = jnp.where(lane >= j * width, jnp.broadcast_to(cols[:, c:c + 1], (rows, 128)), piece)
        pieces.append(piece)
    return jnp.concatenate(pieces, axis=1)


def _block_diag(y, lane_block, n_blocks):
    return jnp.concatenate([jnp.where(lane_block == b, y, jnp.zeros_like(y)) for b in range(n_blocks)], axis=0)


def _split_block_diag(y, lane_block, n_blocks):
    hi, lo = _split_bf16(y)
    return _block_diag(hi, lane_block, n_blocks), _block_diag(lo, lane_block, n_blocks)


def _unit_lower_inverse(ms, row, col, lane_block, n_blocks):
    same16 = (row >> 4) == (col >> 4)
    same32 = (row >> 5) == (col >> 5)
    off16 = same32 & jnp.logical_not(same16)
    eye = (row == col).astype(F32)
    split = lambda xs: [_split_bf16(x) for x in xs]
    split_bd = lambda xs: [_split_block_diag(x, lane_block, n_blocks) for x in xs]
    d = [jnp.where(same16, m, 0.0) for m in ms]
    d2 = [_dot3(x, y) for x, y in zip(split(d), split_bd(d))]
    p = [eye - x for x in d]
    d2_bd = split_bd(d2)
    p = [x + _dot3(xs, y) for x, xs, y in zip(p, split(p), d2_bd)]
    d4 = [_dot3(x, y) for x, y in zip(split(d2), d2_bd)]
    d4_bd = split_bd(d4)
    p = [x + _dot3(xs, y) for x, xs, y in zip(p, split(p), d4_bd)]
    d8_bd = split_bd([_dot3(x, y) for x, y in zip(split(d4), d4_bd)])
    p = [x + _dot3(xs, y) for x, xs, y in zip(p, split(p), d8_bd)]
    ps, p_bd = split(p), split_bd(p)
    t = split([_dot3(xs, _split_block_diag(jnp.where(off16, m, 0.0), lane_block, n_blocks)) for xs, m in zip(ps, ms)])
    q = [x - _dot3(y, xb) for x, xb, y in zip(p, p_bd, t)]
    qs, q_bd = split(q), split_bd(q)
    t = split([_dot3(xs, _split_block_diag(jnp.where(same32, 0.0, m), lane_block, n_blocks)) for xs, m in zip(qs, ms)])
    return [x - _dot3(y, xb) for x, xb, y in zip(q, q_bd, t)]


def _dot_ltri(ltri, x):
    x1 = x.astype(BF16)
    r1 = x - x1.astype(F32)
    x2 = r1.astype(BF16)
    x3 = (r1 - x2.astype(F32)).astype(BF16)
    return (jnp.dot(ltri, x1, preferred_element_type=F32) + jnp.dot(ltri, x2, preferred_element_type=F32)
            + jnp.dot(ltri, x3, preferred_element_type=F32))


def _mixer_kernel(x_ref, gpre_ref, w_head_ref, w_tail_ref, conv_a_ref, pool_w_ref, pool_scale_ref, gdn_conv_ref,
                  gdn_norm_ref, ssm_conv_ref, ssm_bias_ref, ssm_d_ref, ssm_norm_ref,
                  alog_ref, dtbias_ref, out_ref, u_ref, s_ref, h_ref, *, tile):
    i = pl.program_id(1)
    n_chunks = tile // CHUNK
    hg = SSM_HEADS // SSM_GROUPS
    gw = hg * SSM_HEAD_DIM

    @pl.when(i == 0)
    def _():
        s_ref[...] = jnp.zeros_like(s_ref)
        h_ref[...] = jnp.zeros_like(h_ref)
        u_ref[0:HALO, :] = jnp.zeros((HALO, U_COLS), F32)

    @pl.when(i > 0)
    def _():
        u_ref[0:HALO, :] = u_ref[tile:tile + HALO, :]

    xn = _rmsnorm(x_ref[...], gpre_ref[...]).astype(BF16)
    for c0 in range(0, HEAD_COLS, MIX_COL_CHUNK):
        u_ref[HALO:, c0:c0 + MIX_COL_CHUNK] = jnp.dot(xn, w_head_ref[:, c0:c0 + MIX_COL_CHUNK],
                                                        preferred_element_type=F32)
    u_ref[HALO:, HEAD_COLS:] = jnp.dot(xn, w_tail_ref[...], preferred_element_type=F32)

    def cur(off, width, rows=slice(0, tile)):
        return u_ref[HALO + rows.start:HALO + rows.stop, off:off + width]

    def ext(off, width):
        return u_ref[:, off:off + width]

    p = ext(OFF_A + GROUP_WIDTH, GROUP_WIDTH) * ext(OFF_A + 2 * GROUP_WIDTH, GROUP_WIDTH)
    ya = cur(OFF_A, GROUP_WIDTH) * _causal_conv(p, conv_a_ref)[HALO:]
    out_ref[:, 0:GROUP_WIDTH] = ya.astype(out_ref.dtype)

    pos = (i * tile + 1 + lax.broadcasted_iota(jnp.int32, (tile, 1), 0)).astype(F32)
    for gi, window in enumerate(POOL_WINDOWS):
        off = OFF_B + gi * POOL_GROUP
        s = ext(off, POOL_GROUP)
        k = 1
        while k < window:
            s = s + _shift_rows(s, k)
            k *= 2
        pooled = s[HALO:] / jnp.minimum(pos, float(window)) - cur(off, POOL_GROUP)
        yb = _dot(pooled, pool_w_ref[gi]) * pool_scale_ref[:, gi * POOL_GROUP:(gi + 1) * POOL_GROUP]
        out_ref[:, GROUP_WIDTH + gi * POOL_GROUP:GROUP_WIDTH + (gi + 1) * POOL_GROUP] = yb.astype(out_ref.dtype)

    small = cur(OFF_SMALL, 128)
    sp = _softplus(small + dtbias_ref[...])
    rate = -jnp.exp(alog_ref[...]) * sp
    beta_all = jax.nn.sigmoid(small)

    qkv = _silu(_causal_conv(ext(OFF_QKV, 3 * GROUP_WIDTH), gdn_conv_ref)[HALO:])
    xbc = _silu(_causal_conv(ext(OFF_XBC, SSM_XBC), ssm_conv_ref)[HALO:] + ssm_bias_ref[...])

    n4 = GDN_HEADS * CHUNK
    row = lax.broadcasted_iota(jnp.int32, (CHUNK, n4), 0)
    lane = lax.broadcasted_iota(jnp.int32, (CHUNK, n4), 1)
    col = lane & (CHUNK - 1)
    lane_block = lane >> 6
    incl = row >= col
    strict = row > col
    ltri = (lax.broadcasted_iota(jnp.int32, (CHUNK, CHUNK), 0)
            >= lax.broadcasted_iota(jnp.int32, (CHUNK, CHUNK), 1)).astype(BF16)

    def decay_blocks(rate_c, first, n_heads):
        out = []
        for m in range(n_heads // GDN_HEADS):
            r = _expand_heads(rate_c, first + m * GDN_HEADS, GDN_HEADS, CHUNK)
            diff = _dot_ltri(ltri, jnp.where(strict, r, 0.0))
            out.append(jnp.where(incl, jnp.exp(jnp.where(incl, diff, 0.0)), 0.0))
        return out

    chunks = range(n_chunks)
    rows = [slice(c * CHUNK, (c + 1) * CHUNK) for c in chunks]
    acs = [_dot_ltri(ltri, rate[r]) for r in rows]
    e_acs = [jnp.exp(a) for a in acs]
    e_end = [jnp.exp(a[CHUNK - 1:CHUNK, :] - a) for a in acs]
    e_blk = [jnp.exp(a[CHUNK - 1:CHUNK, :]) for a in acs]

    def head_cols(base, h):
        return slice(base + h * GDN_HEAD_DIM, base + (h + 1) * GDN_HEAD_DIM)

    def l2norm_heads(x, scale):
        return jnp.concatenate(
            [x[:, head_cols(0, h)] * (lax.rsqrt(jnp.sum(x[:, head_cols(0, h)] * x[:, head_cols(0, h)],
                                                        axis=-1, keepdims=True) + EPS) * scale)
             for h in range(GDN_HEADS)], axis=1)

    zeros_head = jnp.zeros((CHUNK, GDN_HEAD_DIM), BF16)
    q_all = [l2norm_heads(qkv[r, 0:GROUP_WIDTH], GDN_HEAD_DIM ** -0.5) for r in rows]
    k_all = [l2norm_heads(qkv[r, GROUP_WIDTH:2 * GROUP_WIDTH], 1.0) for r in rows]
    v_all = [qkv[r, 2 * GROUP_WIDTH:3 * GROUP_WIDTH] for r in rows]
    beta_x = [_expand_heads(beta_all[r], SMALL_B, GDN_HEADS, GDN_HEAD_DIM) for r in rows]
    e_g_x = [_expand_heads(e, SMALL_A, GDN_HEADS, GDN_HEAD_DIM) for e in e_acs]
    kb = [x * b for x, b in zip(k_all, beta_x)]
    k_rows = []
    for x in k_all:
        xb = x.astype(BF16)
        k_rows.append(jnp.concatenate(
            [jnp.concatenate([xb[:, head_cols(0, h)] if h == g else zeros_head for h in range(GDN_HEADS)], axis=1)
             for g in range(GDN_HEADS)], axis=0))
    decay = [decay_blocks(rate[r], SMALL_A, GDN_HEADS)[0] for r in rows]
    m_low = [jnp.where(strict, _dot_nt(x, y) * dc, 0.0) for x, y, dc in zip(kb, k_rows, decay)]
    qk = [_dot_nt(x, y) * dc for x, y, dc in zip(q_all, k_rows, decay)]
    t_inv = _unit_lower_inverse(m_low, row, col, lane_block, GDN_HEADS)
    zeros_rhs = jnp.zeros((CHUNK, 2 * GDN_HEAD_DIM), F32)
    sol = []
    for c in chunks:
        vb, kg = v_all[c] * beta_x[c], kb[c] * e_g_x[c]
        rhs = jnp.concatenate(
            [jnp.concatenate([jnp.concatenate([vb[:, head_cols(0, h)], kg[:, head_cols(0, h)]], axis=1)
                              if h == g else zeros_rhs for h in range(GDN_HEADS)], axis=1)
             for g in range(GDN_HEADS)], axis=0)
        sol.append(_dot3(_split_bf16(t_inv[c]), _split_bf16(rhs)))
    q_dec = [x * e for x, e in zip(q_all, e_g_x)]
    k_end = [x * _expand_heads(e, SMALL_A, GDN_HEADS, GDN_HEAD_DIM) for x, e in zip(k_all, e_end)]

    x_in, y_diag, s_inc, c_mat, e_acs_x, e_blk_x = [], [], [], [], [], []
    for c in chunks:
        r = rows[c]
        x_in.append(xbc[r, :GROUP_WIDTH])
        x_dt = x_in[c] * _expand_heads(sp[r], SMALL_DT, SSM_HEADS, SSM_HEAD_DIM)
        x_end = x_dt * _expand_heads(e_end[c], SMALL_DT, SSM_HEADS, SSM_HEAD_DIM)
        e_acs_x.append(_expand_heads(e_acs[c], SMALL_DT, SSM_HEADS, SSM_HEAD_DIM))
        e_blk_x.append(_expand_heads(e_blk[c], SMALL_DT, SSM_HEADS, SSM_HEAD_DIM))
        l_m = decay_blocks(rate[r], SMALL_DT, SSM_HEADS)
        for g in range(SSM_GROUPS):
            b_g = xbc[r, GROUP_WIDTH + g * SSM_STATE:GROUP_WIDTH + (g + 1) * SSM_STATE]
            c_g = xbc[r, GROUP_WIDTH + (SSM_GROUPS + g) * SSM_STATE:GROUP_WIDTH + (SSM_GROUPS + g + 1) * SSM_STATE]
            b_bf = b_g.astype(BF16)
            cb = _dot_nt(c_g, jnp.concatenate([b_bf] * hg, axis=0))
            x_g = x_dt[:, g * gw:(g + 1) * gw]
            y_diag.append(_dot(cb * l_m[g], _block_diag(x_g.astype(BF16), lane_block, hg)))
            s_inc.append(_dot_tn(b_bf, x_end[:, g * gw:(g + 1) * gw]))
            c_mat.append(c_g)

    s_state = [s_ref[h] for h in range(GDN_HEADS)]
    h_state = [h_ref[g] for g in range(SSM_GROUPS)]
    zeros_v = jnp.zeros((CHUNK, GDN_HEAD_DIM), BF16)
    for c in chunks:
        r = rows[c]
        hd = 2 * GDN_HEAD_DIM
        v_new = [sol[c][:, h * hd:h * hd + GDN_HEAD_DIM]
                 - _dot(sol[c][:, h * hd + GDN_HEAD_DIM:(h + 1) * hd], s_state[h]) for h in range(GDN_HEADS)]
        v_rows = jnp.concatenate(
            [jnp.concatenate([v_new[h].astype(BF16) if h == g else zeros_v for h in range(GDN_HEADS)], axis=1)
             for g in range(GDN_HEADS)], axis=0)
        o_intra = _dot(qk[c], v_rows)
        o = [_dot(q_dec[c][:, head_cols(0, h)], s_state[h]) + o_intra[:, head_cols(0, h)]
             for h in range(GDN_HEADS)]
        s_state = [s_state[h] * e_blk[c][:, SMALL_A + h:SMALL_A + h + 1]
                   + _dot_tn(k_end[c][:, head_cols(0, h)], v_new[h]) for h in range(GDN_HEADS)]
        for h in range(GDN_HEADS):
            z = cur(OFF_CZ + h * GDN_HEAD_DIM, GDN_HEAD_DIM, r)
            y = _rmsnorm(o[h], gdn_norm_ref[...]) * _silu(z)
            out_ref[r, head_cols(2 * GROUP_WIDTH, h)] = y.astype(out_ref.dtype)

        ys = []
        for g in range(SSM_GROUPS):
            n = c * SSM_GROUPS + g
            ys.append(y_diag[n] + _dot(c_mat[n], h_state[g]) * e_acs_x[c][:, g * gw:(g + 1) * gw])
            h_state[g] = h_state[g] * e_blk_x[c][:, g * gw:(g + 1) * gw] + s_inc[n]
        y = jnp.concatenate(ys, axis=1) + ssm_d_ref[...] * x_in[c]
        y = y * _silu(cur(OFF_DZ, GROUP_WIDTH, r))
        for g in range(SSM_GROUPS):
            yn = _rmsnorm(y[:, g * gw:(g + 1) * gw], ssm_norm_ref[:, g * gw:(g + 1) * gw])
            out_ref[r, 3 * GROUP_WIDTH + g * gw:3 * GROUP_WIDTH + (g + 1) * gw] = yn.astype(out_ref.dtype)
    for h in range(GDN_HEADS):
        s_ref[h] = s_state[h]
    for g in range(SSM_GROUPS):
        h_ref[g] = h_state[g]


def _mixers(x, g_pre, w_head, w_tail, batch, seq, conv_a_w, pool_w, pool_scale, gdn_conv_w, gdn_norm_g, ssm_conv_w,
            ssm_conv_b, ssm_d_x, ssm_norm_g, alog_row, dtbias_row, tile):
    n_tiles = seq // tile
    d = x.shape[1]

    def const(shape):
        return pl.BlockSpec(shape, lambda b, i: (0,) * len(shape))

    return pl.pallas_call(
        functools.partial(_mixer_kernel, tile=tile),
        grid=(batch, n_tiles),
        in_specs=[
            pl.BlockSpec((tile, d), lambda b, i: (b * n_tiles + i, 0)),
            const((1, d)),
            pl.BlockSpec((d, HEAD_COLS), lambda b, i: (0, 0), pipeline_mode=pl.Buffered(1)),
            pl.BlockSpec((d, TAIL_COLS), lambda b, i: (0, 0), pipeline_mode=pl.Buffered(1)),
            const(conv_a_w.shape), const(pool_w.shape), const(pool_scale.shape), const(gdn_conv_w.shape),
            const(gdn_norm_g.shape), const(ssm_conv_w.shape), const(ssm_conv_b.shape), const(ssm_d_x.shape),
            const(ssm_norm_g.shape), const(alog_row.shape), const(dtbias_row.shape),
        ],
        out_specs=pl.BlockSpec((tile, D_MODEL), lambda b, i: (b * n_tiles + i, 0)),
        out_shape=jax.ShapeDtypeStruct((batch * seq, D_MODEL), BF16),
        scratch_shapes=[
            pltpu.VMEM((HALO + tile, U_COLS), F32),
            pltpu.VMEM((GDN_HEADS, GDN_HEAD_DIM, GDN_HEAD_DIM), F32),
            pltpu.VMEM((SSM_GROUPS, SSM_STATE, (SSM_HEADS // SSM_GROUPS) * SSM_HEAD_DIM), F32),
        ],
        compiler_params=pltpu.CompilerParams(
            dimension_semantics=("arbitrary", "arbitrary"), vmem_limit_bytes=VMEM_LIMIT_LARGE),
        name="mixers",
    )(x, g_pre.reshape(1, d), w_head, w_tail, conv_a_w, pool_w, pool_scale, gdn_conv_w, gdn_norm_g, ssm_conv_w,
      ssm_conv_b, ssm_d_x, ssm_norm_g, alog_row, dtbias_row)


def _xattn_kernel(x_ref, g_ref, wq_ref, kv_ref, o_ref, wb_ref):
    @pl.when((pl.program_id(0) == 0) & (pl.program_id(1) == 0))
    def _():
        wb_ref[...] = wq_ref[...].astype(BF16)

    xn = _rmsnorm(x_ref[...], g_ref[...]).astype(BF16)
    q = jnp.dot(xn, wb_ref[...], preferred_element_type=F32)
    for h in range(XA_HEADS):
        lo = h * XA_HEAD_DIM
        k_h = kv_ref[:, lo:lo + XA_HEAD_DIM]
        v_h = kv_ref[:, D_MODEL + lo:D_MODEL + lo + XA_HEAD_DIM]
        s = _dot_nt(q[:, lo:lo + XA_HEAD_DIM], k_h) * XA_HEAD_DIM ** -0.5
        e = jnp.exp(s - jnp.max(s, axis=-1, keepdims=True))
        p = e / jnp.sum(e, axis=-1, keepdims=True)
        o_ref[:, lo:lo + XA_HEAD_DIM] = _dot(p, v_h).astype(o_ref.dtype)


def _xattn(x, g, wq, layer, kv, batch, seq, tq):
    n, d = x.shape
    n_tiles = seq // tq
    return pl.pallas_call(
        _xattn_kernel,
        grid=(batch, n_tiles),
        in_specs=[
            pl.BlockSpec((tq, d), lambda b, i: (b * n_tiles + i, 0)),
            pl.BlockSpec((1, d), lambda b, i: (0, 0)),
            pl.BlockSpec((None, d, d), lambda b, i: (layer, 0, 0), pipeline_mode=pl.Buffered(1)),
            pl.BlockSpec((MEM_LEN, 2 * d), lambda b, i: (b, 0)),
        ],
        out_specs=pl.BlockSpec((tq, d), lambda b, i: (b * n_tiles + i, 0)),
        out_shape=jax.ShapeDtypeStruct((n, d), BF16),
        scratch_shapes=[pltpu.VMEM((d, d), BF16)],
        compiler_params=pltpu.CompilerParams(
            dimension_semantics=("arbitrary", "arbitrary"), vmem_limit_bytes=VMEM_LIMIT),
        name="xattn",
    )(x, g.reshape(1, d), wq, kv)


def _ffn_kernel(x_ref, gpre_ref, wg_ref, wu_ref, wd_ref, gpost_ref, o_ref, xn_ref):
    j = pl.program_id(1)

    @pl.when(j == 0)
    def _():
        xn_ref[...] = _rmsnorm(x_ref[...], gpre_ref[...]).astype(BF16)
        o_ref[...] = jnp.zeros_like(o_ref)

    xn = xn_ref[...]
    gate = jnp.dot(xn, wg_ref[...].astype(BF16), preferred_element_type=F32)
    up = jnp.dot(xn, wu_ref[...].astype(BF16), preferred_element_type=F32)
    o_ref[...] += jnp.dot((_silu(gate) * up).astype(BF16), wd_ref[...].astype(BF16), preferred_element_type=F32)

    @pl.when(j == pl.num_programs(1) - 1)
    def _():
        o_ref[...] = x_ref[...] + _rmsnorm(o_ref[...], gpost_ref[...])


def _ffn(x, gpre, w_gu, w_down, layer, gpost, tm, tf):
    n, d = x.shape
    n_ff = D_FF // tf
    return pl.pallas_call(
        _ffn_kernel,
        grid=(n // tm, n_ff),
        in_specs=[
            pl.BlockSpec((tm, d), lambda i, j: (i, 0)),
            pl.BlockSpec((1, d), lambda i, j: (0, 0)),
            pl.BlockSpec((None, d, tf), lambda i, j: (layer, 0, j)),
            pl.BlockSpec((None, d, tf), lambda i, j: (layer, 0, j + n_ff)),
            pl.BlockSpec((None, tf, d), lambda i, j: (layer, j, 0)),
            pl.BlockSpec((1, d), lambda i, j: (0, 0)),
        ],
        out_specs=pl.BlockSpec((tm, d), lambda i, j: (i, 0)),
        out_shape=jax.ShapeDtypeStruct((n, d), F32),
        scratch_shapes=[pltpu.VMEM((tm, d), BF16)],
        compiler_params=pltpu.CompilerParams(
            dimension_semantics=("arbitrary", "arbitrary"), vmem_limit_bytes=VMEM_LIMIT_FFN),
        name="ffn",
    )(x, gpre.reshape(1, d), w_gu, w_gu, w_down, gpost.reshape(1, d))


def _small_row(gdn_vals, ssm_vals):
    row = jnp.zeros((1, 128), F32)
    row = row.at[0, SMALL_A:SMALL_A + GDN_HEADS].set(gdn_vals.astype(F32))
    return row.at[0, SMALL_DT:SMALL_DT + SSM_HEADS].set(ssm_vals.astype(F32))


def kernel(x, mem, norm_g, w_in, conv_a_w, pool_w, pool_scale, gdn_conv_w, gdn_A_log, gdn_dt_bias, gdn_norm_g,
           ssm_conv_w, ssm_conv_b, ssm_A_log, ssm_dt_bias, ssm_D, ssm_norm_g, w_out, xa_wq, xa_wkv, xa_wo,
           ffn_w_gu, ffn_w_down):
    batch, seq, d = x.shape
    depth = w_in.shape[0]
    xf = x.reshape(batch * seq, d)
    memf = mem.reshape(batch * MEM_LEN, d)
    w_in_t = jnp.swapaxes(w_in, 1, 2)
    for l in range(depth):
        g = norm_g[l]
        mixed = _mixers(
            xf, g[0], *_prep_w_in(w_in_t, l), batch, seq, conv_a_w[l], pool_w[l], pool_scale[l].reshape(1, -1),
            gdn_conv_w[l], gdn_norm_g[l].reshape(1, -1), ssm_conv_w[l], ssm_conv_b[l].reshape(1, -1),
            jnp.repeat(ssm_D[l], SSM_HEAD_DIM).reshape(1, -1), ssm_norm_g[l].reshape(1, -1),
            _small_row(gdn_A_log[l], ssm_A_log[l]), _small_row(gdn_dt_bias[l], ssm_dt_bias[l]),
            tile=MIX_TILE)
        xf = _proj_post(mixed, w_out, l, g[1], xf, tm=ROW_TILE)
        kv = _norm_matmul(memf, g[4], xa_wkv, l, 2 * d, BF16, tm=batch * MEM_LEN, tn=KV_COL_TILE)
        att = _xattn(xf, g[2], xa_wq, l, kv, batch, seq, tq=ROW_TILE)
        xf = _proj_post(att, xa_wo, l, g[3], xf, tm=ROW_TILE)
        xf = _ffn(xf, g[5], ffn_w_gu, ffn_w_down, l, g[6], tm=FFN_ROW_TILE, tf=FFN_COL_TILE)
    return xf.reshape(batch, seq, d)
```

```python
import functools

import jax
import jax.numpy as jnp
from jax import lax
from jax.experimental import pallas as pl
from jax.experimental.pallas import tpu as pltpu

F32 = jnp.float32
BF16 = jnp.bfloat16
EPS = 1e-6

D_MODEL = 2048
GROUP_WIDTH = 512
CHUNK = 64
POOL_WINDOWS = (2, 4, 8, 16)
POOL_GROUP = 128
GDN_HEADS = 4
GDN_HEAD_DIM = 128
SSM_HEADS = 8
SSM_HEAD_DIM = 64
SSM_GROUPS = 2
SSM_STATE = 128
SSM_XBC = 1024
MEM_LEN = 256
XA_HEADS = 4
XA_HEAD_DIM = 512
D_FF = 5632

IN_COLS = 5648
HEAD_COLS = 4096
TAIL_START = 4096
TAIL_VALID = IN_COLS - TAIL_START
TAIL_COLS = 1664
TAIL_BLOCK = 2048
TAIL_SKEW = 8
U_COLS = HEAD_COLS + TAIL_COLS
OFF_A = 0
OFF_B = 1536
OFF_QKV = 2048
OFF_CZ = 3584
OFF_DZ = 4096
OFF_XBC = 4608
OFF_SMALL = 5632
SMALL_DT = 0
SMALL_A = 120
SMALL_B = 124
HALO = 16

MIX_TILE = 4 * CHUNK
MIX_COL_CHUNK = 1024
ROW_TILE = 512
KV_COL_TILE = 512
FFN_ROW_TILE = 1024
FFN_COL_TILE = 256

VMEM_LIMIT = 48 * 1024 * 1024
VMEM_LIMIT_LARGE = 56 * 1024 * 1024
VMEM_LIMIT_FFN = 60000 * 1024


def _rmsnorm(x, g):
    return x * lax.rsqrt(jnp.mean(x * x, axis=-1, keepdims=True) + EPS) * g


def _silu(x):
    return x * jax.nn.sigmoid(x)


def _softplus(x):
    return jnp.maximum(x, 0.0) + jnp.log1p(jnp.exp(-jnp.abs(x)))


def _dot(a, b):
    return jnp.dot(a.astype(BF16), b.astype(BF16), preferred_element_type=F32)


def _dot_nt(a, b):
    return lax.dot_general(a.astype(BF16), b.astype(BF16), (((1,), (1,)), ((), ())),
                           preferred_element_type=F32)


def _dot_tn(a, b):
    return lax.dot_general(a.astype(BF16), b.astype(BF16), (((0,), (0,)), ((), ())),
                           preferred_element_type=F32)


def _split_bf16(a):
    hi = a.astype(BF16)
    return hi, (a - hi.astype(F32)).astype(BF16)


def _dot3(a, b):
    return (jnp.dot(a[0], b[0], preferred_element_type=F32)
            + jnp.dot(a[0], b[1], preferred_element_type=F32)
            + jnp.dot(a[1], b[0], preferred_element_type=F32))


def _norm_matmul_kernel(x_ref, g_ref, w_ref, o_ref, xn_ref):
    @pl.when(pl.program_id(1) == 0)
    def _():
        xn_ref[...] = _rmsnorm(x_ref[...], g_ref[...]).astype(BF16)

    o_ref[...] = jnp.dot(xn_ref[...], w_ref[...].astype(BF16), preferred_element_type=F32).astype(o_ref.dtype)


def _norm_matmul(x, g, w, layer, n_cols, out_dtype, tm, tn):
    n, k = x.shape
    return pl.pallas_call(
        _norm_matmul_kernel,
        grid=(n // tm, n_cols // tn),
        in_specs=[
            pl.BlockSpec((tm, k), lambda i, j: (i, 0)),
            pl.BlockSpec((1, k), lambda i, j: (0, 0)),
            pl.BlockSpec((None, k, tn), lambda i, j: (layer, 0, j)),
        ],
        out_specs=pl.BlockSpec((tm, tn), lambda i, j: (i, j)),
        out_shape=jax.ShapeDtypeStruct((n, n_cols), out_dtype),
        scratch_shapes=[pltpu.VMEM((tm, k), BF16)],
        compiler_params=pltpu.CompilerParams(
            dimension_semantics=("arbitrary", "arbitrary"), vmem_limit_bytes=VMEM_LIMIT_LARGE),
        name="norm_matmul",
    )(x, g.reshape(1, k), w)


PREP_ROWS = 512


def _prep_head_kernel(wt_ref, o_ref):
    o_ref[...] = wt_ref[...].T.astype(BF16)


def _prep_tail_kernel(wt_ref, o_ref):
    dt0 = TAIL_VALID - TAIL_SKEW
    body = TAIL_COLS - 128
    for r in range(0, body, PREP_ROWS):
        o_ref[:, r:r + PREP_ROWS] = wt_ref[TAIL_SKEW + r:TAIL_SKEW + r + PREP_ROWS, :].T.astype(BF16)
    last = jnp.concatenate([wt_ref[dt0:TAIL_VALID, :],
                            jnp.zeros((128 - 2 * TAIL_SKEW, wt_ref.shape[1]), F32),
                            wt_ref[0:TAIL_SKEW, :]], axis=0)
    o_ref[:, body:] = last.T.astype(BF16)


def _prep_w_in(wt, layer):
    k = wt.shape[2]
    head = pl.pallas_call(
        _prep_head_kernel,
        grid=(HEAD_COLS // PREP_ROWS,),
        in_specs=[pl.BlockSpec((None, PREP_ROWS, k), lambda j: (layer, j, 0))],
        out_specs=pl.BlockSpec((k, PREP_ROWS), lambda j: (0, j)),
        out_shape=jax.ShapeDtypeStruct((k, HEAD_COLS), BF16),
        compiler_params=pltpu.CompilerParams(dimension_semantics=("arbitrary",), vmem_limit_bytes=VMEM_LIMIT),
        name="prep_w_in_head",
    )(wt)
    tail = pl.pallas_call(
        _prep_tail_kernel,
        grid=(1,),
        in_specs=[pl.BlockSpec((None, TAIL_BLOCK, k), lambda j: (layer, TAIL_START // TAIL_BLOCK, 0),
                               pipeline_mode=pl.Buffered(1))],
        out_specs=pl.BlockSpec((k, TAIL_COLS), lambda j: (0, 0)),
        out_shape=jax.ShapeDtypeStruct((k, TAIL_COLS), BF16),
        compiler_params=pltpu.CompilerParams(dimension_semantics=("arbitrary",), vmem_limit_bytes=VMEM_LIMIT_LARGE),
        name="prep_w_in_tail",
    )(wt)
    return head, tail


def _proj_post_kernel(a_ref, w_ref, g_ref, x_ref, o_ref, wb_ref):
    @pl.when(pl.program_id(0) == 0)
    def _():
        wb_ref[...] = w_ref[...].astype(BF16)

    y = jnp.dot(a_ref[...], wb_ref[...], preferred_element_type=F32)
    o_ref[...] = x_ref[...] + _rmsnorm(y, g_ref[...])


def _proj_post(a, w, layer, g, x, tm):
    n, k = a.shape
    d = w.shape[2]
    return pl.pallas_call(
        _proj_post_kernel,
        grid=(n // tm,),
        in_specs=[
            pl.BlockSpec((tm, k), lambda i: (i, 0)),
            pl.BlockSpec((None, k, d), lambda i: (layer, 0, 0), pipeline_mode=pl.Buffered(1)),
            pl.BlockSpec((1, d), lambda i: (0, 0)),
            pl.BlockSpec((tm, d), lambda i: (i, 0)),
        ],
        out_specs=pl.BlockSpec((tm, d), lambda i: (i, 0)),
        out_shape=jax.ShapeDtypeStruct((n, d), F32),
        scratch_shapes=[pltpu.VMEM((k, d), BF16)],
        compiler_params=pltpu.CompilerParams(
            dimension_semantics=("arbitrary",), vmem_limit_bytes=VMEM_LIMIT_LARGE),
        name="proj_post",
    )(a, w, g.reshape(1, d), x)


def _shift_rows(x, k):
    return pltpu.roll(x, k, axis=0) if k else x


def _causal_conv(x_ext, w_ref):
    width = w_ref.shape[0]
    if width == 4:
        x1 = _shift_rows(x_ext, 1)
        older = _shift_rows(x_ext * w_ref[1:2, :] + x1 * w_ref[0:1, :], 2)
        return x_ext * w_ref[3:4, :] + x1 * w_ref[2:3, :] + older
    acc = x_ext * w_ref[width - 1:width, :]
    for k in range(width - 1):
        acc = acc + _shift_rows(x_ext, width - 1 - k) * w_ref[k:k + 1, :]
    return acc


def _expand_heads(cols, first, n_heads, width):
    rows = cols.shape[0]
    per_vreg = 128 // width
    lane = lax.broadcasted_iota(jnp.int32, (rows, 128), 1)
    pieces = []
    for m in range(n_heads // per_vreg):
        piece = jnp.broadcast_to(cols[:, first + m * per_vreg:first + m * per_vreg + 1], (rows, 128))
        for j in range(1, per_vreg):
            c = first + m * per_vreg + j
            piece = jnp.where(lane >= j * width, jnp.broadcast_to(cols[:, c:c + 1], (rows, 128)), piece)
        pieces.append(piece)
    return jnp.concatenate(pieces, axis=1)


def _block_diag(y, lane_block, n_blocks):
    return jnp.concatenate([jnp.where(lane_block == b, y, jnp.zeros_like(y)) for b in range(n_blocks)], axis=0)


def _split_block_diag(y, lane_block, n_blocks):
    hi, lo = _split_bf16(y)
    return _block_diag(hi, lane_block, n_blocks), _block_diag(lo, lane_block, n_blocks)


def _unit_lower_inverse(ms, row, col, lane_block, n_blocks):
    same16 = (row >> 4) == (col >> 4)
    same32 = (row >> 5) == (col >> 5)
    off16 = same32 & jnp.logical_not(same16)
    eye = (row == col).astype(F32)
    split = lambda xs: [_split_bf16(x) for x in xs]
    split_bd = lambda xs: [_split_block_diag(x, lane_block, n_blocks) for x in xs]
    d = [jnp.where(same16, m, 0.0) for m in ms]
    d2 = [_dot3(x, y) for x, y in zip(split(d), split_bd(d))]
    p = [eye - x for x in d]
    d2_bd = split_bd(d2)
    p = [x + _dot3(xs, y) for x, xs, y in zip(p, split(p), d2_bd)]
    d4 = [_dot3(x, y) for x, y in zip(split(d2), d2_bd)]
    d4_bd = split_bd(d4)
    p = [x + _dot3(xs, y) for x, xs, y in zip(p, split(p), d4_bd)]
    d8_bd = split_bd([_dot3(x, y) for x, y in zip(split(d4), d4_bd)])
    p = [x + _dot3(xs, y) for x, xs, y in zip(p, split(p), d8_bd)]
    ps, p_bd = split(p), split_bd(p)
    t = split([_dot3(xs, _split_block_diag(jnp.where(off16, m, 0.0), lane_block, n_blocks)) for xs, m in zip(ps, ms)])
    q = [x - _dot3(y, xb) for x, xb, y in zip(p, p_bd, t)]
    qs, q_bd = split(q), split_bd(q)
    t = split([_dot3(xs, _split_block_diag(jnp.where(same32, 0.0, m), lane_block, n_blocks)) for xs, m in zip(qs, ms)])
    return [x - _dot3(y, xb) for x, xb, y in zip(q, q_bd, t)]


def _dot_ltri(ltri, x):
    x1 = x.astype(BF16)
    r1 = x - x1.astype(F32)
    x2 = r1.astype(BF16)
    x3 = (r1 - x2.astype(F32)).astype(BF16)
    return (jnp.dot(ltri, x1, preferred_element_type=F32) + jnp.dot(ltri, x2, preferred_element_type=F32)
            + jnp.dot(ltri, x3, preferred_element_type=F32))


def _mixer_kernel(x_ref, gpre_ref, w_head_ref, w_tail_ref, conv_a_ref, pool_w_ref, pool_scale_ref, gdn_conv_ref,
                  gdn_norm_ref, ssm_conv_ref, ssm_bias_ref, ssm_d_ref, ssm_norm_ref,
                  alog_ref, dtbias_ref, out_ref, u_ref, s_ref, h_ref, *, tile):
    i = pl.program_id(1)
    n_chunks = tile // CHUNK
    hg = SSM_HEADS // SSM_GROUPS
    gw = hg * SSM_HEAD_DIM

    @pl.when(i == 0)
    def _():
        s_ref[...] = jnp.zeros_like(s_ref)
        h_ref[...] = jnp.zeros_like(h_ref)
        u_ref[0:HALO, :] = jnp.zeros((HALO, U_COLS), F32)

    @pl.when(i > 0)
    def _():
        u_ref[0:HALO, :] = u_ref[tile:tile + HALO, :]

    xn = _rmsnorm(x_ref[...], gpre_ref[...]).astype(BF16)
    for c0 in range(0, HEAD_COLS, MIX_COL_CHUNK):
        u_ref[HALO:, c0:c0 + MIX_COL_CHUNK] = jnp.dot(xn, w_head_ref[:, c0:c0 + MIX_COL_CHUNK],
                                                        preferred_element_type=F32)
    u_ref[HALO:, HEAD_COLS:] = jnp.dot(xn, w_tail_ref[...], preferred_element_type=F32)

    def cur(off, width, rows=slice(0, tile)):
        return u_ref[HALO + rows.start:HALO + rows.stop, off:off + width]

    def ext(off, width):
        return u_ref[:, off:off + width]

    p = ext(OFF_A + GROUP_WIDTH, GROUP_WIDTH) * ext(OFF_A + 2 * GROUP_WIDTH, GROUP_WIDTH)
    ya = cur(OFF_A, GROUP_WIDTH) * _causal_conv(p, conv_a_ref)[HALO:]
    out_ref[:, 0:GROUP_WIDTH] = ya.astype(out_ref.dtype)

    pos = (i * tile + 1 + lax.broadcasted_iota(jnp.int32, (tile, 1), 0)).astype(F32)
    for gi, window in enumerate(POOL_WINDOWS):
        off = OFF_B + gi * POOL_GROUP
        s = ext(off, POOL_GROUP)
        k = 1
        while k < window:
            s = s + _shift_rows(s, k)
            k *= 2
        pooled = s[HALO:] / jnp.minimum(pos, float(window)) - cur(off, POOL_GROUP)
        yb = _dot(pooled, pool_w_ref[gi]) * pool_scale_ref[:, gi * POOL_GROUP:(gi + 1) * POOL_GROUP]
        out_ref[:, GROUP_WIDTH + gi * POOL_GROUP:GROUP_WIDTH + (gi + 1) * POOL_GROUP] = yb.astype(out_ref.dtype)

    small = cur(OFF_SMALL, 128)
    sp = _softplus(small + dtbias_ref[...])
    rate = -jnp.exp(alog_ref[...]) * sp
    beta_all = jax.nn.sigmoid(small)

    qkv = _silu(_causal_conv(ext(OFF_QKV, 3 * GROUP_WIDTH), gdn_conv_ref)[HALO:])
    xbc = _silu(_causal_conv(ext(OFF_XBC, SSM_XBC), ssm_conv_ref)[HALO:] + ssm_bias_ref[...])

    n4 = GDN_HEADS * CHUNK
    row = lax.broadcasted_iota(jnp.int32, (CHUNK, n4), 0)
    lane = lax.broadcasted_iota(jnp.int32, (CHUNK, n4), 1)
    col = lane & (CHUNK - 1)
    lane_block = lane >> 6
    incl = row >= col
    strict = row > col
    ltri = (lax.broadcasted_iota(jnp.int32, (CHUNK, CHUNK), 0)
            >= lax.broadcasted_iota(jnp.int32, (CHUNK, CHUNK), 1)).astype(BF16)

    def decay_blocks(rate_c, first, n_heads):
        out = []
        for m in range(n_heads // GDN_HEADS):
            r = _expand_heads(rate_c, first + m * GDN_HEADS, GDN_HEADS, CHUNK)
            diff = _dot_ltri(ltri, jnp.where(strict, r, 0.0))
            out.append(jnp.where(incl, jnp.exp(jnp.where(incl, diff, 0.0)), 0.0))
        return out

    chunks = range(n_chunks)
    rows = [slice(c * CHUNK, (c + 1) * CHUNK) for c in chunks]
    acs = [_dot_ltri(ltri, rate[r]) for r in rows]
    e_acs = [jnp.exp(a) for a in acs]
    e_end = [jnp.exp(a[CHUNK - 1:CHUNK, :] - a) for a in acs]
    e_blk = [jnp.exp(a[CHUNK - 1:CHUNK, :]) for a in acs]

    def head_cols(base, h):
        return slice(base + h * GDN_HEAD_DIM, base + (h + 1) * GDN_HEAD_DIM)

    def l2norm_heads(x, scale):
        return jnp.concatenate(
            [x[:, head_cols(0, h)] * (lax.rsqrt(jnp.sum(x[:, head_cols(0, h)] * x[:, head_cols(0, h)],
                                                        axis=-1, keepdims=True) + EPS) * scale)
             for h in range(GDN_HEADS)], axis=1)

    zeros_head = jnp.zeros((CHUNK, GDN_HEAD_DIM), BF16)
    q_all = [l2norm_heads(qkv[r, 0:GROUP_WIDTH], GDN_HEAD_DIM ** -0.5) for r in rows]
    k_all = [l2norm_heads(qkv[r, GROUP_WIDTH:2 * GROUP_WIDTH], 1.0) for r in rows]
    v_all = [qkv[r, 2 * GROUP_WIDTH:3 * GROUP_WIDTH] for r in rows]
    beta_x = [_expand_heads(beta_all[r], SMALL_B, GDN_HEADS, GDN_HEAD_DIM) for r in rows]
    e_g_x = [_expand_heads(e, SMALL_A, GDN_HEADS, GDN_HEAD_DIM) for e in e_acs]
    kb = [x * b for x, b in zip(k_all, beta_x)]
    k_rows = []
    for x in k_all:
        xb = x.astype(BF16)
        k_rows.append(jnp.concatenate(
            [jnp.concatenate([xb[:, head_cols(0, h)] if h == g else zeros_head for h in range(GDN_HEADS)], axis=1)
             for g in range(GDN_HEADS)], axis=0))
    decay = [decay_blocks(rate[r], SMALL_A, GDN_HEADS)[0] for r in rows]
    m_low = [jnp.where(strict, _dot_nt(x, y) * dc, 0.0) for x, y, dc in zip(kb, k_rows, decay)]
    qk = [_dot_nt(x, y) * dc for x, y, dc in zip(q_all, k_rows, decay)]
    t_inv = _unit_lower_inverse(m_low, row, col, lane_block, GDN_HEADS)
    zeros_rhs = jnp.zeros((CHUNK, 2 * GDN_HEAD_DIM), F32)
    sol = []
    for c in chunks:
        vb, kg = v_all[c] * beta_x[c], kb[c] * e_g_x[c]
        rhs = jnp.concatenate(
            [jnp.concatenate([jnp.concatenate([vb[:, head_cols(0, h)], kg[:, head_cols(0, h)]], axis=1)
                              if h == g else zeros_rhs for h in range(GDN_HEADS)], axis=1)
             for g in range(GDN_HEADS)], axis=0)
        sol.append(_dot3(_split_bf16(t_inv[c]), _split_bf16(rhs)))
    q_dec = [x * e for x, e in zip(q_all, e_g_x)]
    k_end = [x * _expand_heads(e, SMALL_A, GDN_HEADS, GDN_HEAD_DIM) for x, e in zip(k_all, e_end)]

    x_in, y_diag, s_inc, c_mat, e_acs_x, e_blk_x = [], [], [], [], [], []
    for c in chunks:
        r = rows[c]
        x_in.append(xbc[r, :GROUP_WIDTH])
        x_dt = x_in[c] * _expand_heads(sp[r], SMALL_DT, SSM_HEADS, SSM_HEAD_DIM)
        x_end = x_dt * _expand_heads(e_end[c], SMALL_DT, SSM_HEADS, SSM_HEAD_DIM)
        e_acs_x.append(_expand_heads(e_acs[c], SMALL_DT, SSM_HEADS, SSM_HEAD_DIM))
        e_blk_x.append(_expand_heads(e_blk[c], SMALL_DT, SSM_HEADS, SSM_HEAD_DIM))
        l_m = decay_blocks(rate[r], SMALL_DT, SSM_HEADS)
        for g in range(SSM_GROUPS):
            b_g = xbc[r, GROUP_WIDTH + g * SSM_STATE:GROUP_WIDTH + (g + 1) * SSM_STATE]
            c_g = xbc[r, GROUP_WIDTH + (SSM_GROUPS + g) * SSM_STATE:GROUP_WIDTH + (SSM_GROUPS + g + 1) * SSM_STATE]
            b_bf = b_g.astype(BF16)
            cb = _dot_nt(c_g, jnp.concatenate([b_bf] * hg, axis=0))
            x_g = x_dt[:, g * gw:(g + 1) * gw]
            y_diag.append(_dot(cb * l_m[g], _block_diag(x_g.astype(BF16), lane_block, hg)))
            s_inc.append(_dot_tn(b_bf, x_end[:, g * gw:(g + 1) * gw]))
            c_mat.append(c_g)

    s_state = [s_ref[h] for h in range(GDN_HEADS)]
    h_state = [h_ref[g] for g in range(SSM_GROUPS)]
    zeros_v = jnp.zeros((CHUNK, GDN_HEAD_DIM), BF16)
    for c in chunks:
        r = rows[c]
        hd = 2 * GDN_HEAD_DIM
        v_new = [sol[c][:, h * hd:h * hd + GDN_HEAD_DIM]
                 - _dot(sol[c][:, h * hd + GDN_HEAD_DIM:(h + 1) * hd], s_state[h]) for h in range(GDN_HEADS)]
        v_rows = jnp.concatenate(
            [jnp.concatenate([v_new[h].astype(BF16) if h == g else zeros_v for h in range(GDN_HEADS)], axis=1)
             for g in range(GDN_HEADS)], axis=0)
        o_intra = _dot(qk[c], v_rows)
        o = [_dot(q_dec[c][:, head_cols(0, h)], s_state[h]) + o_intra[:, head_cols(0, h)]
             for h in range(GDN_HEADS)]
        s_state = [s_state[h] * e_blk[c][:, SMALL_A + h:SMALL_A + h + 1]
                   + _dot_tn(k_end[c][:, head_cols(0, h)], v_new[h]) for h in range(GDN_HEADS)]
        for h in range(GDN_HEADS):
            z = cur(OFF_CZ + h * GDN_HEAD_DIM, GDN_HEAD_DIM, r)
            y = _rmsnorm(o[h], gdn_norm_ref[...]) * _silu(z)
            out_ref[r, head_cols(2 * GROUP_WIDTH, h)] = y.astype(out_ref.dtype)

        ys = []
        for g in range(SSM_GROUPS):
            n = c * SSM_GROUPS + g
            ys.append(y_diag[n] + _dot(c_mat[n], h_state[g]) * e_acs_x[c][:, g * gw:(g + 1) * gw])
            h_state[g] = h_state[g] * e_blk_x[c][:, g * gw:(g + 1) * gw] + s_inc[n]
        y = jnp.concatenate(ys, axis=1) + ssm_d_ref[...] * x_in[c]
        y = y * _silu(cur(OFF_DZ, GROUP_WIDTH, r))
        for g in range(SSM_GROUPS):
            yn = _rmsnorm(y[:, g * gw:(g + 1) * gw], ssm_norm_ref[:, g * gw:(g + 1) * gw])
            out_ref[r, 3 * GROUP_WIDTH + g * gw:3 * GROUP_WIDTH + (g + 1) * gw] = yn.astype(out_ref.dtype)
    for h in range(GDN_HEADS):
        s_ref[h] = s_state[h]
    for g in range(SSM_GROUPS):
        h_ref[g] = h_state[g]


def _mixers(x, g_pre, w_head, w_tail, batch, seq, conv_a_w, pool_w, pool_scale, gdn_conv_w, gdn_norm_g, ssm_conv_w,
            ssm_conv_b, ssm_d_x, ssm_norm_g, alog_row, dtbias_row, tile):
    n_tiles = seq // tile
    d = x.shape[1]

    def const(shape):
        return pl.BlockSpec(shape, lambda b, i: (0,) * len(shape))

    return pl.pallas_call(
        functools.partial(_mixer_kernel, tile=tile),
        grid=(batch, n_tiles),
        in_specs=[
            pl.BlockSpec((tile, d), lambda b, i: (b * n_tiles + i, 0)),
            const((1, d)),
            pl.BlockSpec((d, HEAD_COLS), lambda b, i: (0, 0), pipeline_mode=pl.Buffered(1)),
            pl.BlockSpec((d, TAIL_COLS), lambda b, i: (0, 0), pipeline_mode=pl.Buffered(1)),
            const(conv_a_w.shape), const(pool_w.shape), const(pool_scale.shape), const(gdn_conv_w.shape),
            const(gdn_norm_g.shape), const(ssm_conv_w.shape), const(ssm_conv_b.shape), const(ssm_d_x.shape),
            const(ssm_norm_g.shape), const(alog_row.shape), const(dtbias_row.shape),
        ],
        out_specs=pl.BlockSpec((tile, D_MODEL), lambda b, i: (b * n_tiles + i, 0)),
        out_shape=jax.ShapeDtypeStruct((batch * seq, D_MODEL), BF16),
        scratch_shapes=[
            pltpu.VMEM((HALO + tile, U_COLS), F32),
            pltpu.VMEM((GDN_HEADS, GDN_HEAD_DIM, GDN_HEAD_DIM), F32),
            pltpu.VMEM((SSM_GROUPS, SSM_STATE, (SSM_HEADS // SSM_GROUPS) * SSM_HEAD_DIM), F32),
        ],
        compiler_params=pltpu.CompilerParams(
            dimension_semantics=("arbitrary", "arbitrary"), vmem_limit_bytes=VMEM_LIMIT_LARGE),
        name="mixers",
    )(x, g_pre.reshape(1, d), w_head, w_tail, conv_a_w, pool_w, pool_scale, gdn_conv_w, gdn_norm_g, ssm_conv_w,
      ssm_conv_b, ssm_d_x, ssm_norm_g, alog_row, dtbias_row)


def _xattn_kernel(x_ref, g_ref, wq_ref, kv_ref, o_ref, wb_ref):
    @pl.when((pl.program_id(0) == 0) & (pl.program_id(1) == 0))
    def _():
        wb_ref[...] = wq_ref[...].astype(BF16)

    xn = _rmsnorm(x_ref[...], g_ref[...]).astype(BF16)
    q = jnp.dot(xn, wb_ref[...], preferred_element_type=F32)
    for h in range(XA_HEADS):
        lo = h * XA_HEAD_DIM
        k_h = kv_ref[:, lo:lo + XA_HEAD_DIM]
        v_h = kv_ref[:, D_MODEL + lo:D_MODEL + lo + XA_HEAD_DIM]
        s = _dot_nt(q[:, lo:lo + XA_HEAD_DIM], k_h) * XA_HEAD_DIM ** -0.5
        e = jnp.exp(s - jnp.max(s, axis=-1, keepdims=True))
        p = e / jnp.sum(e, axis=-1, keepdims=True)
        o_ref[:, lo:lo + XA_HEAD_DIM] = _dot(p, v_h).astype(o_ref.dtype)


def _xattn(x, g, wq, layer, kv, batch, seq, tq):
    n, d = x.shape
    n_tiles = seq // tq
    return pl.pallas_call(
        _xattn_kernel,
        grid=(batch, n_tiles),
        in_specs=[
            pl.BlockSpec((tq, d), lambda b, i: (b * n_tiles + i, 0)),
            pl.BlockSpec((1, d), lambda b, i: (0, 0)),
            pl.BlockSpec((None, d, d), lambda b, i: (layer, 0, 0), pipeline_mode=pl.Buffered(1)),
            pl.BlockSpec((MEM_LEN, 2 * d), lambda b, i: (b, 0)),
        ],
        out_specs=pl.BlockSpec((tq, d), lambda b, i: (b * n_tiles + i, 0)),
        out_shape=jax.ShapeDtypeStruct((n, d), BF16),
        scratch_shapes=[pltpu.VMEM((d, d), BF16)],
        compiler_params=pltpu.CompilerParams(
            dimension_semantics=("arbitrary", "arbitrary"), vmem_limit_bytes=VMEM_LIMIT),
        name="xattn",
    )(x, g.reshape(1, d), wq, kv)


def _ffn_kernel(x_ref, gpre_ref, wg_ref, wu_ref, wd_ref, gpost_ref, o_ref, xn_ref, *, edge_chunks):
    j = pl.program_id(1)
    last = pl.num_programs(1) - 1
    tm = o_ref.shape[0]

    def partial_out(rows):
        xn = xn_ref[rows, :]
        gate = jnp.dot(xn, wg_ref[...].astype(BF16), preferred_element_type=F32)
        up = jnp.dot(xn, wu_ref[...].astype(BF16), preferred_element_type=F32)
        return jnp.dot((_silu(gate) * up).astype(BF16), wd_ref[...].astype(BF16), preferred_element_type=F32)

    if edge_chunks == 1:
        @pl.when(j == 0)
        def _():
            xn_ref[...] = _rmsnorm(x_ref[...], gpre_ref[...]).astype(BF16)
            o_ref[...] = jnp.zeros_like(o_ref)

        o_ref[...] += partial_out(slice(0, tm))

        @pl.when(j == last)
        def _():
            o_ref[...] = x_ref[...] + _rmsnorm(o_ref[...], gpost_ref[...])
        return

    chunks = [slice(r, r + tm // edge_chunks) for r in range(0, tm, tm // edge_chunks)]

    @pl.when(j == 0)
    def _():
        for rows in chunks:
            xn_ref[rows, :] = _rmsnorm(x_ref[rows, :], gpre_ref[...]).astype(BF16)
            o_ref[rows, :] = partial_out(rows)

    @pl.when((j > 0) & (j < last))
    def _():
        o_ref[...] += partial_out(slice(0, tm))

    @pl.when(j == last)
    def _():
        for rows in chunks:
            y = o_ref[rows, :] + partial_out(rows)
            o_ref[rows, :] = x_ref[rows, :] + _rmsnorm(y, gpost_ref[...])


def _ffn(x, gpre, w_gu, w_down, layer, gpost, tm, tf, edge_chunks):
    n, d = x.shape
    n_ff = D_FF // tf
    return pl.pallas_call(
        functools.partial(_ffn_kernel, edge_chunks=edge_chunks),
        grid=(n // tm, n_ff),
        in_specs=[
            pl.BlockSpec((tm, d), lambda i, j: (i, 0)),
            pl.BlockSpec((1, d), lambda i, j: (0, 0)),
            pl.BlockSpec((None, d, tf), lambda i, j: (layer, 0, j)),
            pl.BlockSpec((None, d, tf), lambda i, j: (layer, 0, j + n_ff)),
            pl.BlockSpec((None, tf, d), lambda i, j: (layer, j, 0)),
            pl.BlockSpec((1, d), lambda i, j: (0, 0)),
        ],
        out_specs=pl.BlockSpec((tm, d), lambda i, j: (i, 0)),
        out_shape=jax.ShapeDtypeStruct((n, d), F32),
        scratch_shapes=[pltpu.VMEM((tm, d), BF16)],
        compiler_params=pltpu.CompilerParams(
            dimension_semantics=("arbitrary", "arbitrary"), vmem_limit_bytes=VMEM_LIMIT_FFN),
        name="ffn",
    )(x, gpre.reshape(1, d), w_gu, w_gu, w_down, gpost.reshape(1, d))


def _small_row(gdn_vals, ssm_vals):
    row = jnp.zeros((1, 128), F32)
    row = row.at[0, SMALL_A:SMALL_A + GDN_HEADS].set(gdn_vals.astype(F32))
    return row.at[0, SMALL_DT:SMALL_DT + SSM_HEADS].set(ssm_vals.astype(F32))


def kernel(x, mem, norm_g, w_in, conv_a_w, pool_w, pool_scale, gdn_conv_w, gdn_A_log, gdn_dt_bias, gdn_norm_g,
           ssm_conv_w, ssm_conv_b, ssm_A_log, ssm_dt_bias, ssm_D, ssm_norm_g, w_out, xa_wq, xa_wkv, xa_wo,
           ffn_w_gu, ffn_w_down):
    batch, seq, d = x.shape
    depth = w_in.shape[0]
    xf = x.reshape(batch * seq, d)
    memf = mem.reshape(batch * MEM_LEN, d)
    w_in_t = jnp.swapaxes(w_in, 1, 2)
    for l in range(depth):
        g = norm_g[l]
        mixed = _mixers(
            xf, g[0], *_prep_w_in(w_in_t, l), batch, seq, conv_a_w[l], pool_w[l], pool_scale[l].reshape(1, -1),
            gdn_conv_w[l], gdn_norm_g[l].reshape(1, -1), ssm_conv_w[l], ssm_conv_b[l].reshape(1, -1),
            jnp.repeat(ssm_D[l], SSM_HEAD_DIM).reshape(1, -1), ssm_norm_g[l].reshape(1, -1),
            _small_row(gdn_A_log[l], ssm_A_log[l]), _small_row(gdn_dt_bias[l], ssm_dt_bias[l]),
            tile=MIX_TILE)
        xf = _proj_post(mixed, w_out, l, g[1], xf, tm=ROW_TILE)
        kv = _norm_matmul(memf, g[4], xa_wkv, l, 2 * d, BF16, tm=batch * MEM_LEN, tn=KV_COL_TILE)
        att = _xattn(xf, g[2], xa_wq, l, kv, batch, seq, tq=ROW_TILE)
        xf = _proj_post(att, xa_wo, l, g[3], xf, tm=ROW_TILE)
        xf = _ffn(xf, g[5], ffn_w_gu, ffn_w_down, l, g[6], tm=FFN_ROW_TILE, tf=FFN_COL_TILE,
                  edge_chunks=4 if l == 0 else 1)
    return xf.reshape(batch, seq, d)
```

```python
import functools

import jax
import jax.numpy as jnp
from jax import lax
from jax.experimental import pallas as pl
from jax.experimental.pallas import tpu as pltpu

F32 = jnp.float32
BF16 = jnp.bfloat16
EPS = 1e-6

D_MODEL = 2048
GROUP_WIDTH = 512
CHUNK = 64
POOL_WINDOWS = (2, 4, 8, 16)
POOL_GROUP = 128
GDN_HEADS = 4
GDN_HEAD_DIM = 128
SSM_HEADS = 8
SSM_HEAD_DIM = 64
SSM_GROUPS = 2
SSM_STATE = 128
SSM_XBC = 1024
MEM_LEN = 256
XA_HEADS = 4
XA_HEAD_DIM = 512
D_FF = 5632

IN_COLS = 5648
HEAD_COLS = 4096
TAIL_START = 4096
TAIL_VALID = IN_COLS - TAIL_START
TAIL_COLS = 1664
TAIL_BLOCK = 2048
TAIL_SKEW = 8
U_COLS = HEAD_COLS + TAIL_COLS
OFF_A = 0
OFF_B = 1536
OFF_QKV = 2048
OFF_CZ = 3584
OFF_DZ = 4096
OFF_XBC = 4608
OFF_SMALL = 5632
SMALL_DT = 0
SMALL_A = 120
SMALL_B = 124
HALO = 16

MIX_TILE = 4 * CHUNK
MIX_COL_CHUNK = 1024
ROW_TILE = 512
KV_COL_TILE = 512
FFN_ROW_TILE = 1024
FFN_COL_TILE = 256
FFN_EDGE_CHUNKS = 4

VMEM_LIMIT = 48 * 1024 * 1024
VMEM_LIMIT_LARGE = 56 * 1024 * 1024
VMEM_LIMIT_FFN = 60000 * 1024


def _rmsnorm(x, g):
    return x * lax.rsqrt(jnp.mean(x * x, axis=-1, keepdims=True) + EPS) * g


def _silu(x):
    return x * jax.nn.sigmoid(x)


def _softplus(x):
    return jnp.maximum(x, 0.0) + jnp.log1p(jnp.exp(-jnp.abs(x)))


def _dot(a, b):
    return jnp.dot(a.astype(BF16), b.astype(BF16), preferred_element_type=F32)


def _dot_nt(a, b):
    return lax.dot_general(a.astype(BF16), b.astype(BF16), (((1,), (1,)), ((), ())),
                           preferred_element_type=F32)


def _dot_tn(a, b):
    return lax.dot_general(a.astype(BF16), b.astype(BF16), (((0,), (0,)), ((), ())),
                           preferred_element_type=F32)


def _split_bf16(a):
    hi = a.astype(BF16)
    return hi, (a - hi.astype(F32)).astype(BF16)


def _dot3(a, b):
    return (jnp.dot(a[0], b[0], preferred_element_type=F32)
            + jnp.dot(a[0], b[1], preferred_element_type=F32)
            + jnp.dot(a[1], b[0], preferred_element_type=F32))


def _norm_matmul_kernel(x_ref, g_ref, w_ref, o_ref, xn_ref):
    @pl.when(pl.program_id(1) == 0)
    def _():
        xn_ref[...] = _rmsnorm(x_ref[...], g_ref[...]).astype(BF16)

    o_ref[...] = jnp.dot(xn_ref[...], w_ref[...].astype(BF16), preferred_element_type=F32).astype(o_ref.dtype)


def _norm_matmul(x, g, w, layer, n_cols, out_dtype, tm, tn):
    n, k = x.shape
    return pl.pallas_call(
        _norm_matmul_kernel,
        grid=(n // tm, n_cols // tn),
        in_specs=[
            pl.BlockSpec((tm, k), lambda i, j: (i, 0)),
            pl.BlockSpec((1, k), lambda i, j: (0, 0)),
            pl.BlockSpec((None, k, tn), lambda i, j: (layer, 0, j)),
        ],
        out_specs=pl.BlockSpec((tm, tn), lambda i, j: (i, j)),
        out_shape=jax.ShapeDtypeStruct((n, n_cols), out_dtype),
        scratch_shapes=[pltpu.VMEM((tm, k), BF16)],
        compiler_params=pltpu.CompilerParams(
            dimension_semantics=("arbitrary", "arbitrary"), vmem_limit_bytes=VMEM_LIMIT_LARGE),
        name="norm_matmul",
    )(x, g.reshape(1, k), w)


PREP_ROWS = 512


def _prep_head_kernel(wt_ref, o_ref):
    o_ref[...] = wt_ref[...].T.astype(BF16)


def _prep_tail_kernel(wt_ref, o_ref):
    dt0 = TAIL_VALID - TAIL_SKEW
    body = TAIL_COLS - 128
    for r in range(0, body, PREP_ROWS):
        o_ref[:, r:r + PREP_ROWS] = wt_ref[TAIL_SKEW + r:TAIL_SKEW + r + PREP_ROWS, :].T.astype(BF16)
    last = jnp.concatenate([wt_ref[dt0:TAIL_VALID, :],
                            jnp.zeros((128 - 2 * TAIL_SKEW, wt_ref.shape[1]), F32),
                            wt_ref[0:TAIL_SKEW, :]], axis=0)
    o_ref[:, body:] = last.T.astype(BF16)


def _prep_w_in(wt, layer):
    k = wt.shape[2]
    head = pl.pallas_call(
        _prep_head_kernel,
        grid=(HEAD_COLS // PREP_ROWS,),
        in_specs=[pl.BlockSpec((None, PREP_ROWS, k), lambda j: (layer, j, 0))],
        out_specs=pl.BlockSpec((k, PREP_ROWS), lambda j: (0, j)),
        out_shape=jax.ShapeDtypeStruct((k, HEAD_COLS), BF16),
        compiler_params=pltpu.CompilerParams(dimension_semantics=("arbitrary",), vmem_limit_bytes=VMEM_LIMIT),
        name="prep_w_in_head",
    )(wt)
    tail = pl.pallas_call(
        _prep_tail_kernel,
        grid=(1,),
        in_specs=[pl.BlockSpec((None, TAIL_BLOCK, k), lambda j: (layer, TAIL_START // TAIL_BLOCK, 0),
                               pipeline_mode=pl.Buffered(1))],
        out_specs=pl.BlockSpec((k, TAIL_COLS), lambda j: (0, 0)),
        out_shape=jax.ShapeDtypeStruct((k, TAIL_COLS), BF16),
        compiler_params=pltpu.CompilerParams(dimension_semantics=("arbitrary",), vmem_limit_bytes=VMEM_LIMIT_LARGE),
        name="prep_w_in_tail",
    )(wt)
    return head, tail


def _proj_post_kernel(a_ref, w_ref, g_ref, x_ref, o_ref, wb_ref):
    @pl.when(pl.program_id(0) == 0)
    def _():
        wb_ref[...] = w_ref[...].astype(BF16)

    y = jnp.dot(a_ref[...], wb_ref[...], preferred_element_type=F32)
    o_ref[...] = x_ref[...] + _rmsnorm(y, g_ref[...])


def _proj_post(a, w, layer, g, x, tm):
    n, k = a.shape
    d = w.shape[2]
    return pl.pallas_call(
        _proj_post_kernel,
        grid=(n // tm,),
        in_specs=[
            pl.BlockSpec((tm, k), lambda i: (i, 0)),
            pl.BlockSpec((None, k, d), lambda i: (layer, 0, 0), pipeline_mode=pl.Buffered(1)),
            pl.BlockSpec((1, d), lambda i: (0, 0)),
            pl.BlockSpec((tm, d), lambda i: (i, 0)),
        ],
        out_specs=pl.BlockSpec((tm, d), lambda i: (i, 0)),
        out_shape=jax.ShapeDtypeStruct((n, d), F32),
        scratch_shapes=[pltpu.VMEM((k, d), BF16)],
        compiler_params=pltpu.CompilerParams(
            dimension_semantics=("arbitrary",), vmem_limit_bytes=VMEM_LIMIT_LARGE),
        name="proj_post",
    )(a, w, g.reshape(1, d), x)


def _shift_rows(x, k):
    return pltpu.roll(x, k, axis=0) if k else x


def _causal_conv(x_ext, w_ref):
    width = w_ref.shape[0]
    if width == 4:
        x1 = _shift_rows(x_ext, 1)
        older = _shift_rows(x_ext * w_ref[1:2, :] + x1 * w_ref[0:1, :], 2)
        return x_ext * w_ref[3:4, :] + x1 * w_ref[2:3, :] + older
    acc = x_ext * w_ref[width - 1:width, :]
    for k in range(width - 1):
        acc = acc + _shift_rows(x_ext, width - 1 - k) * w_ref[k:k + 1, :]
    return acc


def _expand_heads(cols, first, n_heads, width):
    rows = cols.shape[0]
    per_vreg = 128 // width
    lane = lax.broadcasted_iota(jnp.int32, (rows, 128), 1)
    pieces = []
    for m in range(n_heads // per_vreg):
        piece = jnp.broadcast_to(cols[:, first + m * per_vreg:first + m * per_vreg + 1], (rows, 128))
        for j in range(1, per_vreg):
            c = first + m * per_vreg + j
            piece = jnp.where(lane >= j * width, jnp.broadcast_to(cols[:, c:c + 1], (rows, 128)), piece)
        pieces.append(piece)
    return jnp.concatenate(pieces, axis=1)


def _block_diag(y, lane_block, n_blocks):
    return jnp.concatenate([jnp.where(lane_block == b, y, jnp.zeros_like(y)) for b in range(n_blocks)], axis=0)


def _split_block_diag(y, lane_block, n_blocks):
    hi, lo = _split_bf16(y)
    return _block_diag(hi, lane_block, n_blocks), _block_diag(lo, lane_block, n_blocks)


def _unit_lower_inverse(ms, row, col, lane_block, n_blocks):
    same16 = (row >> 4) == (col >> 4)
    same32 = (row >> 5) == (col >> 5)
    off16 = same32 & jnp.logical_not(same16)
    eye = (row == col).astype(F32)
    split = lambda xs: [_split_bf16(x) for x in xs]
    split_bd = lambda xs: [_split_block_diag(x, lane_block, n_blocks) for x in xs]
    d = [jnp.where(same16, m, 0.0) for m in ms]
    d2 = [_dot3(x, y) for x, y in zip(split(d), split_bd(d))]
    p = [eye - x for x in d]
    d2_bd = split_bd(d2)
    p = [x + _dot3(xs, y) for x, xs, y in zip(p, split(p), d2_bd)]
    d4 = [_dot3(x, y) for x, y in zip(split(d2), d2_bd)]
    d4_bd = split_bd(d4)
    p = [x + _dot3(xs, y) for x, xs, y in zip(p, split(p), d4_bd)]
    d8_bd = split_bd([_dot3(x, y) for x, y in zip(split(d4), d4_bd)])
    p = [x + _dot3(xs, y) for x, xs, y in zip(p, split(p), d8_bd)]
    ps, p_bd = split(p), split_bd(p)
    t = split([_dot3(xs, _split_block_diag(jnp.where(off16, m, 0.0), lane_block, n_blocks)) for xs, m in zip(ps, ms)])
    q = [x - _dot3(y, xb) for x, xb, y in zip(p, p_bd, t)]
    qs, q_bd = split(q), split_bd(q)
    t = split([_dot3(xs, _split_block_diag(jnp.where(same32, 0.0, m), lane_block, n_blocks)) for xs, m in zip(qs, ms)])
    return [x - _dot3(y, xb) for x, xb, y in zip(q, q_bd, t)]


def _dot_ltri(ltri, x):
    x1 = x.astype(BF16)
    r1 = x - x1.astype(F32)
    x2 = r1.astype(BF16)
    x3 = (r1 - x2.astype(F32)).astype(BF16)
    return (jnp.dot(ltri, x1, preferred_element_type=F32) + jnp.dot(ltri, x2, preferred_element_type=F32)
            + jnp.dot(ltri, x3, preferred_element_type=F32))


def _mixer_kernel(x_ref, gpre_ref, w_head_ref, w_tail_ref, conv_a_ref, pool_w_ref, pool_scale_ref, gdn_conv_ref,
                  gdn_norm_ref, ssm_conv_ref, ssm_bias_ref, ssm_d_ref, ssm_norm_ref,
                  alog_ref, dtbias_ref, out_ref, u_ref, s_ref, h_ref, *, tile):
    i = pl.program_id(1)
    n_chunks = tile // CHUNK
    hg = SSM_HEADS // SSM_GROUPS
    gw = hg * SSM_HEAD_DIM

    @pl.when(i == 0)
    def _():
        s_ref[...] = jnp.zeros_like(s_ref)
        h_ref[...] = jnp.zeros_like(h_ref)
        u_ref[0:HALO, :] = jnp.zeros((HALO, U_COLS), F32)

    @pl.when(i > 0)
    def _():
        u_ref[0:HALO, :] = u_ref[tile:tile + HALO, :]

    xn = _rmsnorm(x_ref[...], gpre_ref[...]).astype(BF16)
    for c0 in range(0, HEAD_COLS, MIX_COL_CHUNK):
        u_ref[HALO:, c0:c0 + MIX_COL_CHUNK] = jnp.dot(xn, w_head_ref[:, c0:c0 + MIX_COL_CHUNK],
                                                        preferred_element_type=F32)
    u_ref[HALO:, HEAD_COLS:] = jnp.dot(xn, w_tail_ref[...], preferred_element_type=F32)

    def cur(off, width, rows=slice(0, tile)):
        return u_ref[HALO + rows.start:HALO + rows.stop, off:off + width]

    def ext(off, width):
        return u_ref[:, off:off + width]

    p = ext(OFF_A + GROUP_WIDTH, GROUP_WIDTH) * ext(OFF_A + 2 * GROUP_WIDTH, GROUP_WIDTH)
    ya = cur(OFF_A, GROUP_WIDTH) * _causal_conv(p, conv_a_ref)[HALO:]
    out_ref[:, 0:GROUP_WIDTH] = ya.astype(out_ref.dtype)

    pos = (i * tile + 1 + lax.broadcasted_iota(jnp.int32, (tile, 1), 0)).astype(F32)
    for gi, window in enumerate(POOL_WINDOWS):
        off = OFF_B + gi * POOL_GROUP
        s = ext(off, POOL_GROUP)
        k = 1
        while k < window:
            s = s + _shift_rows(s, k)
            k *= 2
        pooled = s[HALO:] / jnp.minimum(pos, float(window)) - cur(off, POOL_GROUP)
        yb = _dot(pooled, pool_w_ref[gi]) * pool_scale_ref[:, gi * POOL_GROUP:(gi + 1) * POOL_GROUP]
        out_ref[:, GROUP_WIDTH + gi * POOL_GROUP:GROUP_WIDTH + (gi + 1) * POOL_GROUP] = yb.astype(out_ref.dtype)

    small = cur(OFF_SMALL, 128)
    sp = _softplus(small + dtbias_ref[...])
    rate = -jnp.exp(alog_ref[...]) * sp
    beta_all = jax.nn.sigmoid(small)

    qkv = _silu(_causal_conv(ext(OFF_QKV, 3 * GROUP_WIDTH), gdn_conv_ref)[HALO:])
    xbc = _silu(_causal_conv(ext(OFF_XBC, SSM_XBC), ssm_conv_ref)[HALO:] + ssm_bias_ref[...])

    n4 = GDN_HEADS * CHUNK
    row = lax.broadcasted_iota(jnp.int32, (CHUNK, n4), 0)
    lane = lax.broadcasted_iota(jnp.int32, (CHUNK, n4), 1)
    col = lane & (CHUNK - 1)
    lane_block = lane >> 6
    incl = row >= col
    strict = row > col
    ltri = (lax.broadcasted_iota(jnp.int32, (CHUNK, CHUNK), 0)
            >= lax.broadcasted_iota(jnp.int32, (CHUNK, CHUNK), 1)).astype(BF16)

    def decay_blocks(rate_c, first, n_heads):
        out = []
        for m in range(n_heads // GDN_HEADS):
            r = _expand_heads(rate_c, first + m * GDN_HEADS, GDN_HEADS, CHUNK)
            diff = _dot_ltri(ltri, jnp.where(strict, r, 0.0))
            out.append(jnp.where(incl, jnp.exp(jnp.where(incl, diff, 0.0)), 0.0))
        return out

    chunks = range(n_chunks)
    rows = [slice(c * CHUNK, (c + 1) * CHUNK) for c in chunks]
    acs = [_dot_ltri(ltri, rate[r]) for r in rows]
    e_acs = [jnp.exp(a) for a in acs]
    e_end = [jnp.exp(a[CHUNK - 1:CHUNK, :] - a) for a in acs]
    e_blk = [jnp.exp(a[CHUNK - 1:CHUNK, :]) for a in acs]

    def head_cols(base, h):
        return slice(base + h * GDN_HEAD_DIM, base + (h + 1) * GDN_HEAD_DIM)

    def l2norm_heads(x, scale):
        return jnp.concatenate(
            [x[:, head_cols(0, h)] * (lax.rsqrt(jnp.sum(x[:, head_cols(0, h)] * x[:, head_cols(0, h)],
                                                        axis=-1, keepdims=True) + EPS) * scale)
             for h in range(GDN_HEADS)], axis=1)

    zeros_head = jnp.zeros((CHUNK, GDN_HEAD_DIM), BF16)
    q_all = [l2norm_heads(qkv[r, 0:GROUP_WIDTH], GDN_HEAD_DIM ** -0.5) for r in rows]
    k_all = [l2norm_heads(qkv[r, GROUP_WIDTH:2 * GROUP_WIDTH], 1.0) for r in rows]
    v_all = [qkv[r, 2 * GROUP_WIDTH:3 * GROUP_WIDTH] for r in rows]
    beta_x = [_expand_heads(beta_all[r], SMALL_B, GDN_HEADS, GDN_HEAD_DIM) for r in rows]
    e_g_x = [_expand_heads(e, SMALL_A, GDN_HEADS, GDN_HEAD_DIM) for e in e_acs]
    kb = [x * b for x, b in zip(k_all, beta_x)]
    k_rows = []
    for x in k_all:
        xb = x.astype(BF16)
        k_rows.append(jnp.concatenate(
            [jnp.concatenate([xb[:, head_cols(0, h)] if h == g else zeros_head for h in range(GDN_HEADS)], axis=1)
             for g in range(GDN_HEADS)], axis=0))
    decay = [decay_blocks(rate[r], SMALL_A, GDN_HEADS)[0] for r in rows]
    m_low = [jnp.where(strict, _dot_nt(x, y) * dc, 0.0) for x, y, dc in zip(kb, k_rows, decay)]
    qk = [_dot_nt(x, y) * dc for x, y, dc in zip(q_all, k_rows, decay)]
    t_inv = _unit_lower_inverse(m_low, row, col, lane_block, GDN_HEADS)
    zeros_rhs = jnp.zeros((CHUNK, 2 * GDN_HEAD_DIM), F32)
    sol = []
    for c in chunks:
        vb, kg = v_all[c] * beta_x[c], kb[c] * e_g_x[c]
        rhs = jnp.concatenate(
            [jnp.concatenate([jnp.concatenate([vb[:, head_cols(0, h)], kg[:, head_cols(0, h)]], axis=1)
                              if h == g else zeros_rhs for h in range(GDN_HEADS)], axis=1)
             for g in range(GDN_HEADS)], axis=0)
        sol.append(_dot3(_split_bf16(t_inv[c]), _split_bf16(rhs)))
    q_dec = [x * e for x, e in zip(q_all, e_g_x)]
    k_end = [x * _expand_heads(e, SMALL_A, GDN_HEADS, GDN_HEAD_DIM) for x, e in zip(k_all, e_end)]

    x_in, y_diag, s_inc, c_mat, e_acs_x, e_blk_x = [], [], [], [], [], []
    for c in chunks:
        r = rows[c]
        x_in.append(xbc[r, :GROUP_WIDTH])
        x_dt = x_in[c] * _expand_heads(sp[r], SMALL_DT, SSM_HEADS, SSM_HEAD_DIM)
        x_end = x_dt * _expand_heads(e_end[c], SMALL_DT, SSM_HEADS, SSM_HEAD_DIM)
        e_acs_x.append(_expand_heads(e_acs[c], SMALL_DT, SSM_HEADS, SSM_HEAD_DIM))
        e_blk_x.append(_expand_heads(e_blk[c], SMALL_DT, SSM_HEADS, SSM_HEAD_DIM))
        l_m = decay_blocks(rate[r], SMALL_DT, SSM_HEADS)
        for g in range(SSM_GROUPS):
            b_g = xbc[r, GROUP_WIDTH + g * SSM_STATE:GROUP_WIDTH + (g + 1) * SSM_STATE]
            c_g = xbc[r, GROUP_WIDTH + (SSM_GROUPS + g) * SSM_STATE:GROUP_WIDTH + (SSM_GROUPS + g + 1) * SSM_STATE]
            b_bf = b_g.astype(BF16)
            cb = _dot_nt(c_g, jnp.concatenate([b_bf] * hg, axis=0))
            x_g = x_dt[:, g * gw:(g + 1) * gw]
            y_diag.append(_dot(cb * l_m[g], _block_diag(x_g.astype(BF16), lane_block, hg)))
            s_inc.append(_dot_tn(b_bf, x_end[:, g * gw:(g + 1) * gw]))
            c_mat.append(c_g)

    s_state = [s_ref[h] for h in range(GDN_HEADS)]
    h_state = [h_ref[g] for g in range(SSM_GROUPS)]
    zeros_v = jnp.zeros((CHUNK, GDN_HEAD_DIM), BF16)
    for c in chunks:
        r = rows[c]
        hd = 2 * GDN_HEAD_DIM
        v_new = [sol[c][:, h * hd:h * hd + GDN_HEAD_DIM]
                 - _dot(sol[c][:, h * hd + GDN_HEAD_DIM:(h + 1) * hd], s_state[h]) for h in range(GDN_HEADS)]
        v_rows = jnp.concatenate(
            [jnp.concatenate([v_new[h].astype(BF16) if h == g else zeros_v for h in range(GDN_HEADS)], axis=1)
             for g in range(GDN_HEADS)], axis=0)
        o_intra = _dot(qk[c], v_rows)
        o = [_dot(q_dec[c][:, head_cols(0, h)], s_state[h]) + o_intra[:, head_cols(0, h)]
             for h in range(GDN_HEADS)]
        s_state = [s_state[h] * e_blk[c][:, SMALL_A + h:SMALL_A + h + 1]
                   + _dot_tn(k_end[c][:, head_cols(0, h)], v_new[h]) for h in range(GDN_HEADS)]
        for h in range(GDN_HEADS):
            z = cur(OFF_CZ + h * GDN_HEAD_DIM, GDN_HEAD_DIM, r)
            y = _rmsnorm(o[h], gdn_norm_ref[...]) * _silu(z)
            out_ref[r, head_cols(2 * GROUP_WIDTH, h)] = y.astype(out_ref.dtype)

        ys = []
        for g in range(SSM_GROUPS):
            n = c * SSM_GROUPS + g
            ys.append(y_diag[n] + _dot(c_mat[n], h_state[g]) * e_acs_x[c][:, g * gw:(g + 1) * gw])
            h_state[g] = h_state[g] * e_blk_x[c][:, g * gw:(g + 1) * gw] + s_inc[n]
        y = jnp.concatenate(ys, axis=1) + ssm_d_ref[...] * x_in[c]
        y = y * _silu(cur(OFF_DZ, GROUP_WIDTH, r))
        for g in range(SSM_GROUPS):
            yn = _rmsnorm(y[:, g * gw:(g + 1) * gw], ssm_norm_ref[:, g * gw:(g + 1) * gw])
            out_ref[r, 3 * GROUP_WIDTH + g * gw:3 * GROUP_WIDTH + (g + 1) * gw] = yn.astype(out_ref.dtype)
    for h in range(GDN_HEADS):
        s_ref[h] = s_state[h]
    for g in range(SSM_GROUPS):
        h_ref[g] = h_state[g]


def _mixers(x, g_pre, w_head, w_tail, batch, seq, conv_a_w, pool_w, pool_scale, gdn_conv_w, gdn_norm_g, ssm_conv_w,
            ssm_conv_b, ssm_d_x, ssm_norm_g, alog_row, dtbias_row, tile):
    n_tiles = seq // tile
    d = x.shape[1]

    def const(shape):
        return pl.BlockSpec(shape, lambda b, i: (0,) * len(shape))

    return pl.pallas_call(
        functools.partial(_mixer_kernel, tile=tile),
        grid=(batch, n_tiles),
        in_specs=[
            pl.BlockSpec((tile, d), lambda b, i: (b * n_tiles + i, 0)),
            const((1, d)),
            pl.BlockSpec((d, HEAD_COLS), lambda b, i: (0, 0), pipeline_mode=pl.Buffered(1)),
            pl.BlockSpec((d, TAIL_COLS), lambda b, i: (0, 0), pipeline_mode=pl.Buffered(1)),
            const(conv_a_w.shape), const(pool_w.shape), const(pool_scale.shape), const(gdn_conv_w.shape),
            const(gdn_norm_g.shape), const(ssm_conv_w.shape), const(ssm_conv_b.shape), const(ssm_d_x.shape),
            const(ssm_norm_g.shape), const(alog_row.shape), const(dtbias_row.shape),
        ],
        out_specs=pl.BlockSpec((tile, D_MODEL), lambda b, i: (b * n_tiles + i, 0)),
        out_shape=jax.ShapeDtypeStruct((batch * seq, D_MODEL), BF16),
        scratch_shapes=[
            pltpu.VMEM((HALO + tile, U_COLS), F32),
            pltpu.VMEM((GDN_HEADS, GDN_HEAD_DIM, GDN_HEAD_DIM), F32),
            pltpu.VMEM((SSM_GROUPS, SSM_STATE, (SSM_HEADS // SSM_GROUPS) * SSM_HEAD_DIM), F32),
        ],
        compiler_params=pltpu.CompilerParams(
            dimension_semantics=("arbitrary", "arbitrary"), vmem_limit_bytes=VMEM_LIMIT_LARGE),
        name="mixers",
    )(x, g_pre.reshape(1, d), w_head, w_tail, conv_a_w, pool_w, pool_scale, gdn_conv_w, gdn_norm_g, ssm_conv_w,
      ssm_conv_b, ssm_d_x, ssm_norm_g, alog_row, dtbias_row)


def _xattn_kernel(x_ref, g_ref, wq_ref, kv_ref, o_ref, wb_ref):
    @pl.when((pl.program_id(0) == 0) & (pl.program_id(1) == 0))
    def _():
        wb_ref[...] = wq_ref[...].astype(BF16)

    xn = _rmsnorm(x_ref[...], g_ref[...]).astype(BF16)
    q = jnp.dot(xn, wb_ref[...], preferred_element_type=F32)
    for h in range(XA_HEADS):
        lo = h * XA_HEAD_DIM
        k_h = kv_ref[:, lo:lo + XA_HEAD_DIM]
        v_h = kv_ref[:, D_MODEL + lo:D_MODEL + lo + XA_HEAD_DIM]
        s = _dot_nt(q[:, lo:lo + XA_HEAD_DIM], k_h) * XA_HEAD_DIM ** -0.5
        e = jnp.exp(s - jnp.max(s, axis=-1, keepdims=True))
        p = e / jnp.sum(e, axis=-1, keepdims=True)
        o_ref[:, lo:lo + XA_HEAD_DIM] = _dot(p, v_h).astype(o_ref.dtype)


def _xattn(x, g, wq, layer, kv, batch, seq, tq):
    n, d = x.shape
    n_tiles = seq // tq
    return pl.pallas_call(
        _xattn_kernel,
        grid=(batch, n_tiles),
        in_specs=[
            pl.BlockSpec((tq, d), lambda b, i: (b * n_tiles + i, 0)),
            pl.BlockSpec((1, d), lambda b, i: (0, 0)),
            pl.BlockSpec((None, d, d), lambda b, i: (layer, 0, 0), pipeline_mode=pl.Buffered(1)),
            pl.BlockSpec((MEM_LEN, 2 * d), lambda b, i: (b, 0)),
        ],
        out_specs=pl.BlockSpec((tq, d), lambda b, i: (b * n_tiles + i, 0)),
        out_shape=jax.ShapeDtypeStruct((n, d), BF16),
        scratch_shapes=[pltpu.VMEM((d, d), BF16)],
        compiler_params=pltpu.CompilerParams(
            dimension_semantics=("arbitrary", "arbitrary"), vmem_limit_bytes=VMEM_LIMIT),
        name="xattn",
    )(x, g.reshape(1, d), wq, kv)


def _ffn_kernel(x_ref, gpre_ref, wg_ref, wu_ref, wd_ref, gpost_ref, o_ref, xn_ref):
    j = pl.program_id(1)
    last = pl.num_programs(1) - 1
    tm = o_ref.shape[0]

    def partial_out(rows):
        xn = xn_ref[rows, :]
        gate = jnp.dot(xn, wg_ref[...].astype(BF16), preferred_element_type=F32)
        up = jnp.dot(xn, wu_ref[...].astype(BF16), preferred_element_type=F32)
        return jnp.dot((_silu(gate) * up).astype(BF16), wd_ref[...].astype(BF16), preferred_element_type=F32)

    chunks = [slice(r, r + tm // FFN_EDGE_CHUNKS) for r in range(0, tm, tm // FFN_EDGE_CHUNKS)]

    @pl.when(j == 0)
    def _():
        for rows in chunks:
            xn_ref[rows, :] = _rmsnorm(x_ref[rows, :], gpre_ref[...]).astype(BF16)
            o_ref[rows, :] = partial_out(rows)

    @pl.when((j > 0) & (j < last))
    def _():
        o_ref[...] += partial_out(slice(0, tm))

    @pl.when(j == last)
    def _():
        for rows in chunks:
            y = o_ref[rows, :] + partial_out(rows)
            o_ref[rows, :] = x_ref[rows, :] + _rmsnorm(y, gpost_ref[...])


def _ffn(x, gpre, w_gu, w_down, layer, gpost, tm, tf):
    n, d = x.shape
    n_ff = D_FF // tf
    return pl.pallas_call(
        _ffn_kernel,
        grid=(n // tm, n_ff),
        in_specs=[
            pl.BlockSpec((tm, d), lambda i, j: (i, 0)),
            pl.BlockSpec((1, d), lambda i, j: (0, 0)),
            pl.BlockSpec((None, d, tf), lambda i, j: (layer, 0, j)),
            pl.BlockSpec((None, d, tf), lambda i, j: (layer, 0, j + n_ff)),
            pl.BlockSpec((None, tf, d), lambda i, j: (layer, j, 0)),
            pl.BlockSpec((1, d), lambda i, j: (0, 0)),
        ],
        out_specs=pl.BlockSpec((tm, d), lambda i, j: (i, 0)),
        out_shape=jax.ShapeDtypeStruct((n, d), F32),
        scratch_shapes=[pltpu.VMEM((tm, d), BF16)],
        compiler_params=pltpu.CompilerParams(
            dimension_semantics=("arbitrary", "arbitrary"), vmem_limit_bytes=VMEM_LIMIT_FFN),
        name="ffn",
    )(x, gpre.reshape(1, d), w_gu, w_gu, w_down, gpost.reshape(1, d))


def _small_row(gdn_vals, ssm_vals):
    row = jnp.zeros((1, 128), F32)
    row = row.at[0, SMALL_A:SMALL_A + GDN_HEADS].set(gdn_vals.astype(F32))
    return row.at[0, SMALL_DT:SMALL_DT + SSM_HEADS].set(ssm_vals.astype(F32))


def kernel(x, mem, norm_g, w_in, conv_a_w, pool_w, pool_scale, gdn_conv_w, gdn_A_log, gdn_dt_bias, gdn_norm_g,
           ssm_conv_w, ssm_conv_b, ssm_A_log, ssm_dt_bias, ssm_D, ssm_norm_g, w_out, xa_wq, xa_wkv, xa_wo,
           ffn_w_gu, ffn_w_down):
    batch, seq, d = x.shape
    depth = w_in.shape[0]
    xf = x.reshape(batch * seq, d)
    memf = mem.reshape(batch * MEM_LEN, d)
    w_in_t = jnp.swapaxes(w_in, 1, 2)
    for l in range(depth):
        g = norm_g[l]
        mixed = _mixers(
            xf, g[0], *_prep_w_in(w_in_t, l), batch, seq, conv_a_w[l], pool_w[l], pool_scale[l].reshape(1, -1),
            gdn_conv_w[l], gdn_norm_g[l].reshape(1, -1), ssm_conv_w[l], ssm_conv_b[l].reshape(1, -1),
            jnp.repeat(ssm_D[l], SSM_HEAD_DIM).reshape(1, -1), ssm_norm_g[l].reshape(1, -1),
            _small_row(gdn_A_log[l], ssm_A_log[l]), _small_row(gdn_dt_bias[l], ssm_dt_bias[l]),
            tile=MIX_TILE)
        xf = _proj_post(mixed, w_out, l, g[1], xf, tm=ROW_TILE)
        kv = _norm_matmul(memf, g[4], xa_wkv, l, 2 * d, BF16, tm=batch * MEM_LEN, tn=KV_COL_TILE)
        att = _xattn(xf, g[2], xa_wq, l, kv, batch, seq, tq=ROW_TILE)
        xf = _proj_post(att, xa_wo, l, g[3], xf, tm=ROW_TILE)
        xf = _ffn(xf, g[5], ffn_w_gu, ffn_w_down, l, g[6], tm=FFN_ROW_TILE, tf=FFN_COL_TILE)
    return xf.reshape(batch, seq, d)
```

```python
import functools

import jax
import jax.numpy as jnp
from jax import lax
from jax.experimental import pallas as pl
from jax.experimental.pallas import tpu as pltpu

F32 = jnp.float32
BF16 = jnp.bfloat16
EPS = 1e-6

D_MODEL = 2048
GROUP_WIDTH = 512
CHUNK = 64
POOL_WINDOWS = (2, 4, 8, 16)
POOL_GROUP = 128
GDN_HEADS = 4
GDN_HEAD_DIM = 128
SSM_HEADS = 8
SSM_HEAD_DIM = 64
SSM_GROUPS = 2
SSM_STATE = 128
SSM_XBC = 1024
MEM_LEN = 256
XA_HEADS = 4
XA_HEAD_DIM = 512
D_FF = 5632

IN_COLS = 5648
HEAD_COLS = 4096
TAIL_START = 4096
TAIL_VALID = IN_COLS - TAIL_START
TAIL_COLS = 1664
TAIL_BLOCK = 2048
TAIL_SKEW = 8
U_COLS = HEAD_COLS + TAIL_COLS
OFF_A = 0
OFF_B = 1536
OFF_QKV = 2048
OFF_CZ = 3584
OFF_DZ = 4096
OFF_XBC = 4608
OFF_SMALL = 5632
SMALL_DT = 0
SMALL_A = 120
SMALL_B = 124
HALO = 16

MIX_TILE = 4 * CHUNK
MIX_COL_CHUNK = 1024
ROW_TILE = 512
KV_COL_TILE = 512
FFN_ROW_TILE = 1024
FFN_COL_TILE = 256
FFN_EDGE_CHUNKS = 4

VMEM_LIMIT = 48 * 1024 * 1024
VMEM_LIMIT_LARGE = 56 * 1024 * 1024
VMEM_LIMIT_FFN = 60000 * 1024


def _rmsnorm(x, g):
    return x * lax.rsqrt(jnp.mean(x * x, axis=-1, keepdims=True) + EPS) * g


def _silu(x):
    return x * jax.nn.sigmoid(x)


def _softplus(x):
    return jnp.maximum(x, 0.0) + jnp.log1p(jnp.exp(-jnp.abs(x)))


def _dot(a, b):
    return jnp.dot(a.astype(BF16), b.astype(BF16), preferred_element_type=F32)


def _dot_nt(a, b):
    return lax.dot_general(a.astype(BF16), b.astype(BF16), (((1,), (1,)), ((), ())),
                           preferred_element_type=F32)


def _dot_tn(a, b):
    return lax.dot_general(a.astype(BF16), b.astype(BF16), (((0,), (0,)), ((), ())),
                           preferred_element_type=F32)


def _split_bf16(a):
    hi = a.astype(BF16)
    return hi, (a - hi.astype(F32)).astype(BF16)


def _dot3(a, b):
    return (jnp.dot(a[0], b[0], preferred_element_type=F32)
            + jnp.dot(a[0], b[1], preferred_element_type=F32)
            + jnp.dot(a[1], b[0], preferred_element_type=F32))


def _norm_matmul_kernel(x_ref, g_ref, w_ref, o_ref, xn_ref):
    @pl.when(pl.program_id(1) == 0)
    def _():
        xn_ref[...] = _rmsnorm(x_ref[...], g_ref[...]).astype(BF16)

    o_ref[...] = jnp.dot(xn_ref[...], w_ref[...].astype(BF16), preferred_element_type=F32).astype(o_ref.dtype)


def _norm_matmul(x, g, w, layer, n_cols, out_dtype, tm, tn):
    n, k = x.shape
    return pl.pallas_call(
        _norm_matmul_kernel,
        grid=(n // tm, n_cols // tn),
        in_specs=[
            pl.BlockSpec((tm, k), lambda i, j: (i, 0)),
            pl.BlockSpec((1, k), lambda i, j: (0, 0)),
            pl.BlockSpec((None, k, tn), lambda i, j: (layer, 0, j)),
        ],
        out_specs=pl.BlockSpec((tm, tn), lambda i, j: (i, j)),
        out_shape=jax.ShapeDtypeStruct((n, n_cols), out_dtype),
        scratch_shapes=[pltpu.VMEM((tm, k), BF16)],
        compiler_params=pltpu.CompilerParams(
            dimension_semantics=("arbitrary", "arbitrary"), vmem_limit_bytes=VMEM_LIMIT_LARGE),
        name="norm_matmul",
    )(x, g.reshape(1, k), w)


PREP_ROWS = 512


def _prep_head_kernel(wt_ref, o_ref):
    o_ref[...] = wt_ref[...].T.astype(BF16)


def _prep_tail_kernel(wt_ref, o_ref):
    dt0 = TAIL_VALID - TAIL_SKEW
    body = TAIL_COLS - 128
    for r in range(0, body, PREP_ROWS):
        o_ref[:, r:r + PREP_ROWS] = wt_ref[TAIL_SKEW + r:TAIL_SKEW + r + PREP_ROWS, :].T.astype(BF16)
    last = jnp.concatenate([wt_ref[dt0:TAIL_VALID, :],
                            jnp.zeros((128 - 2 * TAIL_SKEW, wt_ref.shape[1]), F32),
                            wt_ref[0:TAIL_SKEW, :]], axis=0)
    o_ref[:, body:] = last.T.astype(BF16)


def _prep_w_in(wt, layer):
    k = wt.shape[2]
    head = pl.pallas_call(
        _prep_head_kernel,
        grid=(HEAD_COLS // PREP_ROWS,),
        in_specs=[pl.BlockSpec((None, PREP_ROWS, k), lambda j: (layer, j, 0))],
        out_specs=pl.BlockSpec((k, PREP_ROWS), lambda j: (0, j)),
        out_shape=jax.ShapeDtypeStruct((k, HEAD_COLS), BF16),
        compiler_params=pltpu.CompilerParams(dimension_semantics=("arbitrary",), vmem_limit_bytes=VMEM_LIMIT),
        name="prep_w_in_head",
    )(wt)
    tail = pl.pallas_call(
        _prep_tail_kernel,
        grid=(1,),
        in_specs=[pl.BlockSpec((None, TAIL_BLOCK, k), lambda j: (layer, TAIL_START // TAIL_BLOCK, 0),
                               pipeline_mode=pl.Buffered(1))],
        out_specs=pl.BlockSpec((k, TAIL_COLS), lambda j: (0, 0)),
        out_shape=jax.ShapeDtypeStruct((k, TAIL_COLS), BF16),
        compiler_params=pltpu.CompilerParams(dimension_semantics=("arbitrary",), vmem_limit_bytes=VMEM_LIMIT_LARGE),
        name="prep_w_in_tail",
    )(wt)
    return head, tail


def _proj_post_kernel(a_ref, w_ref, g_ref, x_ref, o_ref, wb_ref):
    @pl.when(pl.program_id(0) == 0)
    def _():
        wb_ref[...] = w_ref[...].astype(BF16)

    y = jnp.dot(a_ref[...], wb_ref[...], preferred_element_type=F32)
    o_ref[...] = x_ref[...] + _rmsnorm(y, g_ref[...])


def _proj_post(a, w, layer, g, x, tm):
    n, k = a.shape
    d = w.shape[2]
    return pl.pallas_call(
        _proj_post_kernel,
        grid=(n // tm,),
        in_specs=[
            pl.BlockSpec((tm, k), lambda i: (i, 0)),
            pl.BlockSpec((None, k, d), lambda i: (layer, 0, 0), pipeline_mode=pl.Buffered(1)),
            pl.BlockSpec((1, d), lambda i: (0, 0)),
            pl.BlockSpec((tm, d), lambda i: (i, 0)),
        ],
        out_specs=pl.BlockSpec((tm, d), lambda i: (i, 0)),
        out_shape=jax.ShapeDtypeStruct((n, d), F32),
        scratch_shapes=[pltpu.VMEM((k, d), BF16)],
        compiler_params=pltpu.CompilerParams(
            dimension_semantics=("arbitrary",), vmem_limit_bytes=VMEM_LIMIT_LARGE),
        name="proj_post",
    )(a, w, g.reshape(1, d), x)


def _shift_rows(x, k):
    return pltpu.roll(x, k, axis=0) if k else x


def _causal_conv(x_ext, w_ref):
    width = w_ref.shape[0]
    if width == 4:
        x1 = _shift_rows(x_ext, 1)
        older = _shift_rows(x_ext * w_ref[1:2, :] + x1 * w_ref[0:1, :], 2)
        return x_ext * w_ref[3:4, :] + x1 * w_ref[2:3, :] + older
    acc = x_ext * w_ref[width - 1:width, :]
    for k in range(width - 1):
        acc = acc + _shift_rows(x_ext, width - 1 - k) * w_ref[k:k + 1, :]
    return acc


def _expand_heads(cols, first, n_heads, width):
    rows = cols.shape[0]
    per_vreg = 128 // width
    lane = lax.broadcasted_iota(jnp.int32, (rows, 128), 1)
    pieces = []
    for m in range(n_heads // per_vreg):
        piece = jnp.broadcast_to(cols[:, first + m * per_vreg:first + m * per_vreg + 1], (rows, 128))
        for j in range(1, per_vreg):
            c = first + m * per_vreg + j
            piece = jnp.where(lane >= j * width, jnp.broadcast_to(cols[:, c:c + 1], (rows, 128)), piece)
        pieces.append(piece)
    return jnp.concatenate(pieces, axis=1)


def _block_diag(y, lane_block, n_blocks):
    return jnp.concatenate([jnp.where(lane_block == b, y, jnp.zeros_like(y)) for b in range(n_blocks)], axis=0)


def _split_block_diag(y, lane_block, n_blocks):
    hi, lo = _split_bf16(y)
    return _block_diag(hi, lane_block, n_blocks), _block_diag(lo, lane_block, n_blocks)


def _unit_lower_inverse(ms, row, col, lane_block, n_blocks):
    same16 = (row >> 4) == (col >> 4)
    same32 = (row >> 5) == (col >> 5)
    off16 = same32 & jnp.logical_not(same16)
    eye = (row == col).astype(F32)
    split = lambda xs: [_split_bf16(x) for x in xs]
    split_bd = lambda xs: [_split_block_diag(x, lane_block, n_blocks) for x in xs]
    d = [jnp.where(same16, m, 0.0) for m in ms]
    d2 = [_dot3(x, y) for x, y in zip(split(d), split_bd(d))]
    p = [eye - x for x in d]
    d2_bd = split_bd(d2)
    p = [x + _dot3(xs, y) for x, xs, y in zip(p, split(p), d2_bd)]
    d4 = [_dot3(x, y) for x, y in zip(split(d2), d2_bd)]
    d4_bd = split_bd(d4)
    p = [x + _dot3(xs, y) for x, xs, y in zip(p, split(p), d4_bd)]
    d8_bd = split_bd([_dot3(x, y) for x, y in zip(split(d4), d4_bd)])
    p = [x + _dot3(xs, y) for x, xs, y in zip(p, split(p), d8_bd)]
    ps, p_bd = split(p), split_bd(p)
    t = split([_dot3(xs, _split_block_diag(jnp.where(off16, m, 0.0), lane_block, n_blocks)) for xs, m in zip(ps, ms)])
    q = [x - _dot3(y, xb) for x, xb, y in zip(p, p_bd, t)]
    qs, q_bd = split(q), split_bd(q)
    t = split([_dot3(xs, _split_block_diag(jnp.where(same32, 0.0, m), lane_block, n_blocks)) for xs, m in zip(qs, ms)])
    return [x - _dot3(y, xb) for x, xb, y in zip(q, q_bd, t)]


def _dot_ltri(ltri, x):
    x1 = x.astype(BF16)
    r1 = x - x1.astype(F32)
    x2 = r1.astype(BF16)
    x3 = (r1 - x2.astype(F32)).astype(BF16)
    return (jnp.dot(ltri, x1, preferred_element_type=F32) + jnp.dot(ltri, x2, preferred_element_type=F32)
            + jnp.dot(ltri, x3, preferred_element_type=F32))


def _mixer_kernel(x_ref, gpre_ref, w_head_ref, w_tail_ref, conv_a_ref, pool_w_ref, pool_scale_ref, gdn_conv_ref,
                  gdn_norm_ref, ssm_conv_ref, ssm_bias_ref, ssm_d_ref, ssm_norm_ref,
                  alog_ref, dtbias_ref, out_ref, u_ref, s_ref, h_ref, *, tile):
    i = pl.program_id(1)
    n_chunks = tile // CHUNK
    hg = SSM_HEADS // SSM_GROUPS
    gw = hg * SSM_HEAD_DIM

    @pl.when(i == 0)
    def _():
        s_ref[...] = jnp.zeros_like(s_ref)
        h_ref[...] = jnp.zeros_like(h_ref)
        u_ref[0:HALO, :] = jnp.zeros((HALO, U_COLS), F32)

    @pl.when(i > 0)
    def _():
        u_ref[0:HALO, :] = u_ref[tile:tile + HALO, :]

    xn = _rmsnorm(x_ref[...], gpre_ref[...]).astype(BF16)
    u_ref[HALO:, HEAD_COLS:] = jnp.dot(xn, w_tail_ref[...], preferred_element_type=F32)
    for c0 in range(0, HEAD_COLS, MIX_COL_CHUNK):
        u_ref[HALO:, c0:c0 + MIX_COL_CHUNK] = jnp.dot(xn, w_head_ref[:, c0:c0 + MIX_COL_CHUNK],
                                                        preferred_element_type=F32)

    def cur(off, width, rows=slice(0, tile)):
        return u_ref[HALO + rows.start:HALO + rows.stop, off:off + width]

    def ext(off, width):
        return u_ref[:, off:off + width]

    small = cur(OFF_SMALL, 128)
    sp = _softplus(small + dtbias_ref[...])
    rate = -jnp.exp(alog_ref[...]) * sp
    beta_all = jax.nn.sigmoid(small)

    xbc = _silu(_causal_conv(ext(OFF_XBC, SSM_XBC), ssm_conv_ref)[HALO:] + ssm_bias_ref[...])

    p = ext(OFF_A + GROUP_WIDTH, GROUP_WIDTH) * ext(OFF_A + 2 * GROUP_WIDTH, GROUP_WIDTH)
    ya = cur(OFF_A, GROUP_WIDTH) * _causal_conv(p, conv_a_ref)[HALO:]
    out_ref[:, 0:GROUP_WIDTH] = ya.astype(out_ref.dtype)

    pos = (i * tile + 1 + lax.broadcasted_iota(jnp.int32, (tile, 1), 0)).astype(F32)
    for gi, window in enumerate(POOL_WINDOWS):
        off = OFF_B + gi * POOL_GROUP
        s = ext(off, POOL_GROUP)
        k = 1
        while k < window:
            s = s + _shift_rows(s, k)
            k *= 2
        pooled = s[HALO:] / jnp.minimum(pos, float(window)) - cur(off, POOL_GROUP)
        yb = _dot(pooled, pool_w_ref[gi]) * pool_scale_ref[:, gi * POOL_GROUP:(gi + 1) * POOL_GROUP]
        out_ref[:, GROUP_WIDTH + gi * POOL_GROUP:GROUP_WIDTH + (gi + 1) * POOL_GROUP] = yb.astype(out_ref.dtype)

    qkv = _silu(_causal_conv(ext(OFF_QKV, 3 * GROUP_WIDTH), gdn_conv_ref)[HALO:])

    n4 = GDN_HEADS * CHUNK
    row = lax.broadcasted_iota(jnp.int32, (CHUNK, n4), 0)
    lane = lax.broadcasted_iota(jnp.int32, (CHUNK, n4), 1)
    col = lane & (CHUNK - 1)
    lane_block = lane >> 6
    incl = row >= col
    strict = row > col
    ltri = (lax.broadcasted_iota(jnp.int32, (CHUNK, CHUNK), 0)
            >= lax.broadcasted_iota(jnp.int32, (CHUNK, CHUNK), 1)).astype(BF16)

    def decay_blocks(rate_c, first, n_heads):
        out = []
        for m in range(n_heads // GDN_HEADS):
            r = _expand_heads(rate_c, first + m * GDN_HEADS, GDN_HEADS, CHUNK)
            diff = _dot_ltri(ltri, jnp.where(strict, r, 0.0))
            out.append(jnp.where(incl, jnp.exp(jnp.where(incl, diff, 0.0)), 0.0))
        return out

    chunks = range(n_chunks)
    rows = [slice(c * CHUNK, (c + 1) * CHUNK) for c in chunks]
    acs = [_dot_ltri(ltri, rate[r]) for r in rows]
    e_acs = [jnp.exp(a) for a in acs]
    e_end = [jnp.exp(a[CHUNK - 1:CHUNK, :] - a) for a in acs]
    e_blk = [jnp.exp(a[CHUNK - 1:CHUNK, :]) for a in acs]

    def head_cols(base, h):
        return slice(base + h * GDN_HEAD_DIM, base + (h + 1) * GDN_HEAD_DIM)

    def l2norm_heads(x, scale):
        return jnp.concatenate(
            [x[:, head_cols(0, h)] * (lax.rsqrt(jnp.sum(x[:, head_cols(0, h)] * x[:, head_cols(0, h)],
                                                        axis=-1, keepdims=True) + EPS) * scale)
             for h in range(GDN_HEADS)], axis=1)

    zeros_head = jnp.zeros((CHUNK, GDN_HEAD_DIM), BF16)
    q_all = [l2norm_heads(qkv[r, 0:GROUP_WIDTH], GDN_HEAD_DIM ** -0.5) for r in rows]
    k_all = [l2norm_heads(qkv[r, GROUP_WIDTH:2 * GROUP_WIDTH], 1.0) for r in rows]
    v_all = [qkv[r, 2 * GROUP_WIDTH:3 * GROUP_WIDTH] for r in rows]
    beta_x = [_expand_heads(beta_all[r], SMALL_B, GDN_HEADS, GDN_HEAD_DIM) for r in rows]
    e_g_x = [_expand_heads(e, SMALL_A, GDN_HEADS, GDN_HEAD_DIM) for e in e_acs]
    kb = [x * b for x, b in zip(k_all, beta_x)]
    k_rows = []
    for x in k_all:
        xb = x.astype(BF16)
        k_rows.append(jnp.concatenate(
            [jnp.concatenate([xb[:, head_cols(0, h)] if h == g else zeros_head for h in range(GDN_HEADS)], axis=1)
             for g in range(GDN_HEADS)], axis=0))
    decay = [decay_blocks(rate[r], SMALL_A, GDN_HEADS)[0] for r in rows]
    m_low = [jnp.where(strict, _dot_nt(x, y) * dc, 0.0) for x, y, dc in zip(kb, k_rows, decay)]
    qk = [_dot_nt(x, y) * dc for x, y, dc in zip(q_all, k_rows, decay)]
    t_inv = _unit_lower_inverse(m_low, row, col, lane_block, GDN_HEADS)
    zeros_rhs = jnp.zeros((CHUNK, 2 * GDN_HEAD_DIM), F32)
    sol = []
    for c in chunks:
        vb, kg = v_all[c] * beta_x[c], kb[c] * e_g_x[c]
        rhs = jnp.concatenate(
            [jnp.concatenate([jnp.concatenate([vb[:, head_cols(0, h)], kg[:, head_cols(0, h)]], axis=1)
                              if h == g else zeros_rhs for h in range(GDN_HEADS)], axis=1)
             for g in range(GDN_HEADS)], axis=0)
        sol.append(_dot3(_split_bf16(t_inv[c]), _split_bf16(rhs)))
    q_dec = [x * e for x, e in zip(q_all, e_g_x)]
    k_end = [x * _expand_heads(e, SMALL_A, GDN_HEADS, GDN_HEAD_DIM) for x, e in zip(k_all, e_end)]

    x_in, y_diag, s_inc, c_mat, e_acs_x, e_blk_x = [], [], [], [], [], []
    for c in chunks:
        r = rows[c]
        x_in.append(xbc[r, :GROUP_WIDTH])
        x_dt = x_in[c] * _expand_heads(sp[r], SMALL_DT, SSM_HEADS, SSM_HEAD_DIM)
        x_end = x_dt * _expand_heads(e_end[c], SMALL_DT, SSM_HEADS, SSM_HEAD_DIM)
        e_acs_x.append(_expand_heads(e_acs[c], SMALL_DT, SSM_HEADS, SSM_HEAD_DIM))
        e_blk_x.append(_expand_heads(e_blk[c], SMALL_DT, SSM_HEADS, SSM_HEAD_DIM))
        l_m = decay_blocks(rate[r], SMALL_DT, SSM_HEADS)
        for g in range(SSM_GROUPS):
            b_g = xbc[r, GROUP_WIDTH + g * SSM_STATE:GROUP_WIDTH + (g + 1) * SSM_STATE]
            c_g = xbc[r, GROUP_WIDTH + (SSM_GROUPS + g) * SSM_STATE:GROUP_WIDTH + (SSM_GROUPS + g + 1) * SSM_STATE]
            b_bf = b_g.astype(BF16)
            cb = _dot_nt(c_g, jnp.concatenate([b_bf] * hg, axis=0))
            x_g = x_dt[:, g * gw:(g + 1) * gw]
            y_diag.append(_dot(cb * l_m[g], _block_diag(x_g.astype(BF16), lane_block, hg)))
            s_inc.append(_dot_tn(b_bf, x_end[:, g * gw:(g + 1) * gw]))
            c_mat.append(c_g)

    s_state = [s_ref[h] for h in range(GDN_HEADS)]
    h_state = [h_ref[g] for g in range(SSM_GROUPS)]
    zeros_v = jnp.zeros((CHUNK, GDN_HEAD_DIM), BF16)
    for c in chunks:
        r = rows[c]
        hd = 2 * GDN_HEAD_DIM
        v_new = [sol[c][:, h * hd:h * hd + GDN_HEAD_DIM]
                 - _dot(sol[c][:, h * hd + GDN_HEAD_DIM:(h + 1) * hd], s_state[h]) for h in range(GDN_HEADS)]
        v_rows = jnp.concatenate(
            [jnp.concatenate([v_new[h].astype(BF16) if h == g else zeros_v for h in range(GDN_HEADS)], axis=1)
             for g in range(GDN_HEADS)], axis=0)
        o_intra = _dot(qk[c], v_rows)
        o = [_dot(q_dec[c][:, head_cols(0, h)], s_state[h]) + o_intra[:, head_cols(0, h)]
             for h in range(GDN_HEADS)]
        s_state = [s_state[h] * e_blk[c][:, SMALL_A + h:SMALL_A + h + 1]
                   + _dot_tn(k_end[c][:, head_cols(0, h)], v_new[h]) for h in range(GDN_HEADS)]
        for h in range(GDN_HEADS):
            z = cur(OFF_CZ + h * GDN_HEAD_DIM, GDN_HEAD_DIM, r)
            y = _rmsnorm(o[h], gdn_norm_ref[...]) * _silu(z)
            out_ref[r, head_cols(2 * GROUP_WIDTH, h)] = y.astype(out_ref.dtype)

        ys = []
        for g in range(SSM_GROUPS):
            n = c * SSM_GROUPS + g
            ys.append(y_diag[n] + _dot(c_mat[n], h_state[g]) * e_acs_x[c][:, g * gw:(g + 1) * gw])
            h_state[g] = h_state[g] * e_blk_x[c][:, g * gw:(g + 1) * gw] + s_inc[n]
        y = jnp.concatenate(ys, axis=1) + ssm_d_ref[...] * x_in[c]
        y = y * _silu(cur(OFF_DZ, GROUP_WIDTH, r))
        for g in range(SSM_GROUPS):
            yn = _rmsnorm(y[:, g * gw:(g + 1) * gw], ssm_norm_ref[:, g * gw:(g + 1) * gw])
            out_ref[r, 3 * GROUP_WIDTH + g * gw:3 * GROUP_WIDTH + (g + 1) * gw] = yn.astype(out_ref.dtype)
    for h in range(GDN_HEADS):
        s_ref[h] = s_state[h]
    for g in range(SSM_GROUPS):
        h_ref[g] = h_state[g]


def _mixers(x, g_pre, w_head, w_tail, batch, seq, conv_a_w, pool_w, pool_scale, gdn_conv_w, gdn_norm_g, ssm_conv_w,
            ssm_conv_b, ssm_d_x, ssm_norm_g, alog_row, dtbias_row, tile):
    n_tiles = seq // tile
    d = x.shape[1]

    def const(shape):
        return pl.BlockSpec(shape, lambda b, i: (0,) * len(shape))

    return pl.pallas_call(
        functools.partial(_mixer_kernel, tile=tile),
        grid=(batch, n_tiles),
        in_specs=[
            pl.BlockSpec((tile, d), lambda b, i: (b * n_tiles + i, 0)),
            const((1, d)),
            pl.BlockSpec((d, HEAD_COLS), lambda b, i: (0, 0), pipeline_mode=pl.Buffered(1)),
            pl.BlockSpec((d, TAIL_COLS), lambda b, i: (0, 0), pipeline_mode=pl.Buffered(1)),
            const(conv_a_w.shape), const(pool_w.shape), const(pool_scale.shape), const(gdn_conv_w.shape),
            const(gdn_norm_g.shape), const(ssm_conv_w.shape), const(ssm_conv_b.shape), const(ssm_d_x.shape),
            const(ssm_norm_g.shape), const(alog_row.shape), const(dtbias_row.shape),
        ],
        out_specs=pl.BlockSpec((tile, D_MODEL), lambda b, i: (b * n_tiles + i, 0)),
        out_shape=jax.ShapeDtypeStruct((batch * seq, D_MODEL), BF16),
        scratch_shapes=[
            pltpu.VMEM((HALO + tile, U_COLS), F32),
            pltpu.VMEM((GDN_HEADS, GDN_HEAD_DIM, GDN_HEAD_DIM), F32),
            pltpu.VMEM((SSM_GROUPS, SSM_STATE, (SSM_HEADS // SSM_GROUPS) * SSM_HEAD_DIM), F32),
        ],
        compiler_params=pltpu.CompilerParams(
            dimension_semantics=("arbitrary", "arbitrary"), vmem_limit_bytes=VMEM_LIMIT_LARGE),
        name="mixers",
    )(x, g_pre.reshape(1, d), w_head, w_tail, conv_a_w, pool_w, pool_scale, gdn_conv_w, gdn_norm_g, ssm_conv_w,
      ssm_conv_b, ssm_d_x, ssm_norm_g, alog_row, dtbias_row)


def _xattn_kernel(x_ref, g_ref, wq_ref, kv_ref, o_ref, wb_ref):
    @pl.when((pl.program_id(0) == 0) & (pl.program_id(1) == 0))
    def _():
        wb_ref[...] = wq_ref[...].astype(BF16)

    xn = _rmsnorm(x_ref[...], g_ref[...]).astype(BF16)
    q = jnp.dot(xn, wb_ref[...], preferred_element_type=F32)
    for h in range(XA_HEADS):
        lo = h * XA_HEAD_DIM
        k_h = kv_ref[:, lo:lo + XA_HEAD_DIM]
        v_h = kv_ref[:, D_MODEL + lo:D_MODEL + lo + XA_HEAD_DIM]
        s = _dot_nt(q[:, lo:lo + XA_HEAD_DIM], k_h) * XA_HEAD_DIM ** -0.5
        e = jnp.exp(s - jnp.max(s, axis=-1, keepdims=True))
        p = e / jnp.sum(e, axis=-1, keepdims=True)
        o_ref[:, lo:lo + XA_HEAD_DIM] = _dot(p, v_h).astype(o_ref.dtype)


def _xattn(x, g, wq, layer, kv, batch, seq, tq):
    n, d = x.shape
    n_tiles = seq // tq
    return pl.pallas_call(
        _xattn_kernel,
        grid=(batch, n_tiles),
        in_specs=[
            pl.BlockSpec((tq, d), lambda b, i: (b * n_tiles + i, 0)),
            pl.BlockSpec((1, d), lambda b, i: (0, 0)),
            pl.BlockSpec((None, d, d), lambda b, i: (layer, 0, 0), pipeline_mode=pl.Buffered(1)),
            pl.BlockSpec((MEM_LEN, 2 * d), lambda b, i: (b, 0)),
        ],
        out_specs=pl.BlockSpec((tq, d), lambda b, i: (b * n_tiles + i, 0)),
        out_shape=jax.ShapeDtypeStruct((n, d), BF16),
        scratch_shapes=[pltpu.VMEM((d, d), BF16)],
        compiler_params=pltpu.CompilerParams(
            dimension_semantics=("arbitrary", "arbitrary"), vmem_limit_bytes=VMEM_LIMIT),
        name="xattn",
    )(x, g.reshape(1, d), wq, kv)


def _ffn_kernel(x_ref, gpre_ref, wg_ref, wu_ref, wd_ref, gpost_ref, o_ref, xn_ref):
    j = pl.program_id(1)
    last = pl.num_programs(1) - 1
    tm = o_ref.shape[0]

    def partial_out(rows):
        xn = xn_ref[rows, :]
        gate = jnp.dot(xn, wg_ref[...].astype(BF16), preferred_element_type=F32)
        up = jnp.dot(xn, wu_ref[...].astype(BF16), preferred_element_type=F32)
        return jnp.dot((_silu(gate) * up).astype(BF16), wd_ref[...].astype(BF16), preferred_element_type=F32)

    chunks = [slice(r, r + tm // FFN_EDGE_CHUNKS) for r in range(0, tm, tm // FFN_EDGE_CHUNKS)]

    @pl.when(j == 0)
    def _():
        for rows in chunks:
            xn_ref[rows, :] = _rmsnorm(x_ref[rows, :], gpre_ref[...]).astype(BF16)
            o_ref[rows, :] = partial_out(rows)

    @pl.when((j > 0) & (j < last))
    def _():
        o_ref[...] += partial_out(slice(0, tm))

    @pl.when(j == last)
    def _():
        for rows in chunks:
            y = o_ref[rows, :] + partial_out(rows)
            o_ref[rows, :] = x_ref[rows, :] + _rmsnorm(y, gpost_ref[...])


def _ffn(x, gpre, w_gu, w_down, layer, gpost, tm, tf):
    n, d = x.shape
    n_ff = D_FF // tf
    return pl.pallas_call(
        _ffn_kernel,
        grid=(n // tm, n_ff),
        in_specs=[
            pl.BlockSpec((tm, d), lambda i, j: (i, 0)),
            pl.BlockSpec((1, d), lambda i, j: (0, 0)),
            pl.BlockSpec((None, d, tf), lambda i, j: (layer, 0, j)),
            pl.BlockSpec((None, d, tf), lambda i, j: (layer, 0, j + n_ff)),
            pl.BlockSpec((None, tf, d), lambda i, j: (layer, j, 0)),
            pl.BlockSpec((1, d), lambda i, j: (0, 0)),
        ],
        out_specs=pl.BlockSpec((tm, d), lambda i, j: (i, 0)),
        out_shape=jax.ShapeDtypeStruct((n, d), F32),
        scratch_shapes=[pltpu.VMEM((tm, d), BF16)],
        compiler_params=pltpu.CompilerParams(
            dimension_semantics=("arbitrary", "arbitrary"), vmem_limit_bytes=VMEM_LIMIT_FFN),
        name="ffn",
    )(x, gpre.reshape(1, d), w_gu, w_gu, w_down, gpost.reshape(1, d))


def _small_row(gdn_vals, ssm_vals):
    row = jnp.zeros((1, 128), F32)
    row = row.at[0, SMALL_A:SMALL_A + GDN_HEADS].set(gdn_vals.astype(F32))
    return row.at[0, SMALL_DT:SMALL_DT + SSM_HEADS].set(ssm_vals.astype(F32))


def kernel(x, mem, norm_g, w_in, conv_a_w, pool_w, pool_scale, gdn_conv_w, gdn_A_log, gdn_dt_bias, gdn_norm_g,
           ssm_conv_w, ssm_conv_b, ssm_A_log, ssm_dt_bias, ssm_D, ssm_norm_g, w_out, xa_wq, xa_wkv, xa_wo,
           ffn_w_gu, ffn_w_down):
    batch, seq, d = x.shape
    depth = w_in.shape[0]
    xf = x.reshape(batch * seq, d)
    memf = mem.reshape(batch * MEM_LEN, d)
    w_in_t = jnp.swapaxes(w_in, 1, 2)
    for l in range(depth):
        g = norm_g[l]
        mixed = _mixers(
            xf, g[0], *_prep_w_in(w_in_t, l), batch, seq, conv_a_w[l], pool_w[l], pool_scale[l].reshape(1, -1),
            gdn_conv_w[l], gdn_norm_g[l].reshape(1, -1), ssm_conv_w[l], ssm_conv_b[l].reshape(1, -1),
            jnp.repeat(ssm_D[l], SSM_HEAD_DIM).reshape(1, -1), ssm_norm_g[l].reshape(1, -1),
            _small_row(gdn_A_log[l], ssm_A_log[l]), _small_row(gdn_dt_bias[l], ssm_dt_bias[l]),
            tile=MIX_TILE)
        xf = _proj_post(mixed, w_out, l, g[1], xf, tm=ROW_TILE)
        kv = _norm_matmul(memf, g[4], xa_wkv, l, 2 * d, BF16, tm=batch * MEM_LEN, tn=KV_COL_TILE)
        att = _xattn(xf, g[2], xa_wq, l, kv, batch, seq, tq=ROW_TILE)
        xf = _proj_post(att, xa_wo, l, g[3], xf, tm=ROW_TILE)
        xf = _ffn(xf, g[5], ffn_w_gu, ffn_w_down, l, g[6], tm=FFN_ROW_TILE, tf=FFN_COL_TILE)
    return xf.reshape(batch, seq, d)
```

```python
import functools

import jax
import jax.numpy as jnp
from jax import lax
from jax.experimental import pallas as pl
from jax.experimental.pallas import tpu as pltpu

F32 = jnp.float32
BF16 = jnp.bfloat16
EPS = 1e-6

D_MODEL = 2048
GROUP_WIDTH = 512
CHUNK = 64
POOL_WINDOWS = (2, 4, 8, 16)
POOL_GROUP = 128
GDN_HEADS = 4
GDN_HEAD_DIM = 128
SSM_HEADS = 8
SSM_HEAD_DIM = 64
SSM_GROUPS = 2
SSM_STATE = 128
SSM_XBC = 1024
MEM_LEN = 256
XA_HEADS = 4
XA_HEAD_DIM = 512
D_FF = 5632

IN_COLS = 5648
HEAD_COLS = 4096
TAIL_START = 4096
TAIL_VALID = IN_COLS - TAIL_START
TAIL_COLS = 1664
TAIL_BLOCK = 2048
TAIL_SKEW = 8
U_COLS = HEAD_COLS + TAIL_COLS
OFF_A = 0
OFF_B = 1536
OFF_QKV = 2048
OFF_CZ = 3584
OFF_DZ = 4096
OFF_XBC = 4608
OFF_SMALL = 5632
SMALL_DT = 0
SMALL_A = 120
SMALL_B = 124
HALO = 16

MIX_TILE = 4 * CHUNK
MIX_COL_CHUNK = 1024
ROW_TILE = 512
KV_COL_TILE = 512
FFN_ROW_TILE = 1024
FFN_COL_TILE = 256
FFN_EDGE_CHUNKS = 4

VMEM_LIMIT = 48 * 1024 * 1024
VMEM_LIMIT_LARGE = 56 * 1024 * 1024
VMEM_LIMIT_FFN = 60000 * 1024


def _rmsnorm(x, g):
    return x * lax.rsqrt(jnp.mean(x * x, axis=-1, keepdims=True) + EPS) * g


def _silu(x):
    return x * jax.nn.sigmoid(x)


def _softplus(x):
    return jnp.maximum(x, 0.0) + jnp.log1p(jnp.exp(-jnp.abs(x)))


def _dot(a, b):
    return jnp.dot(a.astype(BF16), b.astype(BF16), preferred_element_type=F32)


def _dot_nt(a, b):
    return lax.dot_general(a.astype(BF16), b.astype(BF16), (((1,), (1,)), ((), ())),
                           preferred_element_type=F32)


def _dot_tn(a, b):
    return lax.dot_general(a.astype(BF16), b.astype(BF16), (((0,), (0,)), ((), ())),
                           preferred_element_type=F32)


def _split_bf16(a):
    hi = a.astype(BF16)
    return hi, (a - hi.astype(F32)).astype(BF16)


def _dot3(a, b):
    return (jnp.dot(a[0], b[0], preferred_element_type=F32)
            + jnp.dot(a[0], b[1], preferred_element_type=F32)
            + jnp.dot(a[1], b[0], preferred_element_type=F32))


def _norm_matmul_kernel(x_ref, g_ref, w_ref, o_ref, xn_ref):
    @pl.when(pl.program_id(1) == 0)
    def _():
        xn_ref[...] = _rmsnorm(x_ref[...], g_ref[...]).astype(BF16)

    o_ref[...] = jnp.dot(xn_ref[...], w_ref[...].astype(BF16), preferred_element_type=F32).astype(o_ref.dtype)


def _norm_matmul(x, g, w, layer, n_cols, out_dtype, tm, tn):
    n, k = x.shape
    return pl.pallas_call(
        _norm_matmul_kernel,
        grid=(n // tm, n_cols // tn),
        in_specs=[
            pl.BlockSpec((tm, k), lambda i, j: (i, 0)),
            pl.BlockSpec((1, k), lambda i, j: (0, 0)),
            pl.BlockSpec((None, k, tn), lambda i, j: (layer, 0, j)),
        ],
        out_specs=pl.BlockSpec((tm, tn), lambda i, j: (i, j)),
        out_shape=jax.ShapeDtypeStruct((n, n_cols), out_dtype),
        scratch_shapes=[pltpu.VMEM((tm, k), BF16)],
        compiler_params=pltpu.CompilerParams(
            dimension_semantics=("arbitrary", "arbitrary"), vmem_limit_bytes=VMEM_LIMIT_LARGE),
        name="norm_matmul",
    )(x, g.reshape(1, k), w)


PREP_ROWS = 512


def _prep_head_kernel(wt_ref, o_ref):
    o_ref[...] = wt_ref[...].T.astype(BF16)


def _prep_tail_kernel(wt_ref, o_ref):
    dt0 = TAIL_VALID - TAIL_SKEW
    body = TAIL_COLS - 128
    for r in range(0, body, PREP_ROWS):
        o_ref[:, r:r + PREP_ROWS] = wt_ref[TAIL_SKEW + r:TAIL_SKEW + r + PREP_ROWS, :].T.astype(BF16)
    last = jnp.concatenate([wt_ref[dt0:TAIL_VALID, :],
                            jnp.zeros((128 - 2 * TAIL_SKEW, wt_ref.shape[1]), F32),
                            wt_ref[0:TAIL_SKEW, :]], axis=0)
    o_ref[:, body:] = last.T.astype(BF16)


def _prep_w_in(wt, layer):
    k = wt.shape[2]
    head = pl.pallas_call(
        _prep_head_kernel,
        grid=(HEAD_COLS // PREP_ROWS,),
        in_specs=[pl.BlockSpec((None, PREP_ROWS, k), lambda j: (layer, j, 0))],
        out_specs=pl.BlockSpec((k, PREP_ROWS), lambda j: (0, j)),
        out_shape=jax.ShapeDtypeStruct((k, HEAD_COLS), BF16),
        compiler_params=pltpu.CompilerParams(dimension_semantics=("arbitrary",), vmem_limit_bytes=VMEM_LIMIT),
        name="prep_w_in_head",
    )(wt)
    tail = pl.pallas_call(
        _prep_tail_kernel,
        grid=(1,),
        in_specs=[pl.BlockSpec((None, TAIL_BLOCK, k), lambda j: (layer, TAIL_START // TAIL_BLOCK, 0),
                               pipeline_mode=pl.Buffered(1))],
        out_specs=pl.BlockSpec((k, TAIL_COLS), lambda j: (0, 0)),
        out_shape=jax.ShapeDtypeStruct((k, TAIL_COLS), BF16),
        compiler_params=pltpu.CompilerParams(dimension_semantics=("arbitrary",), vmem_limit_bytes=VMEM_LIMIT_LARGE),
        name="prep_w_in_tail",
    )(wt)
    return head, tail


def _proj_post_kernel(a_ref, w_ref, g_ref, x_ref, o_ref, wb_ref):
    @pl.when(pl.program_id(0) == 0)
    def _():
        wb_ref[...] = w_ref[...].astype(BF16)

    y = jnp.dot(a_ref[...], wb_ref[...], preferred_element_type=F32)
    o_ref[...] = x_ref[...] + _rmsnorm(y, g_ref[...])


def _proj_post(a, w, layer, g, x, tm):
    n, k = a.shape
    d = w.shape[2]
    return pl.pallas_call(
        _proj_post_kernel,
        grid=(n // tm,),
        in_specs=[
            pl.BlockSpec((tm, k), lambda i: (i, 0)),
            pl.BlockSpec((None, k, d), lambda i: (layer, 0, 0), pipeline_mode=pl.Buffered(1)),
            pl.BlockSpec((1, d), lambda i: (0, 0)),
            pl.BlockSpec((tm, d), lambda i: (i, 0)),
        ],
        out_specs=pl.BlockSpec((tm, d), lambda i: (i, 0)),
        out_shape=jax.ShapeDtypeStruct((n, d), F32),
        scratch_shapes=[pltpu.VMEM((k, d), BF16)],
        compiler_params=pltpu.CompilerParams(
            dimension_semantics=("arbitrary",), vmem_limit_bytes=VMEM_LIMIT_LARGE),
        name="proj_post",
    )(a, w, g.reshape(1, d), x)


def _shift_rows(x, k):
    return pltpu.roll(x, k, axis=0) if k else x


def _causal_conv(x_ext, w_ref):
    width = w_ref.shape[0]
    if width == 4:
        x1 = _shift_rows(x_ext, 1)
        older = _shift_rows(x_ext * w_ref[1:2, :] + x1 * w_ref[0:1, :], 2)
        return x_ext * w_ref[3:4, :] + x1 * w_ref[2:3, :] + older
    acc = x_ext * w_ref[width - 1:width, :]
    for k in range(width - 1):
        acc = acc + _shift_rows(x_ext, width - 1 - k) * w_ref[k:k + 1, :]
    return acc


def _expand_heads(cols, first, n_heads, width):
    rows = cols.shape[0]
    per_vreg = 128 // width
    lane = lax.broadcasted_iota(jnp.int32, (rows, 128), 1)
    pieces = []
    for m in range(n_heads // per_vreg):
        piece = jnp.broadcast_to(cols[:, first + m * per_vreg:first + m * per_vreg + 1], (rows, 128))
        for j in range(1, per_vreg):
            c = first + m * per_vreg + j
            piece = jnp.where(lane >= j * width, jnp.broadcast_to(cols[:, c:c + 1], (rows, 128)), piece)
        pieces.append(piece)
    return jnp.concatenate(pieces, axis=1)


def _block_diag(y, lane_block, n_blocks):
    return jnp.concatenate([jnp.where(lane_block == b, y, jnp.zeros_like(y)) for b in range(n_blocks)], axis=0)


def _split_block_diag(y, lane_block, n_blocks):
    hi, lo = _split_bf16(y)
    return _block_diag(hi, lane_block, n_blocks), _block_diag(lo, lane_block, n_blocks)


def _unit_lower_inverse(ms, row, col, lane_block, n_blocks):
    same16 = (row >> 4) == (col >> 4)
    same32 = (row >> 5) == (col >> 5)
    off16 = same32 & jnp.logical_not(same16)
    eye = (row == col).astype(F32)
    split = lambda xs: [_split_bf16(x) for x in xs]
    split_bd = lambda xs: [_split_block_diag(x, lane_block, n_blocks) for x in xs]
    d = [jnp.where(same16, m, 0.0) for m in ms]
    d2 = [_dot3(x, y) for x, y in zip(split(d), split_bd(d))]
    p = [eye - x for x in d]
    d2_bd = split_bd(d2)
    p = [x + _dot3(xs, y) for x, xs, y in zip(p, split(p), d2_bd)]
    d4 = [_dot3(x, y) for x, y in zip(split(d2), d2_bd)]
    d4_bd = split_bd(d4)
    p = [x + _dot3(xs, y) for x, xs, y in zip(p, split(p), d4_bd)]
    d8_bd = split_bd([_dot3(x, y) for x, y in zip(split(d4), d4_bd)])
    p = [x + _dot3(xs, y) for x, xs, y in zip(p, split(p), d8_bd)]
    ps, p_bd = split(p), split_bd(p)
    t = split([_dot3(xs, _split_block_diag(jnp.where(off16, m, 0.0), lane_block, n_blocks)) for xs, m in zip(ps, ms)])
    q = [x - _dot3(y, xb) for x, xb, y in zip(p, p_bd, t)]
    qs, q_bd = split(q), split_bd(q)
    t = split([_dot3(xs, _split_block_diag(jnp.where(same32, 0.0, m), lane_block, n_blocks)) for xs, m in zip(qs, ms)])
    return [x - _dot3(y, xb) for x, xb, y in zip(q, q_bd, t)]


def _dot_ltri(ltri, x):
    x1 = x.astype(BF16)
    r1 = x - x1.astype(F32)
    x2 = r1.astype(BF16)
    x3 = (r1 - x2.astype(F32)).astype(BF16)
    return (jnp.dot(ltri, x1, preferred_element_type=F32) + jnp.dot(ltri, x2, preferred_element_type=F32)
            + jnp.dot(ltri, x3, preferred_element_type=F32))


def _mixer_kernel(x_ref, gpre_ref, w_head_ref, w_tail_ref, conv_a_ref, pool_w_ref, pool_scale_ref, gdn_conv_ref,
                  gdn_norm_ref, ssm_conv_ref, ssm_bias_ref, ssm_d_ref, ssm_norm_ref,
                  alog_ref, dtbias_ref, out_ref, u_ref, s_ref, h_ref, *, tile):
    i = pl.program_id(1)
    n_chunks = tile // CHUNK
    hg = SSM_HEADS // SSM_GROUPS
    gw = hg * SSM_HEAD_DIM

    @pl.when(i == 0)
    def _():
        s_ref[...] = jnp.zeros_like(s_ref)
        h_ref[...] = jnp.zeros_like(h_ref)
        u_ref[0:HALO, :] = jnp.zeros((HALO, U_COLS), F32)

    @pl.when(i > 0)
    def _():
        u_ref[0:HALO, :] = u_ref[tile:tile + HALO, :]

    xn = _rmsnorm(x_ref[...], gpre_ref[...]).astype(BF16)
    for c0 in range(0, HEAD_COLS, MIX_COL_CHUNK):
        u_ref[HALO:, c0:c0 + MIX_COL_CHUNK] = jnp.dot(xn, w_head_ref[:, c0:c0 + MIX_COL_CHUNK],
                                                        preferred_element_type=F32)
    u_ref[HALO:, HEAD_COLS:] = jnp.dot(xn, w_tail_ref[...], preferred_element_type=F32)

    def cur(off, width, rows=slice(0, tile)):
        return u_ref[HALO + rows.start:HALO + rows.stop, off:off + width]

    def ext(off, width):
        return u_ref[:, off:off + width]

    p = ext(OFF_A + GROUP_WIDTH, GROUP_WIDTH) * ext(OFF_A + 2 * GROUP_WIDTH, GROUP_WIDTH)
    ya = cur(OFF_A, GROUP_WIDTH) * _causal_conv(p, conv_a_ref)[HALO:]
    out_ref[:, 0:GROUP_WIDTH] = ya.astype(out_ref.dtype)

    pos = (i * tile + 1 + lax.broadcasted_iota(jnp.int32, (tile, 1), 0)).astype(F32)
    for gi, window in enumerate(POOL_WINDOWS):
        off = OFF_B + gi * POOL_GROUP
        s = ext(off, POOL_GROUP)
        k = 1
        while k < window:
            s = s + _shift_rows(s, k)
            k *= 2
        pooled = s[HALO:] / jnp.minimum(pos, float(window)) - cur(off, POOL_GROUP)
        yb = _dot(pooled, pool_w_ref[gi]) * pool_scale_ref[:, gi * POOL_GROUP:(gi + 1) * POOL_GROUP]
        out_ref[:, GROUP_WIDTH + gi * POOL_GROUP:GROUP_WIDTH + (gi + 1) * POOL_GROUP] = yb.astype(out_ref.dtype)

    small = cur(OFF_SMALL, 128)
    sp = _softplus(small + dtbias_ref[...])
    rate = -jnp.exp(alog_ref[...]) * sp
    beta_all = jax.nn.sigmoid(small)

    qkv = _silu(_causal_conv(ext(OFF_QKV, 3 * GROUP_WIDTH), gdn_conv_ref)[HALO:])
    xbc = _silu(_causal_conv(ext(OFF_XBC, SSM_XBC), ssm_conv_ref)[HALO:] + ssm_bias_ref[...])

    n4 = GDN_HEADS * CHUNK
    row = lax.broadcasted_iota(jnp.int32, (CHUNK, n4), 0)
    lane = lax.broadcasted_iota(jnp.int32, (CHUNK, n4), 1)
    col = lane & (CHUNK - 1)
    lane_block = lane >> 6
    incl = row >= col
    strict = row > col
    ltri = (lax.broadcasted_iota(jnp.int32, (CHUNK, CHUNK), 0)
            >= lax.broadcasted_iota(jnp.int32, (CHUNK, CHUNK), 1)).astype(BF16)

    def decay_blocks(rate_c, first, n_heads):
        out = []
        for m in range(n_heads // GDN_HEADS):
            r = _expand_heads(rate_c, first + m * GDN_HEADS, GDN_HEADS, CHUNK)
            diff = _dot_ltri(ltri, jnp.where(strict, r, 0.0))
            out.append(jnp.where(incl, jnp.exp(jnp.where(incl, diff, 0.0)), 0.0))
        return out

    chunks = range(n_chunks)
    rows = [slice(c * CHUNK, (c + 1) * CHUNK) for c in chunks]
    acs = [_dot_ltri(ltri, rate[r]) for r in rows]
    e_acs = [jnp.exp(a) for a in acs]
    e_end = [jnp.exp(a[CHUNK - 1:CHUNK, :] - a) for a in acs]
    e_blk = [jnp.exp(a[CHUNK - 1:CHUNK, :]) for a in acs]

    def head_cols(base, h):
        return slice(base + h * GDN_HEAD_DIM, base + (h + 1) * GDN_HEAD_DIM)

    def l2norm_heads(x, scale):
        return jnp.concatenate(
            [x[:, head_cols(0, h)] * (lax.rsqrt(jnp.sum(x[:, head_cols(0, h)] * x[:, head_cols(0, h)],
                                                        axis=-1, keepdims=True) + EPS) * scale)
             for h in range(GDN_HEADS)], axis=1)

    zeros_head = jnp.zeros((CHUNK, GDN_HEAD_DIM), BF16)
    q_all = [l2norm_heads(qkv[r, 0:GROUP_WIDTH], GDN_HEAD_DIM ** -0.5) for r in rows]
    k_all = [l2norm_heads(qkv[r, GROUP_WIDTH:2 * GROUP_WIDTH], 1.0) for r in rows]
    v_all = [qkv[r, 2 * GROUP_WIDTH:3 * GROUP_WIDTH] for r in rows]
    beta_x = [_expand_heads(beta_all[r], SMALL_B, GDN_HEADS, GDN_HEAD_DIM) for r in rows]
    e_g_x = [_expand_heads(e, SMALL_A, GDN_HEADS, GDN_HEAD_DIM) for e in e_acs]
    kb = [x * b for x, b in zip(k_all, beta_x)]
    k_rows = []
    for x in k_all:
        xb = x.astype(BF16)
        k_rows.append(jnp.concatenate(
            [jnp.concatenate([xb[:, head_cols(0, h)] if h == g else zeros_head for h in range(GDN_HEADS)], axis=1)
             for g in range(GDN_HEADS)], axis=0))
    decay = [decay_blocks(rate[r], SMALL_A, GDN_HEADS)[0] for r in rows]
    m_low = [jnp.where(strict, _dot_nt(x, y) * dc, 0.0) for x, y, dc in zip(kb, k_rows, decay)]
    qk = [_dot_nt(x, y) * dc for x, y, dc in zip(q_all, k_rows, decay)]
    t_inv = _unit_lower_inverse(m_low, row, col, lane_block, GDN_HEADS)
    zeros_rhs = jnp.zeros((CHUNK, 2 * GDN_HEAD_DIM), F32)
    sol = []
    for c in chunks:
        vb, kg = v_all[c] * beta_x[c], kb[c] * e_g_x[c]
        rhs = jnp.concatenate(
            [jnp.concatenate([jnp.concatenate([vb[:, head_cols(0, h)], kg[:, head_cols(0, h)]], axis=1)
                              if h == g else zeros_rhs for h in range(GDN_HEADS)], axis=1)
             for g in range(GDN_HEADS)], axis=0)
        sol.append(_dot3(_split_bf16(t_inv[c]), _split_bf16(rhs)))
    q_dec = [x * e for x, e in zip(q_all, e_g_x)]
    k_end = [x * _expand_heads(e, SMALL_A, GDN_HEADS, GDN_HEAD_DIM) for x, e in zip(k_all, e_end)]

    x_in, y_diag, s_inc, c_mat, e_acs_x, e_blk_x = [], [], [], [], [], []
    for c in chunks:
        r = rows[c]
        x_in.append(xbc[r, :GROUP_WIDTH])
        x_dt = x_in[c] * _expand_heads(sp[r], SMALL_DT, SSM_HEADS, SSM_HEAD_DIM)
        x_end = x_dt * _expand_heads(e_end[c], SMALL_DT, SSM_HEADS, SSM_HEAD_DIM)
        e_acs_x.append(_expand_heads(e_acs[c], SMALL_DT, SSM_HEADS, SSM_HEAD_DIM))
        e_blk_x.append(_expand_heads(e_blk[c], SMALL_DT, SSM_HEADS, SSM_HEAD_DIM))
        l_m = decay_blocks(rate[r], SMALL_DT, SSM_HEADS)
        for g in range(SSM_GROUPS):
            b_g = xbc[r, GROUP_WIDTH + g * SSM_STATE:GROUP_WIDTH + (g + 1) * SSM_STATE]
            c_g = xbc[r, GROUP_WIDTH + (SSM_GROUPS + g) * SSM_STATE:GROUP_WIDTH + (SSM_GROUPS + g + 1) * SSM_STATE]
            b_bf = b_g.astype(BF16)
            cb = _dot_nt(c_g, jnp.concatenate([b_bf] * hg, axis=0))
            x_g = x_dt[:, g * gw:(g + 1) * gw]
            y_diag.append(_dot(cb * l_m[g], _block_diag(x_g.astype(BF16), lane_block, hg)))
            s_inc.append(_dot_tn(b_bf, x_end[:, g * gw:(g + 1) * gw]))
            c_mat.append(c_g)

    s_state = [s_ref[h] for h in range(GDN_HEADS)]
    h_state = [h_ref[g] for g in range(SSM_GROUPS)]
    zeros_v = jnp.zeros((CHUNK, GDN_HEAD_DIM), BF16)
    for c in chunks:
        r = rows[c]
        hd = 2 * GDN_HEAD_DIM
        v_new = [sol[c][:, h * hd:h * hd + GDN_HEAD_DIM]
                 - _dot(sol[c][:, h * hd + GDN_HEAD_DIM:(h + 1) * hd], s_state[h]) for h in range(GDN_HEADS)]
        v_rows = jnp.concatenate(
            [jnp.concatenate([v_new[h].astype(BF16) if h == g else zeros_v for h in range(GDN_HEADS)], axis=1)
             for g in range(GDN_HEADS)], axis=0)
        o_intra = _dot(qk[c], v_rows)
        o = [_dot(q_dec[c][:, head_cols(0, h)], s_state[h]) + o_intra[:, head_cols(0, h)]
             for h in range(GDN_HEADS)]
        s_state = [s_state[h] * e_blk[c][:, SMALL_A + h:SMALL_A + h + 1]
                   + _dot_tn(k_end[c][:, head_cols(0, h)], v_new[h]) for h in range(GDN_HEADS)]
        for h in range(GDN_HEADS):
            z = cur(OFF_CZ + h * GDN_HEAD_DIM, GDN_HEAD_DIM, r)
            y = _rmsnorm(o[h], gdn_norm_ref[...]) * _silu(z)
            out_ref[r, head_cols(2 * GROUP_WIDTH, h)] = y.astype(out_ref.dtype)

        ys = []
        for g in range(SSM_GROUPS):
            n = c * SSM_GROUPS + g
            ys.append(y_diag[n] + _dot(c_mat[n], h_state[g]) * e_acs_x[c][:, g * gw:(g + 1) * gw])
            h_state[g] = h_state[g] * e_blk_x[c][:, g * gw:(g + 1) * gw] + s_inc[n]
        y = jnp.concatenate(ys, axis=1) + ssm_d_ref[...] * x_in[c]
        y = y * _silu(cur(OFF_DZ, GROUP_WIDTH, r))
        for g in range(SSM_GROUPS):
            yn = _rmsnorm(y[:, g * gw:(g + 1) * gw], ssm_norm_ref[:, g * gw:(g + 1) * gw])
            out_ref[r, 3 * GROUP_WIDTH + g * gw:3 * GROUP_WIDTH + (g + 1) * gw] = yn.astype(out_ref.dtype)
    for h in range(GDN_HEADS):
        s_ref[h] = s_state[h]
    for g in range(SSM_GROUPS):
        h_ref[g] = h_state[g]


def _mixers(x, g_pre, w_head, w_tail, batch, seq, conv_a_w, pool_w, pool_scale, gdn_conv_w, gdn_norm_g, ssm_conv_w,
            ssm_conv_b, ssm_d_x, ssm_norm_g, alog_row, dtbias_row, tile):
    n_tiles = seq // tile
    d = x.shape[1]

    def const(shape):
        return pl.BlockSpec(shape, lambda b, i: (0,) * len(shape))

    return pl.pallas_call(
        functools.partial(_mixer_kernel, tile=tile),
        grid=(batch, n_tiles),
        in_specs=[
            pl.BlockSpec((tile, d), lambda b, i: (b * n_tiles + i, 0)),
            const((1, d)),
            pl.BlockSpec((d, HEAD_COLS), lambda b, i: (0, 0), pipeline_mode=pl.Buffered(1)),
            pl.BlockSpec((d, TAIL_COLS), lambda b, i: (0, 0), pipeline_mode=pl.Buffered(1)),
            const(conv_a_w.shape), const(pool_w.shape), const(pool_scale.shape), const(gdn_conv_w.shape),
            const(gdn_norm_g.shape), const(ssm_conv_w.shape), const(ssm_conv_b.shape), const(ssm_d_x.shape),
            const(ssm_norm_g.shape), const(alog_row.shape), const(dtbias_row.shape),
        ],
        out_specs=pl.BlockSpec((tile, D_MODEL), lambda b, i: (b * n_tiles + i, 0)),
        out_shape=jax.ShapeDtypeStruct((batch * seq, D_MODEL), BF16),
        scratch_shapes=[
            pltpu.VMEM((HALO + tile, U_COLS), F32),
            pltpu.VMEM((GDN_HEADS, GDN_HEAD_DIM, GDN_HEAD_DIM), F32),
            pltpu.VMEM((SSM_GROUPS, SSM_STATE, (SSM_HEADS // SSM_GROUPS) * SSM_HEAD_DIM), F32),
        ],
        compiler_params=pltpu.CompilerParams(
            dimension_semantics=("arbitrary", "arbitrary"), vmem_limit_bytes=VMEM_LIMIT_LARGE),
        name="mixers",
    )(x, g_pre.reshape(1, d), w_head, w_tail, conv_a_w, pool_w, pool_scale, gdn_conv_w, gdn_norm_g, ssm_conv_w,
      ssm_conv_b, ssm_d_x, ssm_norm_g, alog_row, dtbias_row)


def _xattn_kernel(x_ref, g_ref, wq_ref, kv_ref, o_ref, wb_ref):
    @pl.when((pl.program_id(0) == 0) & (pl.program_id(1) == 0))
    def _():
        wb_ref[...] = wq_ref[...].astype(BF16)

    xn = _rmsnorm(x_ref[...], g_ref[...]).astype(BF16)
    for h in range(XA_HEADS):
        lo = h * XA_HEAD_DIM
        q_h = jnp.dot(xn, wb_ref[:, lo:lo + XA_HEAD_DIM], preferred_element_type=F32)
        k_h = kv_ref[:, lo:lo + XA_HEAD_DIM]
        v_h = kv_ref[:, D_MODEL + lo:D_MODEL + lo + XA_HEAD_DIM]
        s = _dot_nt(q_h, k_h) * XA_HEAD_DIM ** -0.5
        e = jnp.exp(s - jnp.max(s, axis=-1, keepdims=True))
        p = e / jnp.sum(e, axis=-1, keepdims=True)
        o_ref[:, lo:lo + XA_HEAD_DIM] = _dot(p, v_h).astype(o_ref.dtype)


def _xattn(x, g, wq, layer, kv, batch, seq, tq):
    n, d = x.shape
    n_tiles = seq // tq
    return pl.pallas_call(
        _xattn_kernel,
        grid=(batch, n_tiles),
        in_specs=[
            pl.BlockSpec((tq, d), lambda b, i: (b * n_tiles + i, 0)),
            pl.BlockSpec((1, d), lambda b, i: (0, 0)),
            pl.BlockSpec((None, d, d), lambda b, i: (layer, 0, 0), pipeline_mode=pl.Buffered(1)),
            pl.BlockSpec((MEM_LEN, 2 * d), lambda b, i: (b, 0)),
        ],
        out_specs=pl.BlockSpec((tq, d), lambda b, i: (b * n_tiles + i, 0)),
        out_shape=jax.ShapeDtypeStruct((n, d), BF16),
        scratch_shapes=[pltpu.VMEM((d, d), BF16)],
        compiler_params=pltpu.CompilerParams(
            dimension_semantics=("arbitrary", "arbitrary"), vmem_limit_bytes=VMEM_LIMIT),
        name="xattn",
    )(x, g.reshape(1, d), wq, kv)


def _ffn_kernel(x_ref, gpre_ref, wg_ref, wu_ref, wd_ref, gpost_ref, o_ref, xn_ref):
    j = pl.program_id(1)
    last = pl.num_programs(1) - 1
    tm = o_ref.shape[0]

    def partial_out(rows):
        xn = xn_ref[rows, :]
        gate = jnp.dot(xn, wg_ref[...].astype(BF16), preferred_element_type=F32)
        up = jnp.dot(xn, wu_ref[...].astype(BF16), preferred_element_type=F32)
        return jnp.dot((_silu(gate) * up).astype(BF16), wd_ref[...].astype(BF16), preferred_element_type=F32)

    chunks = [slice(r, r + tm // FFN_EDGE_CHUNKS) for r in range(0, tm, tm // FFN_EDGE_CHUNKS)]

    @pl.when(j == 0)
    def _():
        for rows in chunks:
            xn_ref[rows, :] = _rmsnorm(x_ref[rows, :], gpre_ref[...]).astype(BF16)
            o_ref[rows, :] = partial_out(rows)

    @pl.when((j > 0) & (j < last))
    def _():
        o_ref[...] += partial_out(slice(0, tm))

    @pl.when(j == last)
    def _():
        for rows in chunks:
            y = o_ref[rows, :] + partial_out(rows)
            o_ref[rows, :] = x_ref[rows, :] + _rmsnorm(y, gpost_ref[...])


def _ffn(x, gpre, w_gu, w_down, layer, gpost, tm, tf):
    n, d = x.shape
    n_ff = D_FF // tf
    return pl.pallas_call(
        _ffn_kernel,
        grid=(n // tm, n_ff),
        in_specs=[
            pl.BlockSpec((tm, d), lambda i, j: (i, 0)),
            pl.BlockSpec((1, d), lambda i, j: (0, 0)),
            pl.BlockSpec((None, d, tf), lambda i, j: (layer, 0, j)),
            pl.BlockSpec((None, d, tf), lambda i, j: (layer, 0, j + n_ff)),
            pl.BlockSpec((None, tf, d), lambda i, j: (layer, j, 0)),
            pl.BlockSpec((1, d), lambda i, j: (0, 0)),
        ],
        out_specs=pl.BlockSpec((tm, d), lambda i, j: (i, 0)),
        out_shape=jax.ShapeDtypeStruct((n, d), F32),
        scratch_shapes=[pltpu.VMEM((tm, d), BF16)],
        compiler_params=pltpu.CompilerParams(
            dimension_semantics=("arbitrary", "arbitrary"), vmem_limit_bytes=VMEM_LIMIT_FFN),
        name="ffn",
    )(x, gpre.reshape(1, d), w_gu, w_gu, w_down, gpost.reshape(1, d))


def _small_row(gdn_vals, ssm_vals):
    row = jnp.zeros((1, 128), F32)
    row = row.at[0, SMALL_A:SMALL_A + GDN_HEADS].set(gdn_vals.astype(F32))
    return row.at[0, SMALL_DT:SMALL_DT + SSM_HEADS].set(ssm_vals.astype(F32))


def kernel(x, mem, norm_g, w_in, conv_a_w, pool_w, pool_scale, gdn_conv_w, gdn_A_log, gdn_dt_bias, gdn_norm_g,
           ssm_conv_w, ssm_conv_b, ssm_A_log, ssm_dt_bias, ssm_D, ssm_norm_g, w_out, xa_wq, xa_wkv, xa_wo,
           ffn_w_gu, ffn_w_down):
    batch, seq, d = x.shape
    depth = w_in.shape[0]
    xf = x.reshape(batch * seq, d)
    memf = mem.reshape(batch * MEM_LEN, d)
    w_in_t = jnp.swapaxes(w_in, 1, 2)
    for l in range(depth):
        g = norm_g[l]
        mixed = _mixers(
            xf, g[0], *_prep_w_in(w_in_t, l), batch, seq, conv_a_w[l], pool_w[l], pool_scale[l].reshape(1, -1),
            gdn_conv_w[l], gdn_norm_g[l].reshape(1, -1), ssm_conv_w[l], ssm_conv_b[l].reshape(1, -1),
            jnp.repeat(ssm_D[l], SSM_HEAD_DIM).reshape(1, -1), ssm_norm_g[l].reshape(1, -1),
            _small_row(gdn_A_log[l], ssm_A_log[l]), _small_row(gdn_dt_bias[l], ssm_dt_bias[l]),
            tile=MIX_TILE)
        xf = _proj_post(mixed, w_out, l, g[1], xf, tm=ROW_TILE)
        kv = _norm_matmul(memf, g[4], xa_wkv, l, 2 * d, BF16, tm=batch * MEM_LEN, tn=KV_COL_TILE)
        att = _xattn(xf, g[2], xa_wq, l, kv, batch, seq, tq=ROW_TILE)
        xf = _proj_post(att, xa_wo, l, g[3], xf, tm=ROW_TILE)
        xf = _ffn(xf, g[5], ffn_w_gu, ffn_w_down, l, g[6], tm=FFN_ROW_TILE, tf=FFN_COL_TILE)
    return xf.reshape(batch, seq, d)
```
